```python
import math
import jax, jax.numpy as jnp
from jax import lax
import numpy as np

D_MODEL = 2048
BATCH = 4
SEQ = 2048
DEPTH = 1

ATTN_HEADS = 8
ATTN_HEAD_DIM = 128
MOBA_BLOCK = 256
MOBA_TOPK = 3
MOBA_Q_CHUNK = 64
REL_BUCKETS = 32
REL_MAX_DIST = 128
RET_HEADS = 8
RET_KEY_DIM = 128
RET_VAL_DIM = 256
RET_CHUNK = 128
ROPE_BASE = 10000.0
FFN_DIM = 5632
CONV_WIDTH = 3
EPS = 1e-6

ATTN_WIDTH = ATTN_HEADS * ATTN_HEAD_DIM
RET_QK_WIDTH = RET_HEADS * RET_KEY_DIM
RET_V_WIDTH = RET_HEADS * RET_VAL_DIM
IN_SPLITS = (ATTN_WIDTH, ATTN_WIDTH, ATTN_WIDTH, RET_QK_WIDTH, RET_QK_WIDTH,
             RET_V_WIDTH, RET_V_WIDTH, D_MODEL, D_MODEL)
IN_WIDTH = sum(IN_SPLITS)
N_MOD = 6

kernel_name = "hybrid_moba_retention_block"


def rms_norm(x, g):
    xf = x.astype(jnp.float32)
    y = xf * lax.rsqrt(jnp.mean(xf * xf, axis=-1, keepdims=True) + EPS)
    return (y * g.astype(jnp.float32)).astype(x.dtype)


def t5_bucket(dist):
    n = jnp.maximum(dist, 0)
    max_exact = REL_BUCKETS // 2
    nf = jnp.maximum(n, 1).astype(jnp.float32)
    large = max_exact + (jnp.log(nf / max_exact) / math.log(REL_MAX_DIST / max_exact)
                         * (REL_BUCKETS - max_exact)).astype(jnp.int32)
    large = jnp.minimum(large, REL_BUCKETS - 1)
    return jnp.where(n < max_exact, n, large)


def rotary(x, pos):
    half = x.shape[-1] // 2
    freqs = jnp.power(ROPE_BASE, -jnp.arange(half, dtype=jnp.float32) / half)
    ang = pos.astype(jnp.float32)[:, None] * freqs[None, :]
    cos, sin = jnp.cos(ang), jnp.sin(ang)
    xf = x.astype(jnp.float32)
    x1, x2 = xf[..., :half], xf[..., half:]
    return jnp.concatenate([x1 * cos - x2 * sin, x2 * cos + x1 * sin], axis=-1).astype(x.dtype)


def moba_attention(q, k, v, rel_bias):
    B, H, S, hd = q.shape
    nb = -(-S // MOBA_BLOCK)
    s_pad = nb * MOBA_BLOCK
    ksel = min(MOBA_TOPK, nb)
    pad = ((0, 0), (0, 0), (0, s_pad - S), (0, 0))
    k_pad = jnp.pad(k, pad)
    v_pad = jnp.pad(v, pad)
    k_blocks = k_pad.reshape(B, H, nb, MOBA_BLOCK, hd)
    v_blocks = v_pad.reshape(B, H, nb, MOBA_BLOCK, hd)
    k_mean = jnp.mean(k_blocks.astype(jnp.float32), axis=3)
    table = rel_bias.T.astype(jnp.float32)
    scale = hd ** -0.5
    b_idx = jnp.arange(B)[:, None, None, None]
    h_idx = jnp.arange(H)[None, :, None, None]
    h_idx5 = jnp.arange(H)[None, :, None, None, None]
    blk = jnp.arange(MOBA_BLOCK)

    def chunk(ci):
        start = ci * MOBA_Q_CHUNK
        qc = lax.dynamic_slice_in_dim(q, start, MOBA_Q_CHUNK, axis=2)
        q_pos = start + jnp.arange(MOBA_Q_CHUNK)
        cur = start // MOBA_BLOCK
        gate = jnp.einsum('bhqd,bhnd->bhqn', qc.astype(jnp.float32), k_mean)
        gate = jnp.where(jnp.arange(nb) < cur, gate, -jnp.inf)
        _, idx = lax.top_k(gate, ksel)
        sel_ok = jnp.arange(ksel) < cur
        kg = k_blocks[b_idx, h_idx, idx]
        vg = v_blocks[b_idx, h_idx, idx]
        k_pos = idx[..., None] * MOBA_BLOCK + blk
        bias_sel = table[h_idx5, t5_bucket(q_pos[None, None, :, None, None] - k_pos)]
        s_sel = jnp.einsum('bhqd,bhqnkd->bhqnk', qc, kg).astype(jnp.float32) * scale + bias_sel
        s_sel = jnp.where(sel_ok[:, None], s_sel, -jnp.inf)
        ko = lax.dynamic_slice_in_dim(k_pad, cur * MOBA_BLOCK, MOBA_BLOCK, axis=2)
        vo = lax.dynamic_slice_in_dim(v_pad, cur * MOBA_BLOCK, MOBA_BLOCK, axis=2)
        dist_own = q_pos[:, None] - (cur * MOBA_BLOCK + blk)[None, :]
        s_own = (jnp.einsum('bhqd,bhkd->bhqk', qc, ko).astype(jnp.float32) * scale
                 + table[:, t5_bucket(dist_own)])
        s_own = jnp.where(dist_own >= 0, s_own, -jnp.inf)
        logits = jnp.concatenate(
            [s_sel.reshape(B, H, MOBA_Q_CHUNK, ksel * MOBA_BLOCK), s_own], axis=-1)
        p = jax.nn.softmax(logits, axis=-1).astype(v.dtype)
        p_sel = p[..., :ksel * MOBA_BLOCK].reshape(B, H, MOBA_Q_CHUNK, ksel, MOBA_BLOCK)
        p_own = p[..., ksel * MOBA_BLOCK:]
        return (jnp.einsum('bhqnk,bhqnkd->bhqd', p_sel, vg)
                + jnp.einsum('bhqk,bhkd->bhqd', p_own, vo))

    out = lax.map(chunk, jnp.arange(S // MOBA_Q_CHUNK))
    return out.transpose(1, 2, 0, 3, 4).reshape(B, H, S, hd)


def retention(q, k, v):
    B, H, S, dk = q.shape
    dv = v.shape[-1]
    C = RET_CHUNK
    n = S // C
    dt = q.dtype
    log_decay = jnp.log(1.0 - jnp.power(2.0, -5.0 - jnp.arange(H, dtype=jnp.float32)))
    i = jnp.arange(C, dtype=jnp.float32)
    diff = i[:, None] - i[None, :]
    ld = log_decay[:, None, None]
    inner_decay = jnp.where(diff >= 0, jnp.exp(ld * jnp.maximum(diff, 0.0)), 0.0)
    q_decay = jnp.exp(log_decay[:, None] * (i + 1.0))
    k_decay = jnp.exp(log_decay[:, None] * (C - 1.0 - i))
    chunk_decay = jnp.exp(log_decay * C).astype(dt)[None, :, None, None]
    qc = q.reshape(B, H, n, C, dk)
    kc = k.reshape(B, H, n, C, dk)
    vc = v.reshape(B, H, n, C, dv)
    scores = jnp.einsum('bhnid,bhnjd->bhnij', qc, kc) * inner_decay[:, None].astype(dt)
    inner = jnp.einsum('bhnij,bhnje->bhnie', scores, vc)
    kv = jnp.einsum('bhnjd,bhnje->nbhde', kc * k_decay[:, None, :, None].astype(dt), vc)

    def step(state, kv_n):
        return chunk_decay * state + kv_n, state

    _, prev = lax.scan(step, jnp.zeros((B, H, dk, dv), kv.dtype), kv)
    cross = jnp.einsum('bhnid,nbhde->bhnie', qc * q_decay[:, None, :, None].astype(dt), prev)
    return (inner + cross).reshape(B, H, S, dv)


def head_group_norm(y, g):
    B, H, S, dv = y.shape
    yf = y.astype(jnp.float32)
    mu = jnp.mean(yf, axis=-1, keepdims=True)
    var = jnp.mean(jnp.square(yf - mu), axis=-1, keepdims=True)
    yn = ((yf - mu) * lax.rsqrt(var + EPS)).transpose(0, 2, 1, 3).reshape(B, S, H * dv)
    return (yn * g.astype(jnp.float32)).astype(y.dtype)


def causal_depthwise_conv(u, w, b):
    C = u.shape[-1]
    y = lax.conv_general_dilated(u, w[:, None, :].astype(u.dtype), window_strides=(1,),
                                 padding=[(CONV_WIDTH - 1, 0)],
                                 dimension_numbers=('NWC', 'WIO', 'NWC'),
                                 feature_group_count=C)
    return y + b.astype(u.dtype)


def setup_inputs(seed: int = 0) -> dict:
    key = jax.random.key(seed)
    ks = jax.random.split(key, 18)
    f32 = jnp.float32
    L = DEPTH

    def nrm(k, shape, scale):
        return jax.random.normal(k, shape, f32) * scale

    return {
        "x": nrm(ks[0], (BATCH, SEQ, D_MODEL), 1.0),
        "c": nrm(ks[1], (BATCH, D_MODEL), 1.0),
        "w_ada": nrm(ks[2], (L, D_MODEL, N_MOD * D_MODEL), D_MODEL ** -0.5),
        "b_ada": nrm(ks[3], (L, N_MOD * D_MODEL), 0.01),
        "norm1_g": 1.0 + nrm(ks[4], (L, D_MODEL), 0.02),
        "w_in": nrm(ks[5], (L, D_MODEL, IN_WIDTH), D_MODEL ** -0.5),
        "q_norm_g": 1.0 + nrm(ks[6], (L, ATTN_HEAD_DIM), 0.02),
        "k_norm_g": 1.0 + nrm(ks[7], (L, ATTN_HEAD_DIM), 0.02),
        "rel_bias": nrm(ks[8], (REL_BUCKETS, ATTN_HEADS), 0.3),
        "ret_norm_g": 1.0 + nrm(ks[9], (L, RET_V_WIDTH), 0.02),
        "w_attn_br": nrm(ks[10], (L, ATTN_WIDTH, D_MODEL), ATTN_WIDTH ** -0.5),
        "w_ret_br": nrm(ks[11], (L, RET_V_WIDTH, D_MODEL), RET_V_WIDTH ** -0.5),
        "w_o": nrm(ks[12], (L, D_MODEL, D_MODEL), D_MODEL ** -0.5),
        "norm2_g": 1.0 + nrm(ks[13], (L, D_MODEL), 0.02),
        "w_up": nrm(ks[14], (L, D_MODEL, 2 * FFN_DIM), D_MODEL ** -0.5),
        "conv_w": nrm(ks[15], (L, CONV_WIDTH, 2 * FFN_DIM), CONV_WIDTH ** -0.5),
        "conv_b": nrm(ks[16], (L, 2 * FFN_DIM), 0.01),
        "w_down": nrm(ks[17], (L, FFN_DIM, D_MODEL), FFN_DIM ** -0.5),
    }


def reference(x, c, w_ada, b_ada, norm1_g, w_in, q_norm_g, k_norm_g, rel_bias,
              ret_norm_g, w_attn_br, w_ret_br, w_o, norm2_g, w_up, conv_w, conv_b, w_down):
    B, S, D = x.shape
    pos = jnp.arange(S)
    split_at = [int(s) for s in np.cumsum(IN_SPLITS)[:-1]]

    def heads(t, n_heads, hd):
        return t.reshape(B, S, n_heads, hd).transpose(0, 2, 1, 3)

    for layer in range(DEPTH):
        mod = jax.nn.silu(c) @ w_ada[layer] + b_ada[layer]
        shift1, scale1, gate1, shift2, scale2, gate2 = jnp.split(mod, N_MOD, axis=-1)

        h = rms_norm(x, norm1_g[layer]) * (1.0 + scale1[:, None, :]) + shift1[:, None, :]
        proj = h @ w_in[layer]
        qa, ka, va, qr, kr, vr, gr, ga_logit, gb_logit = jnp.split(proj, split_at, axis=-1)

        qa = rms_norm(heads(qa, ATTN_HEADS, ATTN_HEAD_DIM), q_norm_g[layer])
        ka = rms_norm(heads(ka, ATTN_HEADS, ATTN_HEAD_DIM), k_norm_g[layer])
        va = heads(va, ATTN_HEADS, ATTN_HEAD_DIM)
        ya = moba_attention(qa, ka, va, rel_bias)
        ya = ya.transpose(0, 2, 1, 3).reshape(B, S, ATTN_WIDTH) @ w_attn_br[layer]

        qr = rotary(heads(qr, RET_HEADS, RET_KEY_DIM), pos)
        kr = rotary(heads(kr, RET_HEADS, RET_KEY_DIM), pos) * (RET_KEY_DIM ** -0.5)
        vr = heads(vr, RET_HEADS, RET_VAL_DIM)
        yr = retention(qr, kr, vr)
        yr = (head_group_norm(yr, ret_norm_g[layer]) * jax.nn.silu(gr)) @ w_ret_br[layer]

        merged = jax.nn.sigmoid(ga_logit) * ya + jax.nn.sigmoid(gb_logit) * yr
        x = x + gate1[:, None, :] * (merged @ w_o[layer])

        h2 = rms_norm(x, norm2_g[layer]) * (1.0 + scale2[:, None, :]) + shift2[:, None, :]
        u = causal_depthwise_conv(h2 @ w_up[layer], conv_w[layer], conv_b[layer])
        val, gt = jnp.split(u, 2, axis=-1)
        x = x + gate2[:, None, :] * ((jax.nn.silu(gt) * val) @ w_down[layer])
    return x
```

```python
import functools
import math

import numpy as np
import jax
import jax.numpy as jnp
from jax import lax
from jax.experimental import pallas as pl
from jax.experimental.pallas import tpu as pltpu

F32 = jnp.float32
BF16 = jnp.bfloat16

D_MODEL = 2048
BATCH = 4
SEQ = 2048
ATTN_HEADS = 8
ATTN_HEAD_DIM = 128
MOBA_BLOCK = 256
MOBA_TOPK = 3
REL_BUCKETS = 32
REL_MAX_DIST = 128
RET_HEADS = 8
RET_KEY_DIM = 128
RET_VAL_DIM = 256
RET_CHUNK = 128
ROPE_BASE = 10000.0
FFN_DIM = 5632
CONV_WIDTH = 3
EPS = 1e-6
N_MOD = 6

ATTN_WIDTH = ATTN_HEADS * ATTN_HEAD_DIM
RET_QK_WIDTH = RET_HEADS * RET_KEY_DIM
RET_V_WIDTH = RET_HEADS * RET_VAL_DIM
OFF_QA = 0
OFF_KA = OFF_QA + ATTN_WIDTH
OFF_VA = OFF_KA + ATTN_WIDTH
OFF_QR = OFF_VA + ATTN_WIDTH
OFF_KR = OFF_QR + RET_QK_WIDTH
OFF_VR = OFF_KR + RET_QK_WIDTH
OFF_GR = OFF_VR + RET_V_WIDTH
OFF_GA = OFF_GR + RET_V_WIDTH
OFF_GB = OFF_GA + D_MODEL
IN_WIDTH = OFF_GB + D_MODEL

TOKENS = BATCH * SEQ
N_BLOCKS = SEQ // MOBA_BLOCK
N_CHUNKS = SEQ // RET_CHUNK
MASK_VALUE = -1e30
MIB = 1024 * 1024

SHIFT1, SCALE1, GATE1, SHIFT2, SCALE2, GATE2 = range(N_MOD)

MOD_TN = 1024
INPROJ_TM, INPROJ_TN = 1024, 1024
MIX_TM, MIX_TN = 512, 512
FFN_TM, FFN_TN = 512, 512
FFN_HALO = 16
NORM_ROWS = 64


def _params(semantics, vmem_bytes):
    return pltpu.CompilerParams(dimension_semantics=semantics,
                                vmem_limit_bytes=int(vmem_bytes))


def _sigmoid(v):
    return 1.0 / (1.0 + jnp.exp(-v))


def _silu(v):
    return v / (1.0 + jnp.exp(-v))


def _rms_mod(x, g, scale, shift):
    ms = jnp.mean(x * x, axis=-1, keepdims=True)
    return ((x * lax.rsqrt(ms + EPS)) * g) * (1.0 + scale) + shift


def _mod_kernel(c_ref, w_ref, b_ref, o_ref):
    s = _silu(c_ref[...])
    o_ref[...] = jnp.dot(s, w_ref[...], preferred_element_type=F32,
                         precision=lax.Precision.HIGHEST) + b_ref[...]


def _modulation(c_pad, w_ada, b_ada):
    rows = c_pad.shape[0]
    n = w_ada.shape[1]
    return pl.pallas_call(
        _mod_kernel,
        grid=(n // MOD_TN,),
        in_specs=[pl.BlockSpec((rows, D_MODEL), lambda j: (0, 0)),
                  pl.BlockSpec((D_MODEL, MOD_TN), lambda j: (0, j)),
                  pl.BlockSpec((1, MOD_TN), lambda j: (0, j))],
        out_specs=pl.BlockSpec((rows, MOD_TN), lambda j: (0, j)),
        out_shape=jax.ShapeDtypeStruct((rows, n), F32),
        compiler_params=_params(("arbitrary",), 2 * D_MODEL * MOD_TN * 4 + 8 * MIB),
        name="mod",
    )(c_pad, w_ada, b_ada)


def _inproj_kernel(x_ref, mod_ref, g_ref, w_ref, o_ref, h_ref):
    @pl.when(pl.program_id(1) == 0)
    def _():
        g = g_ref[...]
        shift = mod_ref[0, SHIFT1:SHIFT1 + 1, :]
        scale = mod_ref[0, SCALE1:SCALE1 + 1, :]

        def body(r, carry):
            rows = pl.ds(pl.multiple_of(r * NORM_ROWS, NORM_ROWS), NORM_ROWS)
            h_ref[rows, :] = _rms_mod(x_ref[rows, :], g, scale, shift).astype(BF16)
            return carry

        lax.fori_loop(0, INPROJ_TM // NORM_ROWS, body, 0)

    o_ref[...] = jnp.dot(h_ref[...], w_ref[...],
                         preferred_element_type=F32).astype(o_ref.dtype)


def _input_projection(x2d, mod3, norm_g, w_in):
    tm, tn = INPROJ_TM, INPROJ_TN
    vmem = (2 * tm * D_MODEL * 4 + 2 * D_MODEL * tn * 2 + 2 * tm * tn * 2
            + tm * D_MODEL * 2 + tm * tn * 4 + 4 * MIB)
    return pl.pallas_call(
        _inproj_kernel,
        grid=(TOKENS // tm, IN_WIDTH // tn),
        in_specs=[pl.BlockSpec((tm, D_MODEL), lambda i, j: (i, 0)),
                  pl.BlockSpec((1, N_MOD, D_MODEL), lambda i, j: (i // (SEQ // tm), 0, 0)),
                  pl.BlockSpec((1, D_MODEL), lambda i, j: (0, 0)),
                  pl.BlockSpec((D_MODEL, tn), lambda i, j: (0, j))],
        out_specs=pl.BlockSpec((tm, tn), lambda i, j: (i, j)),
        out_shape=jax.ShapeDtypeStruct((TOKENS, IN_WIDTH), BF16),
        scratch_shapes=[pltpu.VMEM((tm, D_MODEL), BF16)],
        compiler_params=_params(("arbitrary", "arbitrary"), vmem),
        name="inproj",
    )(x2d, mod3, norm_g, w_in)


def _relbias_kernel(rb_ref, o_ref):
    h = pl.program_id(0)
    shape = (2 * MOBA_BLOCK, MOBA_BLOCK)
    key = lax.broadcasted_iota(jnp.int32, shape, 0)
    qry = lax.broadcasted_iota(jnp.int32, shape, 1)
    dist = qry - key + MOBA_BLOCK
    n = jnp.maximum(dist, 0)
    max_exact = REL_BUCKETS // 2
    nf = jnp.maximum(n, 1).astype(F32)
    large = max_exact + (jnp.log(nf / max_exact) / math.log(REL_MAX_DIST / max_exact)
                         * (REL_BUCKETS - max_exact)).astype(jnp.int32)
    large = jnp.minimum(large, REL_BUCKETS - 1)
    bucket = jnp.where(n < max_exact, n, large)
    bias = jnp.zeros(shape, F32)
    for b in range(REL_BUCKETS):
        bias = jnp.where(bucket == b, rb_ref[b, h], bias)
    o_ref[0] = jnp.where(dist >= 0, bias, MASK_VALUE)


def _relbias_tiles(rel_bias):
    return pl.pallas_call(
        _relbias_kernel,
        grid=(ATTN_HEADS,),
        in_specs=[pl.BlockSpec(memory_space=pltpu.SMEM)],
        out_specs=pl.BlockSpec((1, 2 * MOBA_BLOCK, MOBA_BLOCK), lambda h: (h, 0, 0)),
        out_shape=jax.ShapeDtypeStruct((ATTN_HEADS, 2 * MOBA_BLOCK, MOBA_BLOCK), F32),
        compiler_params=_params(("arbitrary",), 16 * MIB),
        name="relbias",
    )(rel_bias)


def _far_bucket_is_last():
    d = np.arange(MOBA_BLOCK + 1, SEQ, dtype=np.float32)
    max_exact = REL_BUCKETS // 2
    large = max_exact + (np.log(d / max_exact) / math.log(REL_MAX_DIST / max_exact)
                         * (REL_BUCKETS - max_exact)).astype(np.int32)
    return bool(np.all(np.minimum(large, REL_BUCKETS - 1) == REL_BUCKETS - 1))


def _moba_kernel(rb_ref, q_ref, k_ref, v_ref, gq_ref, gk_ref, bt_ref, o_ref,
                 qb_ref, kb_ref, vt_ref, s_ref):
    h = pl.program_id(1)
    far_bias = rb_ref[REL_BUCKETS - 1, h]
    blk = MOBA_BLOCK
    nt = (((1,), (1,)), ((), ()))

    q = q_ref[0].astype(F32)
    k = k_ref[0].astype(F32)
    qn = (q * lax.rsqrt(jnp.mean(q * q, axis=-1, keepdims=True) + EPS)) * gq_ref[...]
    kn = (k * lax.rsqrt(jnp.mean(k * k, axis=-1, keepdims=True) + EPS)) * gk_ref[...]
    qb_ref[...] = qn.astype(BF16)
    kb_ref[...] = kn.astype(BF16)
    vt_ref[...] = v_ref[0].astype(F32).T.astype(BF16)

    k_mean = jnp.concatenate(
        [jnp.sum(kn[n * blk:(n + 1) * blk], axis=0, keepdims=True) for n in range(N_BLOCKS)],
        axis=0) * (1.0 / blk)
    gate = lax.dot_general(k_mean, qn, nt, preferred_element_type=F32,
                           precision=lax.Precision.HIGHEST)
    row = lax.broadcasted_iota(jnp.int32, gate.shape, 0)
    cur = lax.broadcasted_iota(jnp.int32, gate.shape, 1) // blk
    rank = jnp.zeros(gate.shape, jnp.int32)
    for m in range(N_BLOCKS):
        gm = gate[m:m + 1, :]
        beats = (m < cur) & ((gm > gate) | ((gm == gate) & (m < row)))
        rank = rank + beats.astype(jnp.int32)
    sel = ((row < cur) & (rank < MOBA_TOPK)).astype(F32)

    scale = ATTN_HEAD_DIM ** -0.5
    for qi in range(N_BLOCKS):
        cols = slice(qi * blk, (qi + 1) * blk)
        nk = (qi + 1) * blk
        s_all = lax.dot_general(kb_ref[0:nk, :], qb_ref[cols, :], nt,
                                preferred_element_type=F32) * scale
        for n in range(qi + 1):
            s_blk = s_all[n * blk:(n + 1) * blk]
            if n == qi:
                s_blk = s_blk + bt_ref[0, blk:2 * blk, :]
            else:
                if n == qi - 1:
                    s_blk = s_blk + bt_ref[0, 0:blk, :]
                else:
                    s_blk = s_blk + far_bias
                s_blk = jnp.where(sel[n:n + 1, cols] > 0.5, s_blk, MASK_VALUE)
            s_ref[n * blk:(n + 1) * blk, :] = s_blk
        s = s_ref[0:nk, :]
        m = jnp.max(s, axis=0, keepdims=True)
        p = jnp.exp(s - m)
        l = jnp.sum(p, axis=0, keepdims=True)
        o_t = jnp.dot(vt_ref[:, 0:nk], p.astype(BF16), preferred_element_type=F32)
        o_t = o_t * (1.0 / l)
        o_ref[0, cols, :] = o_t.T.astype(o_ref.dtype)


def _moba(proj3, rel_bias, q_norm_g, k_norm_g, bias_tiles):
    hd = ATTN_HEAD_DIM
    head = lambda off: (lambda b, h: (b, 0, off // hd + h))
    return pl.pallas_call(
        _moba_kernel,
        grid=(BATCH, ATTN_HEADS),
        in_specs=[pl.BlockSpec(memory_space=pltpu.SMEM),
                  pl.BlockSpec((1, SEQ, hd), head(OFF_QA)),
                  pl.BlockSpec((1, SEQ, hd), head(OFF_KA)),
                  pl.BlockSpec((1, SEQ, hd), head(OFF_VA)),
                  pl.BlockSpec((1, hd), lambda b, h: (0, 0)),
                  pl.BlockSpec((1, hd), lambda b, h: (0, 0)),
                  pl.BlockSpec((1, 2 * MOBA_BLOCK, MOBA_BLOCK), lambda b, h: (h, 0, 0))],
        out_specs=pl.BlockSpec((1, SEQ, hd), lambda b, h: (b, 0, h)),
        out_shape=jax.ShapeDtypeStruct((BATCH, SEQ, ATTN_WIDTH), BF16),
        scratch_shapes=[pltpu.VMEM((SEQ, hd), BF16),
                        pltpu.VMEM((SEQ, hd), BF16),
                        pltpu.VMEM((hd, SEQ), BF16),
                        pltpu.VMEM((SEQ, MOBA_BLOCK), F32)],
        compiler_params=_params(("arbitrary", "arbitrary"), 40 * MIB),
        name="moba",
    )(rel_bias, proj3, proj3, proj3, q_norm_g, k_norm_g, bias_tiles)


def _retention_kernel(cd_ref, q_ref, k_ref, v_ref, gr_ref, cos_ref, sin_ref,
                      dmask_ref, qdec_ref, kdec_ref, gn_ref, o_ref, qf_ref, kf_ref):
    h = pl.program_id(1)
    chunk_decay = cd_ref[h]
    half = RET_KEY_DIM // 2
    cos = cos_ref[...]
    sin = sin_ref[...]
    q = q_ref[0].astype(F32)
    k = k_ref[0].astype(F32)
    qf_ref[...] = q * cos + pltpu.roll(q, half, 1) * sin
    kf_ref[...] = (k * cos + pltpu.roll(k, half, 1) * sin) * (RET_KEY_DIM ** -0.5)

    nt = (((1,), (1,)), ((), ()))
    state = jnp.zeros((RET_KEY_DIM, RET_VAL_DIM), F32)
    for c in range(N_CHUNKS):
        rows = slice(c * RET_CHUNK, (c + 1) * RET_CHUNK)
        qc = qf_ref[rows, :]
        kc = kf_ref[rows, :]
        vc = v_ref[0, rows, :]
        scores = lax.dot_general(qc.astype(BF16), kc.astype(BF16), nt,
                                 preferred_element_type=F32) * dmask_ref[0]
        inner = jnp.dot(scores.astype(BF16), vc, preferred_element_type=F32)
        cross = jnp.dot((qc * qdec_ref[0]).astype(BF16), state.astype(BF16),
                        preferred_element_type=F32)
        y = inner + cross
        kd_t = (kc * kdec_ref[0]).T.astype(BF16)
        state = chunk_decay * state + jnp.dot(kd_t, vc, preferred_element_type=F32)

        mu = jnp.mean(y, axis=-1, keepdims=True)
        yc = y - mu
        var = jnp.mean(yc * yc, axis=-1, keepdims=True)
        yn = (yc * lax.rsqrt(var + EPS)) * gn_ref[...]
        o_ref[0, rows, :] = (yn * _silu(gr_ref[0, rows, :].astype(F32))).astype(o_ref.dtype)


def _retention_tables():
    half = RET_KEY_DIM // 2
    freqs = jnp.power(ROPE_BASE, -jnp.arange(half, dtype=F32) / half)
    ang = jnp.arange(SEQ).astype(F32)[:, None] * freqs[None, :]
    cos, sin = jnp.cos(ang), jnp.sin(ang)
    cos_full = jnp.concatenate([cos, cos], axis=-1)
    sin_signed = jnp.concatenate([-sin, sin], axis=-1)

    log_decay = jnp.log(1.0 - jnp.power(2.0, -5.0 - jnp.arange(RET_HEADS, dtype=F32)))
    i = jnp.arange(RET_CHUNK, dtype=F32)
    diff = i[:, None] - i[None, :]
    ld = log_decay[:, None, None]
    inner_decay = jnp.where(diff >= 0, jnp.exp(ld * jnp.maximum(diff, 0.0)), 0.0)
    q_decay = jnp.exp(log_decay[:, None] * (i + 1.0))
    k_decay = jnp.exp(log_decay[:, None] * (RET_CHUNK - 1.0 - i))
    chunk_decay = jnp.exp(log_decay * RET_CHUNK)
    bcast = lambda t: jnp.broadcast_to(t[:, :, None], (RET_HEADS, RET_CHUNK, RET_KEY_DIM))
    return cos_full, sin_signed, inner_decay, bcast(q_decay), bcast(k_decay), chunk_decay


def _retention(proj3, ret_norm_g):
    dk, dv = RET_KEY_DIM, RET_VAL_DIM
    cos, sin, dmask, qdec, kdec, chunk_decay = _retention_tables()
    head = lambda off, w: (lambda b, h: (b, 0, off // w + h))
    per_head = lambda b, h: (h, 0, 0)
    return pl.pallas_call(
        _retention_kernel,
        grid=(BATCH, RET_HEADS),
        in_specs=[pl.BlockSpec(memory_space=pltpu.SMEM),
                  pl.BlockSpec((1, SEQ, dk), head(OFF_QR, dk)),
                  pl.BlockSpec((1, SEQ, dk), head(OFF_KR, dk)),
                  pl.BlockSpec((1, SEQ, dv), head(OFF_VR, dv)),
                  pl.BlockSpec((1, SEQ, dv), head(OFF_GR, dv)),
                  pl.BlockSpec((SEQ, dk), lambda b, h: (0, 0)),
                  pl.BlockSpec((SEQ, dk), lambda b, h: (0, 0)),
                  pl.BlockSpec((1, RET_CHUNK, RET_CHUNK), per_head),
                  pl.BlockSpec((1, RET_CHUNK, dk), per_head),
                  pl.BlockSpec((1, RET_CHUNK, dk), per_head),
                  pl.BlockSpec((1, dv), lambda b, h: (0, h))],
        out_specs=pl.BlockSpec((1, SEQ, dv), lambda b, h: (b, 0, h)),
        out_shape=jax.ShapeDtypeStruct((BATCH, SEQ, RET_V_WIDTH), BF16),
        scratch_shapes=[pltpu.VMEM((SEQ, dk), F32), pltpu.VMEM((SEQ, dk), F32)],
        compiler_params=_params(("arbitrary", "arbitrary"), 40 * MIB),
        name="retention",
    )(chunk_decay, proj3, proj3, proj3, proj3, cos, sin, dmask, qdec, kdec, ret_norm_g)


def _mixer_kernel(ya_ref, yr_ref, ga_ref, gb_ref, wa_ref, wr_ref, wo_ref, x_ref, mod_ref,
                  o_ref):
    j = pl.program_id(1)
    a = jnp.dot(ya_ref[...], wa_ref[...], preferred_element_type=F32)
    r = jnp.dot(yr_ref[...], wr_ref[...], preferred_element_type=F32)
    merged = (_sigmoid(ga_ref[...].astype(F32)) * a
              + _sigmoid(gb_ref[...].astype(F32)) * r)
    part = jnp.dot(merged.astype(BF16), wo_ref[...], preferred_element_type=F32)

    @pl.when(j == 0)
    def _():
        o_ref[...] = part

    @pl.when(j > 0)
    def _():
        o_ref[...] += part

    @pl.when(j == pl.num_programs(1) - 1)
    def _():
        o_ref[...] = x_ref[...] + mod_ref[0, GATE1:GATE1 + 1, :] * o_ref[...]


def _mixer(ya2d, yr2d, proj2d, w_attn_br, w_ret_br, w_o, x2d, mod3):
    tm, tn = MIX_TM, MIX_TN
    vmem = (2 * tm * (ATTN_WIDTH + RET_V_WIDTH + 2 * tn) * 2
            + 2 * (ATTN_WIDTH + RET_V_WIDTH + D_MODEL) * tn * 2
            + 4 * tm * D_MODEL * 4 + tm * D_MODEL * 4 + 6 * tm * tn * 4 + 4 * MIB)
    return pl.pallas_call(
        _mixer_kernel,
        grid=(TOKENS // tm, D_MODEL // tn),
        in_specs=[pl.BlockSpec((tm, ATTN_WIDTH), lambda i, j: (i, 0)),
                  pl.BlockSpec((tm, RET_V_WIDTH), lambda i, j: (i, 0)),
                  pl.BlockSpec((tm, tn), lambda i, j: (i, OFF_GA // tn + j)),
                  pl.BlockSpec((tm, tn), lambda i, j: (i, OFF_GB // tn + j)),
                  pl.BlockSpec((ATTN_WIDTH, tn), lambda i, j: (0, j)),
                  pl.BlockSpec((RET_V_WIDTH, tn), lambda i, j: (0, j)),
                  pl.BlockSpec((tn, D_MODEL), lambda i, j: (j, 0)),
                  pl.BlockSpec((tm, D_MODEL), lambda i, j: (i, 0)),
                  pl.BlockSpec((1, N_MOD, D_MODEL), lambda i, j: (i // (SEQ // tm), 0, 0))],
        out_specs=pl.BlockSpec((tm, D_MODEL), lambda i, j: (i, 0)),
        out_shape=jax.ShapeDtypeStruct((TOKENS, D_MODEL), F32),
        compiler_params=_params(("arbitrary", "arbitrary"), vmem),
        name="mixer",
    )(ya2d, yr2d, proj2d, proj2d, w_attn_br, w_ret_br, w_o, x2d, mod3)


def _ffn_kernel(x_ref, halo_ref, mod_ref, g_ref, wv_ref, wg_ref, cwv_ref, cwg_ref,
                cbv_ref, cbg_ref, wd_ref, o_ref, h_ref, u_ref):
    i = pl.program_id(0)
    j = pl.program_id(1)
    tm, halo = FFN_TM, FFN_HALO

    @pl.when(j == 0)
    def _():
        g = g_ref[...]
        shift = mod_ref[0, SHIFT2:SHIFT2 + 1, :]
        scale = mod_ref[0, SCALE2:SCALE2 + 1, :]
        seq_start = (i % (SEQ // tm)) == 0
        h_halo = _rms_mod(halo_ref[...], g, scale, shift)
        h_ref[0:halo, :] = jnp.where(seq_start, 0.0, h_halo).astype(BF16)

        def body(r, carry):
            src = pl.ds(pl.multiple_of(r * NORM_ROWS, NORM_ROWS), NORM_ROWS)
            dst = pl.ds(pl.multiple_of(halo + r * NORM_ROWS, halo), NORM_ROWS)
            h_ref[dst, :] = _rms_mod(x_ref[src, :], g, scale, shift).astype(BF16)
            return carry

        lax.fori_loop(0, tm // NORM_ROWS, body, 0)

    def conv(half, w_ref, cw_ref, cb_ref):
        u_ref[half] = jnp.dot(h_ref[...], w_ref[...], preferred_element_type=F32)
        y = cb_ref[...]
        for t in range(CONV_WIDTH):
            lag = CONV_WIDTH - 1 - t
            y = y + cw_ref[0, t:t + 1, :] * u_ref[half, halo - lag:halo - lag + tm, :]
        return y

    val = conv(0, wv_ref, cwv_ref, cbv_ref)
    gt = conv(1, wg_ref, cwg_ref, cbg_ref)
    act = (_silu(gt) * val).astype(BF16)
    part = jnp.dot(act, wd_ref[...], preferred_element_type=F32)

    @pl.when(j == 0)
    def _():
        o_ref[...] = part

    @pl.when(j > 0)
    def _():
        o_ref[...] += part

    @pl.when(j == pl.num_programs(1) - 1)
    def _():
        o_ref[...] = x_ref[...] + mod_ref[0, GATE2:GATE2 + 1, :] * o_ref[...]


def _ffn(x1, mod3, norm_g, w_up, conv_w, conv_b, w_down):
    tm, tn, halo = FFN_TM, FFN_TN, FFN_HALO
    nj = FFN_DIM // tn
    vmem = (4 * tm * D_MODEL * 4 + 2 * halo * D_MODEL * 4
            + 2 * 3 * D_MODEL * tn * 2
            + (tm + halo) * D_MODEL * 2 + 2 * (tm + halo) * tn * 4
            + tm * D_MODEL * 4 + 8 * tm * tn * 4 + 4 * MIB)
    return pl.pallas_call(
        _ffn_kernel,
        grid=(TOKENS // tm, nj),
        in_specs=[pl.BlockSpec((tm, D_MODEL), lambda i, j: (i, 0)),
                  pl.BlockSpec((halo, D_MODEL),
                               lambda i, j: (jnp.maximum(i * (tm // halo) - 1, 0), 0)),
                  pl.BlockSpec((1, N_MOD, D_MODEL), lambda i, j: (i // (SEQ // tm), 0, 0)),
                  pl.BlockSpec((1, D_MODEL), lambda i, j: (0, 0)),
                  pl.BlockSpec((D_MODEL, tn), lambda i, j: (0, j)),
                  pl.BlockSpec((D_MODEL, tn), lambda i, j: (0, nj + j)),
                  pl.BlockSpec((1, CONV_WIDTH, tn), lambda i, j: (0, 0, j)),
                  pl.BlockSpec((1, CONV_WIDTH, tn), lambda i, j: (0, 0, nj + j)),
                  pl.BlockSpec((1, tn), lambda i, j: (0, j)),
                  pl.BlockSpec((1, tn), lambda i, j: (0, nj + j)),
                  pl.BlockSpec((tn, D_MODEL), lambda i, j: (j, 0))],
        out_specs=pl.BlockSpec((tm, D_MODEL), lambda i, j: (i, 0)),
        out_shape=jax.ShapeDtypeStruct((TOKENS, D_MODEL), F32),
        scratch_shapes=[pltpu.VMEM((tm + halo, D_MODEL), BF16),
                        pltpu.VMEM((2, tm + halo, tn), F32)],
        compiler_params=_params(("arbitrary", "arbitrary"), vmem),
        name="ffn",
    )(x1, x1, mod3, norm_g, w_up, w_up, conv_w, conv_w, conv_b, conv_b, w_down)


def kernel(x, c, w_ada, b_ada, norm1_g, w_in, q_norm_g, k_norm_g, rel_bias, ret_norm_g,
           w_attn_br, w_ret_br, w_o, norm2_g, w_up, conv_w, conv_b, w_down):
    assert x.shape == (BATCH, SEQ, D_MODEL) and w_ada.shape[0] == 1
    assert _far_bucket_is_last()
    layer = 0
    x2d = x.reshape(TOKENS, D_MODEL)

    c_pad = jnp.pad(c, ((0, 8 - BATCH), (0, 0)))
    mod = _modulation(c_pad, w_ada[layer], b_ada)[:BATCH]
    mod3 = mod.reshape(BATCH, N_MOD, D_MODEL)

    proj = _input_projection(x2d, mod3, norm1_g, w_in[layer].astype(BF16))
    proj3 = proj.reshape(BATCH, SEQ, IN_WIDTH)

    bias_tiles = _relbias_tiles(rel_bias)
    ya = _moba(proj3, rel_bias, q_norm_g, k_norm_g, bias_tiles)
    yr = _retention(proj3, ret_norm_g)

    x1 = _mixer(ya.reshape(TOKENS, ATTN_WIDTH), yr.reshape(TOKENS, RET_V_WIDTH), proj,
                w_attn_br[layer].astype(BF16), w_ret_br[layer].astype(BF16),
                w_o[layer].astype(BF16), x2d, mod3)

    out = _ffn(x1, mod3, norm2_g, w_up[layer].astype(BF16), conv_w, conv_b,
               w_down[layer].astype(BF16))
    return out.reshape(BATCH, SEQ, D_MODEL)
```

```python
import functools
import math

import numpy as np
import jax
import jax.numpy as jnp
from jax import lax
from jax.experimental import pallas as pl
from jax.experimental.pallas import tpu as pltpu

F32 = jnp.float32
BF16 = jnp.bfloat16

D_MODEL = 2048
BATCH = 4
SEQ = 2048
ATTN_HEADS = 8
ATTN_HEAD_DIM = 128
MOBA_BLOCK = 256
MOBA_TOPK = 3
REL_BUCKETS = 32
REL_MAX_DIST = 128
RET_HEADS = 8
RET_KEY_DIM = 128
RET_VAL_DIM = 256
RET_CHUNK = 128
ROPE_BASE = 10000.0
FFN_DIM = 5632
CONV_WIDTH = 3
EPS = 1e-6
N_MOD = 6

ATTN_WIDTH = ATTN_HEADS * ATTN_HEAD_DIM
RET_QK_WIDTH = RET_HEADS * RET_KEY_DIM
RET_V_WIDTH = RET_HEADS * RET_VAL_DIM
OFF_QA = 0
OFF_KA = OFF_QA + ATTN_WIDTH
OFF_VA = OFF_KA + ATTN_WIDTH
OFF_QR = OFF_VA + ATTN_WIDTH
OFF_KR = OFF_QR + RET_QK_WIDTH
OFF_VR = OFF_KR + RET_QK_WIDTH
OFF_GR = OFF_VR + RET_V_WIDTH
OFF_GA = OFF_GR + RET_V_WIDTH
OFF_GB = OFF_GA + D_MODEL
IN_WIDTH = OFF_GB + D_MODEL

TOKENS = BATCH * SEQ
N_BLOCKS = SEQ // MOBA_BLOCK
N_CHUNKS = SEQ // RET_CHUNK
MASK_VALUE = -1e30
MIB = 1024 * 1024

SHIFT1, SCALE1, GATE1, SHIFT2, SCALE2, GATE2 = range(N_MOD)

MOD_TN = 1024
INPROJ_TM, INPROJ_TN = 1024, 1024
MIX_TM, MIX_TN = 512, 512
FFN_TM, FFN_TN = 512, 512
FFN_HALO = 16
NORM_ROWS = 64


def _params(semantics, vmem_bytes):
    return pltpu.CompilerParams(dimension_semantics=semantics,
                                vmem_limit_bytes=int(vmem_bytes))


def _sigmoid(v):
    return 1.0 / (1.0 + jnp.exp(-v))


def _silu(v):
    return v / (1.0 + jnp.exp(-v))


def _rms_mod(x, g, scale, shift):
    ms = jnp.mean(x * x, axis=-1, keepdims=True)
    return ((x * lax.rsqrt(ms + EPS)) * g) * (1.0 + scale) + shift


def _mod_kernel(c_ref, w_ref, b_ref, o_ref):
    s = _silu(c_ref[...])
    o_ref[...] = jnp.dot(s, w_ref[...], preferred_element_type=F32,
                         precision=lax.Precision.HIGHEST) + b_ref[...]


def _modulation(c_pad, w_ada, b_ada):
    rows = c_pad.shape[0]
    n = w_ada.shape[1]
    return pl.pallas_call(
        _mod_kernel,
        grid=(n // MOD_TN,),
        in_specs=[pl.BlockSpec((rows, D_MODEL), lambda j: (0, 0)),
                  pl.BlockSpec((D_MODEL, MOD_TN), lambda j: (0, j)),
                  pl.BlockSpec((1, MOD_TN), lambda j: (0, j))],
        out_specs=pl.BlockSpec((rows, MOD_TN), lambda j: (0, j)),
        out_shape=jax.ShapeDtypeStruct((rows, n), F32),
        compiler_params=_params(("arbitrary",), 2 * D_MODEL * MOD_TN * 4 + 8 * MIB),
        name="mod",
    )(c_pad, w_ada, b_ada)


def _inproj_kernel(x_ref, mod_ref, g_ref, w_ref, o_ref, h_ref):
    @pl.when(pl.program_id(1) == 0)
    def _():
        g = g_ref[...]
        shift = mod_ref[0, SHIFT1:SHIFT1 + 1, :]
        scale = mod_ref[0, SCALE1:SCALE1 + 1, :]

        def body(r, carry):
            rows = pl.ds(pl.multiple_of(r * NORM_ROWS, NORM_ROWS), NORM_ROWS)
            h_ref[rows, :] = _rms_mod(x_ref[rows, :], g, scale, shift).astype(BF16)
            return carry

        lax.fori_loop(0, INPROJ_TM // NORM_ROWS, body, 0)

    o_ref[...] = jnp.dot(h_ref[...], w_ref[...],
                         preferred_element_type=F32).astype(o_ref.dtype)


def _input_projection(x2d, mod3, norm_g, w_in):
    tm, tn = INPROJ_TM, INPROJ_TN
    vmem = (2 * tm * D_MODEL * 4 + 2 * D_MODEL * tn * 2 + 2 * tm * tn * 2
            + tm * D_MODEL * 2 + tm * tn * 4 + 4 * MIB)
    return pl.pallas_call(
        _inproj_kernel,
        grid=(TOKENS // tm, IN_WIDTH // tn),
        in_specs=[pl.BlockSpec((tm, D_MODEL), lambda i, j: (i, 0)),
                  pl.BlockSpec((1, N_MOD, D_MODEL), lambda i, j: (i // (SEQ // tm), 0, 0)),
                  pl.BlockSpec((1, D_MODEL), lambda i, j: (0, 0)),
                  pl.BlockSpec((D_MODEL, tn), lambda i, j: (0, j))],
        out_specs=pl.BlockSpec((tm, tn), lambda i, j: (i, j)),
        out_shape=jax.ShapeDtypeStruct((TOKENS, IN_WIDTH), BF16),
        scratch_shapes=[pltpu.VMEM((tm, D_MODEL), BF16)],
        compiler_params=_params(("arbitrary", "arbitrary"), vmem),
        name="inproj",
    )(x2d, mod3, norm_g, w_in)


def _relbias_kernel(rb_ref, o_ref):
    h = pl.program_id(0)
    shape = (2 * MOBA_BLOCK, MOBA_BLOCK)
    key = lax.broadcasted_iota(jnp.int32, shape, 0)
    qry = lax.broadcasted_iota(jnp.int32, shape, 1)
    dist = qry - key + MOBA_BLOCK
    n = jnp.maximum(dist, 0)
    max_exact = REL_BUCKETS // 2
    nf = jnp.maximum(n, 1).astype(F32)
    large = max_exact + (jnp.log(nf / max_exact) / math.log(REL_MAX_DIST / max_exact)
                         * (REL_BUCKETS - max_exact)).astype(jnp.int32)
    large = jnp.minimum(large, REL_BUCKETS - 1)
    bucket = jnp.where(n < max_exact, n, large)
    bias = jnp.zeros(shape, F32)
    for b in range(REL_BUCKETS):
        bias = jnp.where(bucket == b, rb_ref[b, h], bias)
    o_ref[0] = jnp.where(dist >= 0, bias, MASK_VALUE)


def _relbias_tiles(rel_bias):
    return pl.pallas_call(
        _relbias_kernel,
        grid=(ATTN_HEADS,),
        in_specs=[pl.BlockSpec(memory_space=pltpu.SMEM)],
        out_specs=pl.BlockSpec((1, 2 * MOBA_BLOCK, MOBA_BLOCK), lambda h: (h, 0, 0)),
        out_shape=jax.ShapeDtypeStruct((ATTN_HEADS, 2 * MOBA_BLOCK, MOBA_BLOCK), F32),
        compiler_params=_params(("arbitrary",), 16 * MIB),
        name="relbias",
    )(rel_bias)


def _far_bucket_is_last():
    d = np.arange(MOBA_BLOCK + 1, SEQ, dtype=np.float32)
    max_exact = REL_BUCKETS // 2
    large = max_exact + (np.log(d / max_exact) / math.log(REL_MAX_DIST / max_exact)
                         * (REL_BUCKETS - max_exact)).astype(np.int32)
    return bool(np.all(np.minimum(large, REL_BUCKETS - 1) == REL_BUCKETS - 1))


def _moba_kernel(rb_ref, q_ref, k_ref, v_ref, gq_ref, gk_ref, bt_ref, o_ref,
                 qb_ref, kb_ref, vt_ref, s_ref):
    h = pl.program_id(1)
    far_bias = rb_ref[REL_BUCKETS - 1, h]
    blk = MOBA_BLOCK
    nt = (((1,), (1,)), ((), ()))

    q = q_ref[0].astype(F32)
    k = k_ref[0].astype(F32)
    qn = (q * lax.rsqrt(jnp.mean(q * q, axis=-1, keepdims=True) + EPS)) * gq_ref[...]
    kn = (k * lax.rsqrt(jnp.mean(k * k, axis=-1, keepdims=True) + EPS)) * gk_ref[...]
    qb_ref[...] = qn.astype(BF16)
    kb_ref[...] = kn.astype(BF16)
    vt_ref[...] = v_ref[0].astype(F32).T.astype(BF16)

    k_mean = jnp.concatenate(
        [jnp.sum(kn[n * blk:(n + 1) * blk], axis=0, keepdims=True) for n in range(N_BLOCKS)],
        axis=0) * (1.0 / blk)
    gate = lax.dot_general(k_mean, qn, nt, preferred_element_type=F32,
                           precision=lax.Precision.HIGHEST)
    row = lax.broadcasted_iota(jnp.int32, gate.shape, 0)
    cur = lax.broadcasted_iota(jnp.int32, gate.shape, 1) // blk
    rank = jnp.zeros(gate.shape, jnp.int32)
    for m in range(N_BLOCKS):
        gm = gate[m:m + 1, :]
        beats = (m < cur) & ((gm > gate) | ((gm == gate) & (m < row)))
        rank = rank + beats.astype(jnp.int32)
    sel = ((row < cur) & (rank < MOBA_TOPK)).astype(F32)

    scale = ATTN_HEAD_DIM ** -0.5
    for qi in range(N_BLOCKS):
        cols = slice(qi * blk, (qi + 1) * blk)
        nk = (qi + 1) * blk
        s_all = lax.dot_general(kb_ref[0:nk, :], qb_ref[cols, :], nt,
                                preferred_element_type=F32) * scale
        for n in range(qi + 1):
            s_blk = s_all[n * blk:(n + 1) * blk]
            if n == qi:
                s_blk = s_blk + bt_ref[0, blk:2 * blk, :]
            else:
                if n == qi - 1:
                    s_blk = s_blk + bt_ref[0, 0:blk, :]
                else:
                    s_blk = s_blk + far_bias
                s_blk = jnp.where(sel[n:n + 1, cols] > 0.5, s_blk, MASK_VALUE)
            s_ref[n * blk:(n + 1) * blk, :] = s_blk
        s = s_ref[0:nk, :]
        m = jnp.max(s, axis=0, keepdims=True)
        p = jnp.exp(s - m)
        l = jnp.sum(p, axis=0, keepdims=True)
        o_t = jnp.dot(vt_ref[:, 0:nk], p.astype(BF16), preferred_element_type=F32)
        o_t = o_t * (1.0 / l)
        o_ref[0, cols, :] = o_t.T.astype(o_ref.dtype)


def _moba(proj3, rel_bias, q_norm_g, k_norm_g, bias_tiles):
    hd = ATTN_HEAD_DIM
    head = lambda off: (lambda b, h: (b, 0, off // hd + h))
    return pl.pallas_call(
        _moba_kernel,
        grid=(BATCH, ATTN_HEADS),
        in_specs=[pl.BlockSpec(memory_space=pltpu.SMEM),
                  pl.BlockSpec((1, SEQ, hd), head(OFF_QA)),
                  pl.BlockSpec((1, SEQ, hd), head(OFF_KA)),
                  pl.BlockSpec((1, SEQ, hd), head(OFF_VA)),
                  pl.BlockSpec((1, hd), lambda b, h: (0, 0)),
                  pl.BlockSpec((1, hd), lambda b, h: (0, 0)),
                  pl.BlockSpec((1, 2 * MOBA_BLOCK, MOBA_BLOCK), lambda b, h: (h, 0, 0))],
        out_specs=pl.BlockSpec((1, SEQ, hd), lambda b, h: (b, 0, h)),
        out_shape=jax.ShapeDtypeStruct((BATCH, SEQ, ATTN_WIDTH), BF16),
        scratch_shapes=[pltpu.VMEM((SEQ, hd), BF16),
                        pltpu.VMEM((SEQ, hd), BF16),
                        pltpu.VMEM((hd, SEQ), BF16),
                        pltpu.VMEM((SEQ, MOBA_BLOCK), F32)],
        compiler_params=_params(("arbitrary", "arbitrary"), 40 * MIB),
        name="moba",
    )(rel_bias, proj3, proj3, proj3, q_norm_g, k_norm_g, bias_tiles)


def _retention_kernel(cd_ref, q_ref, k_ref, v_ref, gr_ref, cos_ref, sin_ref,
                      dmask_ref, qdec_ref, kdec_ref, gn_ref, o_ref, qf_ref, kf_ref):
    h = pl.program_id(1)
    chunk_decay = cd_ref[h]
    half = RET_KEY_DIM // 2
    cos = cos_ref[...]
    sin = sin_ref[...]
    q = q_ref[0].astype(F32)
    k = k_ref[0].astype(F32)
    qf_ref[...] = q * cos + pltpu.roll(q, half, 1) * sin
    kf_ref[...] = (k * cos + pltpu.roll(k, half, 1) * sin) * (RET_KEY_DIM ** -0.5)

    nt = (((1,), (1,)), ((), ()))
    state = jnp.zeros((RET_KEY_DIM, RET_VAL_DIM), F32)
    for c in range(N_CHUNKS):
        rows = slice(c * RET_CHUNK, (c + 1) * RET_CHUNK)
        qc = qf_ref[rows, :]
        kc = kf_ref[rows, :]
        vc = v_ref[0, rows, :]
        scores = lax.dot_general(qc.astype(BF16), kc.astype(BF16), nt,
                                 preferred_element_type=F32) * dmask_ref[0]
        inner = jnp.dot(scores.astype(BF16), vc, preferred_element_type=F32)
        cross = jnp.dot((qc * qdec_ref[0]).astype(BF16), state.astype(BF16),
                        preferred_element_type=F32)
        y = inner + cross
        kd_t = (kc * kdec_ref[0]).T.astype(BF16)
        state = chunk_decay * state + jnp.dot(kd_t, vc, preferred_element_type=F32)

        mu = jnp.mean(y, axis=-1, keepdims=True)
        yc = y - mu
        var = jnp.mean(yc * yc, axis=-1, keepdims=True)
        yn = (yc * lax.rsqrt(var + EPS)) * gn_ref[...]
        o_ref[0, rows, :] = (yn * _silu(gr_ref[0, rows, :].astype(F32))).astype(o_ref.dtype)


def _retention_tables():
    half = RET_KEY_DIM // 2
    freqs = jnp.power(ROPE_BASE, -jnp.arange(half, dtype=F32) / half)
    ang = jnp.arange(SEQ).astype(F32)[:, None] * freqs[None, :]
    cos, sin = jnp.cos(ang), jnp.sin(ang)
    cos_full = jnp.concatenate([cos, cos], axis=-1)
    sin_signed = jnp.concatenate([-sin, sin], axis=-1)

    log_decay = jnp.log(1.0 - jnp.power(2.0, -5.0 - jnp.arange(RET_HEADS, dtype=F32)))
    i = jnp.arange(RET_CHUNK, dtype=F32)
    diff = i[:, None] - i[None, :]
    ld = log_decay[:, None, None]
    inner_decay = jnp.where(diff >= 0, jnp.exp(ld * jnp.maximum(diff, 0.0)), 0.0)
    q_decay = jnp.exp(log_decay[:, None] * (i + 1.0))
    k_decay = jnp.exp(log_decay[:, None] * (RET_CHUNK - 1.0 - i))
    chunk_decay = jnp.exp(log_decay * RET_CHUNK)
    bcast = lambda t: jnp.broadcast_to(t[:, :, None], (RET_HEADS, RET_CHUNK, RET_KEY_DIM))
    return cos_full, sin_signed, inner_decay, bcast(q_decay), bcast(k_decay), chunk_decay


def _retention(proj3, ret_norm_g):
    dk, dv = RET_KEY_DIM, RET_VAL_DIM
    cos, sin, dmask, qdec, kdec, chunk_decay = _retention_tables()
    head = lambda off, w: (lambda b, h: (b, 0, off // w + h))
    per_head = lambda b, h: (h, 0, 0)
    return pl.pallas_call(
        _retention_kernel,
        grid=(BATCH, RET_HEADS),
        in_specs=[pl.BlockSpec(memory_space=pltpu.SMEM),
                  pl.BlockSpec((1, SEQ, dk), head(OFF_QR, dk)),
                  pl.BlockSpec((1, SEQ, dk), head(OFF_KR, dk)),
                  pl.BlockSpec((1, SEQ, dv), head(OFF_VR, dv)),
                  pl.BlockSpec((1, SEQ, dv), head(OFF_GR, dv)),
                  pl.BlockSpec((SEQ, dk), lambda b, h: (0, 0)),
                  pl.BlockSpec((SEQ, dk), lambda b, h: (0, 0)),
                  pl.BlockSpec((1, RET_CHUNK, RET_CHUNK), per_head),
                  pl.BlockSpec((1, RET_CHUNK, dk), per_head),
                  pl.BlockSpec((1, RET_CHUNK, dk), per_head),
                  pl.BlockSpec((1, dv), lambda b, h: (0, h))],
        out_specs=pl.BlockSpec((1, SEQ, dv), lambda b, h: (b, 0, h)),
        out_shape=jax.ShapeDtypeStruct((BATCH, SEQ, RET_V_WIDTH), BF16),
        scratch_shapes=[pltpu.VMEM((SEQ, dk), F32), pltpu.VMEM((SEQ, dk), F32)],
        compiler_params=_params(("arbitrary", "arbitrary"), 40 * MIB),
        name="retention",
    )(chunk_decay, proj3, proj3, proj3, proj3, cos, sin, dmask, qdec, kdec, ret_norm_g)


def _mixer_kernel(ya_ref, yr_ref, ga_ref, gb_ref, wa_ref, wr_ref, wo_ref, x_ref, mod_ref,
                  o_ref):
    j = pl.program_id(1)

    @pl.when(j == 0)
    def _():
        o_ref[...] = jnp.zeros_like(o_ref)

    a = jnp.dot(ya_ref[...], wa_ref[...], preferred_element_type=F32)
    r = jnp.dot(yr_ref[...], wr_ref[...], preferred_element_type=F32)
    merged = (_sigmoid(ga_ref[...].astype(F32)) * a
              + _sigmoid(gb_ref[...].astype(F32)) * r)
    o_ref[...] += jnp.dot(merged.astype(BF16), wo_ref[...], preferred_element_type=F32)

    @pl.when(j == pl.num_programs(1) - 1)
    def _():
        o_ref[...] = x_ref[...] + mod_ref[0, GATE1:GATE1 + 1, :] * o_ref[...]


def _mixer(ya2d, yr2d, proj2d, w_attn_br, w_ret_br, w_o, x2d, mod3):
    tm, tn = MIX_TM, MIX_TN
    vmem = (2 * tm * (ATTN_WIDTH + RET_V_WIDTH + 2 * tn) * 2
            + 2 * (ATTN_WIDTH + RET_V_WIDTH + D_MODEL) * tn * 2
            + 4 * tm * D_MODEL * 4 + tm * D_MODEL * 4 + 6 * tm * tn * 4 + 4 * MIB)
    return pl.pallas_call(
        _mixer_kernel,
        grid=(TOKENS // tm, D_MODEL // tn),
        in_specs=[pl.BlockSpec((tm, ATTN_WIDTH), lambda i, j: (i, 0)),
                  pl.BlockSpec((tm, RET_V_WIDTH), lambda i, j: (i, 0)),
                  pl.BlockSpec((tm, tn), lambda i, j: (i, OFF_GA // tn + j)),
                  pl.BlockSpec((tm, tn), lambda i, j: (i, OFF_GB // tn + j)),
                  pl.BlockSpec((ATTN_WIDTH, tn), lambda i, j: (0, j)),
                  pl.BlockSpec((RET_V_WIDTH, tn), lambda i, j: (0, j)),
                  pl.BlockSpec((tn, D_MODEL), lambda i, j: (j, 0)),
                  pl.BlockSpec((tm, D_MODEL), lambda i, j: (i, 0)),
                  pl.BlockSpec((1, N_MOD, D_MODEL), lambda i, j: (i // (SEQ // tm), 0, 0))],
        out_specs=pl.BlockSpec((tm, D_MODEL), lambda i, j: (i, 0)),
        out_shape=jax.ShapeDtypeStruct((TOKENS, D_MODEL), F32),
        compiler_params=_params(("arbitrary", "arbitrary"), vmem),
        name="mixer",
    )(ya2d, yr2d, proj2d, proj2d, w_attn_br, w_ret_br, w_o, x2d, mod3)


def _ffn_kernel(x_ref, halo_ref, mod_ref, g_ref, wv_ref, wg_ref, cwv_ref, cwg_ref,
                cbv_ref, cbg_ref, wd_ref, o_ref, h_ref, u_ref):
    i = pl.program_id(0)
    j = pl.program_id(1)
    tm, halo = FFN_TM, FFN_HALO

    @pl.when(j == 0)
    def _():
        o_ref[...] = jnp.zeros_like(o_ref)
        g = g_ref[...]
        shift = mod_ref[0, SHIFT2:SHIFT2 + 1, :]
        scale = mod_ref[0, SCALE2:SCALE2 + 1, :]
        seq_start = (i % (SEQ // tm)) == 0
        h_halo = _rms_mod(halo_ref[...], g, scale, shift)
        h_ref[0:halo, :] = jnp.where(seq_start, 0.0, h_halo).astype(BF16)

        def body(r, carry):
            src = pl.ds(pl.multiple_of(r * NORM_ROWS, NORM_ROWS), NORM_ROWS)
            dst = pl.ds(pl.multiple_of(halo + r * NORM_ROWS, halo), NORM_ROWS)
            h_ref[dst, :] = _rms_mod(x_ref[src, :], g, scale, shift).astype(BF16)
            return carry

        lax.fori_loop(0, tm // NORM_ROWS, body, 0)

    def conv(half, w_ref, cw_ref, cb_ref):
        u_ref[half] = jnp.dot(h_ref[...], w_ref[...], preferred_element_type=F32)
        y = cb_ref[...]
        for t in range(CONV_WIDTH):
            lag = CONV_WIDTH - 1 - t
            y = y + cw_ref[0, t:t + 1, :] * u_ref[half, halo - lag:halo - lag + tm, :]
        return y

    val = conv(0, wv_ref, cwv_ref, cbv_ref)
    gt = conv(1, wg_ref, cwg_ref, cbg_ref)
    act = (_silu(gt) * val).astype(BF16)
    o_ref[...] += jnp.dot(act, wd_ref[...], preferred_element_type=F32)

    @pl.when(j == pl.num_programs(1) - 1)
    def _():
        o_ref[...] = x_ref[...] + mod_ref[0, GATE2:GATE2 + 1, :] * o_ref[...]


def _ffn(x1, mod3, norm_g, w_up, conv_w, conv_b, w_down):
    tm, tn, halo = FFN_TM, FFN_TN, FFN_HALO
    nj = FFN_DIM // tn
    vmem = (4 * tm * D_MODEL * 4 + 2 * halo * D_MODEL * 4
            + 2 * 3 * D_MODEL * tn * 2
            + (tm + halo) * D_MODEL * 2 + 2 * (tm + halo) * tn * 4
            + tm * D_MODEL * 4 + 8 * tm * tn * 4 + 4 * MIB)
    return pl.pallas_call(
        _ffn_kernel,
        grid=(TOKENS // tm, nj),
        in_specs=[pl.BlockSpec((tm, D_MODEL), lambda i, j: (i, 0)),
                  pl.BlockSpec((halo, D_MODEL),
                               lambda i, j: (jnp.maximum(i * (tm // halo) - 1, 0), 0)),
                  pl.BlockSpec((1, N_MOD, D_MODEL), lambda i, j: (i // (SEQ // tm), 0, 0)),
                  pl.BlockSpec((1, D_MODEL), lambda i, j: (0, 0)),
                  pl.BlockSpec((D_MODEL, tn), lambda i, j: (0, j)),
                  pl.BlockSpec((D_MODEL, tn), lambda i, j: (0, nj + j)),
                  pl.BlockSpec((1, CONV_WIDTH, tn), lambda i, j: (0, 0, j)),
                  pl.BlockSpec((1, CONV_WIDTH, tn), lambda i, j: (0, 0, nj + j)),
                  pl.BlockSpec((1, tn), lambda i, j: (0, j)),
                  pl.BlockSpec((1, tn), lambda i, j: (0, nj + j)),
                  pl.BlockSpec((tn, D_MODEL), lambda i, j: (j, 0))],
        out_specs=pl.BlockSpec((tm, D_MODEL), lambda i, j: (i, 0)),
        out_shape=jax.ShapeDtypeStruct((TOKENS, D_MODEL), F32),
        scratch_shapes=[pltpu.VMEM((tm + halo, D_MODEL), BF16),
                        pltpu.VMEM((2, tm + halo, tn), F32)],
        compiler_params=_params(("arbitrary", "arbitrary"), vmem),
        name="ffn",
    )(x1, x1, mod3, norm_g, w_up, w_up, conv_w, conv_w, conv_b, conv_b, w_down)


def kernel(x, c, w_ada, b_ada, norm1_g, w_in, q_norm_g, k_norm_g, rel_bias, ret_norm_g,
           w_attn_br, w_ret_br, w_o, norm2_g, w_up, conv_w, conv_b, w_down):
    assert x.shape == (BATCH, SEQ, D_MODEL) and w_ada.shape[0] == 1
    assert _far_bucket_is_last()
    layer = 0
    x2d = x.reshape(TOKENS, D_MODEL)

    c_pad = jnp.pad(c, ((0, 8 - BATCH), (0, 0)))
    mod = _modulation(c_pad, w_ada[layer], b_ada)[:BATCH]
    mod3 = mod.reshape(BATCH, N_MOD, D_MODEL)

    proj = _input_projection(x2d, mod3, norm1_g, w_in[layer].astype(BF16))
    proj3 = proj.reshape(BATCH, SEQ, IN_WIDTH)

    bias_tiles = _relbias_tiles(rel_bias)
    ya = _moba(proj3, rel_bias, q_norm_g, k_norm_g, bias_tiles)
    yr = _retention(proj3, ret_norm_g)

    x1 = _mixer(ya.reshape(TOKENS, ATTN_WIDTH), yr.reshape(TOKENS, RET_V_WIDTH), proj,
                w_attn_br[layer].astype(BF16), w_ret_br[layer].astype(BF16),
                w_o[layer].astype(BF16), x2d, mod3)

    out = _ffn(x1, mod3, norm2_g, w_up[layer].astype(BF16), conv_w, conv_b,
               w_down[layer].astype(BF16))
    return out.reshape(BATCH, SEQ, D_MODEL)
```

```python
import functools
import math

import numpy as np
import jax
import jax.numpy as jnp
from jax import lax
from jax.experimental import pallas as pl
from jax.experimental.pallas import tpu as pltpu

F32 = jnp.float32
BF16 = jnp.bfloat16

D_MODEL = 2048
BATCH = 4
SEQ = 2048
ATTN_HEADS = 8
ATTN_HEAD_DIM = 128
MOBA_BLOCK = 256
MOBA_TOPK = 3
REL_BUCKETS = 32
REL_MAX_DIST = 128
RET_HEADS = 8
RET_KEY_DIM = 128
RET_VAL_DIM = 256
RET_CHUNK = 128
ROPE_BASE = 10000.0
FFN_DIM = 5632
CONV_WIDTH = 3
EPS = 1e-6
N_MOD = 6

ATTN_WIDTH = ATTN_HEADS * ATTN_HEAD_DIM
RET_QK_WIDTH = RET_HEADS * RET_KEY_DIM
RET_V_WIDTH = RET_HEADS * RET_VAL_DIM
OFF_QA = 0
OFF_KA = OFF_QA + ATTN_WIDTH
OFF_VA = OFF_KA + ATTN_WIDTH
OFF_QR = OFF_VA + ATTN_WIDTH
OFF_KR = OFF_QR + RET_QK_WIDTH
OFF_VR = OFF_KR + RET_QK_WIDTH
OFF_GR = OFF_VR + RET_V_WIDTH
OFF_GA = OFF_GR + RET_V_WIDTH
OFF_GB = OFF_GA + D_MODEL
IN_WIDTH = OFF_GB + D_MODEL

TOKENS = BATCH * SEQ
N_BLOCKS = SEQ // MOBA_BLOCK
N_CHUNKS = SEQ // RET_CHUNK
MASK_VALUE = -1e30
LOG2E = math.log2(math.e)
MOBA_ONES_ROWS = 16
MOBA_TILE_GROUPS = ((0, 7, 3, 4, 1, 6, 2, 5),)
MOBA_GROUP_SIZE = max(len(g) for g in MOBA_TILE_GROUPS)
MIB = 1024 * 1024

SHIFT1, SCALE1, GATE1, SHIFT2, SCALE2, GATE2 = range(N_MOD)

MOD_TN = 1024
INPROJ_TM, INPROJ_TN = 1024, 1024
MIX_TM, MIX_TN = 512, 512
FFN_TM, FFN_TN = 512, 512
FFN_HALO = 16
FFN_NJ = FFN_DIM // FFN_TN
FFN_STEPS = (TOKENS // FFN_TM) * FFN_NJ
FFN_PIECES = 2
FFN_FLAGS = None
NORM_ROWS = 64


def _params(semantics, vmem_bytes, flags=None):
    return pltpu.CompilerParams(dimension_semantics=semantics,
                                vmem_limit_bytes=int(vmem_bytes), flags=flags)


def _sigmoid(v):
    return 0.5 + 0.5 * jnp.tanh(0.5 * v)


def _silu(v):
    hv = 0.5 * v
    return hv + hv * jnp.tanh(hv)


def _rms_mod(x, g, scale, shift):
    ms = jnp.mean(x * x, axis=-1, keepdims=True)
    return ((x * lax.rsqrt(ms + EPS)) * g) * (1.0 + scale) + shift


def _mod_kernel(c_ref, w_ref, b_ref, o_ref):
    s = _silu(c_ref[...])
    o_ref[...] = jnp.dot(s, w_ref[...], preferred_element_type=F32,
                         precision=lax.Precision.HIGHEST) + b_ref[...]


def _modulation(c_pad, w_ada, b_ada):
    rows = c_pad.shape[0]
    n = w_ada.shape[1]
    return pl.pallas_call(
        _mod_kernel,
        grid=(n // MOD_TN,),
        in_specs=[pl.BlockSpec((rows, D_MODEL), lambda j: (0, 0)),
                  pl.BlockSpec((D_MODEL, MOD_TN), lambda j: (0, j)),
                  pl.BlockSpec((1, MOD_TN), lambda j: (0, j))],
        out_specs=pl.BlockSpec((rows, MOD_TN), lambda j: (0, j)),
        out_shape=jax.ShapeDtypeStruct((rows, n), F32),
        compiler_params=_params(("arbitrary",), 2 * D_MODEL * MOD_TN * 4 + 8 * MIB),
        name="mod",
    )(c_pad, w_ada, b_ada)


def _inproj_kernel(x_ref, mod_ref, g_ref, w_ref, o_ref, h_ref):
    @pl.when(pl.program_id(1) == 0)
    def _():
        g = g_ref[...]
        shift = mod_ref[0, SHIFT1:SHIFT1 + 1, :]
        scale = mod_ref[0, SCALE1:SCALE1 + 1, :]

        def body(r, carry):
            rows = pl.ds(pl.multiple_of(r * NORM_ROWS, NORM_ROWS), NORM_ROWS)
            h_ref[rows, :] = _rms_mod(x_ref[rows, :], g, scale, shift).astype(BF16)
            return carry

        lax.fori_loop(0, INPROJ_TM // NORM_ROWS, body, 0)

    o_ref[...] = jnp.dot(h_ref[...], w_ref[...].astype(BF16),
                         preferred_element_type=F32).astype(o_ref.dtype)


def _input_projection(x2d, mod3, norm_g, w_in):
    tm, tn = INPROJ_TM, INPROJ_TN
    vmem = (2 * tm * D_MODEL * 4 + 2 * D_MODEL * tn * 4 + 2 * tm * tn * 2
            + tm * D_MODEL * 2 + D_MODEL * tn * 2 + tm * tn * 4 + 4 * MIB)
    return pl.pallas_call(
        _inproj_kernel,
        grid=(TOKENS // tm, IN_WIDTH // tn),
        in_specs=[pl.BlockSpec((tm, D_MODEL), lambda i, j: (i, 0)),
                  pl.BlockSpec((1, N_MOD, D_MODEL), lambda i, j: (i // (SEQ // tm), 0, 0)),
                  pl.BlockSpec((1, D_MODEL), lambda i, j: (0, 0)),
                  pl.BlockSpec((D_MODEL, tn), lambda i, j: (0, j))],
        out_specs=pl.BlockSpec((tm, tn), lambda i, j: (i, j)),
        out_shape=jax.ShapeDtypeStruct((TOKENS, IN_WIDTH), BF16),
        scratch_shapes=[pltpu.VMEM((tm, D_MODEL), BF16)],
        compiler_params=_params(("arbitrary", "arbitrary"), vmem),
        name="inproj",
    )(x2d, mod3, norm_g, w_in)


def _relbias_kernel(rb_ref, o_ref):
    h = pl.program_id(0)
    shape = (2 * MOBA_BLOCK, MOBA_BLOCK)
    key = lax.broadcasted_iota(jnp.int32, shape, 0)
    qry = lax.broadcasted_iota(jnp.int32, shape, 1)
    dist = qry - key + MOBA_BLOCK
    n = jnp.maximum(dist, 0)
    max_exact = REL_BUCKETS // 2
    nf = jnp.maximum(n, 1).astype(F32)
    large = max_exact + (jnp.log(nf / max_exact) / math.log(REL_MAX_DIST / max_exact)
                         * (REL_BUCKETS - max_exact)).astype(jnp.int32)
    large = jnp.minimum(large, REL_BUCKETS - 1)
    bucket = jnp.where(n < max_exact, n, large)
    bias = jnp.zeros(shape, F32)
    for b in range(REL_BUCKETS):
        bias = jnp.where(bucket == b, rb_ref[b, h], bias)
    o_ref[0] = jnp.where(dist >= 0, bias * LOG2E, MASK_VALUE)


def _relbias_tiles(rel_bias):
    return pl.pallas_call(
        _relbias_kernel,
        grid=(ATTN_HEADS,),
        in_specs=[pl.BlockSpec(memory_space=pltpu.SMEM)],
        out_specs=pl.BlockSpec((1, 2 * MOBA_BLOCK, MOBA_BLOCK), lambda h: (h, 0, 0)),
        out_shape=jax.ShapeDtypeStruct((ATTN_HEADS, 2 * MOBA_BLOCK, MOBA_BLOCK), F32),
        compiler_params=_params(("arbitrary",), 16 * MIB),
        name="relbias",
    )(rel_bias)


def _far_bucket_is_last():
    d = np.arange(MOBA_BLOCK + 1, SEQ, dtype=np.float32)
    max_exact = REL_BUCKETS // 2
    large = max_exact + (np.log(d / max_exact) / math.log(REL_MAX_DIST / max_exact)
                         * (REL_BUCKETS - max_exact)).astype(np.int32)
    return bool(np.all(np.minimum(large, REL_BUCKETS - 1) == REL_BUCKETS - 1))


def _moba_kernel(rb_ref, q_ref, k_ref, v_ref, gq_ref, gk_ref, bt_ref, o_ref,
                 qb_ref, kb_ref, vt_ref, *s_refs):
    h = pl.program_id(1)
    far_bias = rb_ref[REL_BUCKETS - 1, h] * LOG2E
    blk, hd = MOBA_BLOCK, ATTN_HEAD_DIM
    nt = (((1,), (1,)), ((), ()))

    q = q_ref[0].astype(F32)
    k = k_ref[0].astype(F32)
    qn = (q * lax.rsqrt(jnp.mean(q * q, axis=-1, keepdims=True) + EPS)) * gq_ref[...]
    kn = (k * lax.rsqrt(jnp.mean(k * k, axis=-1, keepdims=True) + EPS)) * gk_ref[...]
    qb_ref[...] = (qn * (hd ** -0.5 * LOG2E)).astype(BF16)
    kb_ref[...] = kn.astype(BF16)
    vt_ref[0:hd, :] = v_ref[0].astype(F32).T.astype(BF16)
    vt_ref[hd:, :] = jnp.ones((MOBA_ONES_ROWS, SEQ), BF16)

    k_mean = jnp.concatenate(
        [jnp.sum(kn[n * blk:(n + 1) * blk], axis=0, keepdims=True) for n in range(N_BLOCKS)],
        axis=0) * (1.0 / blk)
    gate = lax.dot_general(k_mean, qn, nt, preferred_element_type=F32,
                           precision=lax.Precision.HIGHEST)
    row = lax.broadcasted_iota(jnp.int32, (N_BLOCKS, blk), 0)

    def scores(qi, s_ref):
        cols = slice(qi * blk, (qi + 1) * blk)
        nk = (qi + 1) * blk

        mask_add = None
        if qi > MOBA_TOPK:
            g = gate[:, cols]
            rank = jnp.zeros((N_BLOCKS, blk), F32)
            for m in range(qi):
                gm = g[m:m + 1, :]
                beats = (gm > g) | ((gm == g) & (row > m))
                rank = rank + jnp.where(beats, 1.0, 0.0)
            mask_add = jnp.where(rank < MOBA_TOPK, 0.0, MASK_VALUE)

        s_all = lax.dot_general(kb_ref[0:nk, :], qb_ref[cols, :], nt,
                                preferred_element_type=F32)
        m8 = None
        for n in range(qi + 1):
            s_blk = s_all[n * blk:(n + 1) * blk]
            if n == qi:
                s_blk = s_blk + bt_ref[0, blk:2 * blk, :]
            else:
                if n == qi - 1:
                    s_blk = s_blk + bt_ref[0, 0:blk, :]
                    if mask_add is not None:
                        s_blk = s_blk + mask_add[n:n + 1, :]
                elif mask_add is not None:
                    s_blk = s_blk + (mask_add[n:n + 1, :] + far_bias)
                else:
                    s_blk = s_blk + far_bias
            s_ref[n * blk:(n + 1) * blk, :] = s_blk
            b8 = jnp.max(s_blk.reshape(blk // 8, 8, blk), axis=0)
            m8 = b8 if m8 is None else jnp.maximum(m8, b8)
        return jnp.max(m8, axis=0, keepdims=True)

    def attend(qi, s_ref, m):
        cols = slice(qi * blk, (qi + 1) * blk)
        nk = (qi + 1) * blk
        p = jnp.exp2(s_ref[0:nk, :] - m).astype(BF16)
        o_aug = jnp.dot(vt_ref[:, 0:nk], p, preferred_element_type=F32)
        o_t = o_aug[0:hd] * (1.0 / o_aug[hd:hd + 1])
        o_ref[0, cols, :] = o_t.T.astype(o_ref.dtype)

    for group in MOBA_TILE_GROUPS:
        maxes = [scores(qi, s_ref) for qi, s_ref in zip(group, s_refs)]
        for qi, s_ref, m in zip(group, s_refs, maxes):
            attend(qi, s_ref, m)


def _moba(proj3, rel_bias, q_norm_g, k_norm_g, bias_tiles):
    hd = ATTN_HEAD_DIM
    head = lambda off: (lambda b, h: (b, 0, off // hd + h))
    return pl.pallas_call(
        _moba_kernel,
        grid=(BATCH, ATTN_HEADS),
        in_specs=[pl.BlockSpec(memory_space=pltpu.SMEM),
                  pl.BlockSpec((1, SEQ, hd), head(OFF_QA)),
                  pl.BlockSpec((1, SEQ, hd), head(OFF_KA)),
                  pl.BlockSpec((1, SEQ, hd), head(OFF_VA)),
                  pl.BlockSpec((1, hd), lambda b, h: (0, 0)),
                  pl.BlockSpec((1, hd), lambda b, h: (0, 0)),
                  pl.BlockSpec((1, 2 * MOBA_BLOCK, MOBA_BLOCK), lambda b, h: (h, 0, 0))],
        out_specs=pl.BlockSpec((1, SEQ, hd), lambda b, h: (b, 0, h)),
        out_shape=jax.ShapeDtypeStruct((BATCH, SEQ, ATTN_WIDTH), BF16),
        scratch_shapes=[pltpu.VMEM((SEQ, hd), BF16),
                        pltpu.VMEM((SEQ, hd), BF16),
                        pltpu.VMEM((hd + MOBA_ONES_ROWS, SEQ), BF16)]
                       + [pltpu.VMEM((SEQ, MOBA_BLOCK), F32)] * MOBA_GROUP_SIZE,
        compiler_params=_params(("arbitrary", "arbitrary"), 40 * MIB),
        name="moba",
    )(rel_bias, proj3, proj3, proj3, q_norm_g, k_norm_g, bias_tiles)


def _retention_kernel(cd_ref, q_ref, k_ref, v_ref, gr_ref, cos_ref, sin_ref,
                      dmask_ref, qdec_ref, kdec_ref, gn_ref, o_ref, qf_ref, kf_ref,
                      y_ref, kv_ref):
    h = pl.program_id(1)
    chunk_decay = cd_ref[h]
    half = RET_KEY_DIM // 2
    cos = cos_ref[...]
    sin = sin_ref[...]
    q = q_ref[0].astype(F32)
    k = k_ref[0].astype(F32)
    qf_ref[...] = q * cos + pltpu.roll(q, half, 1) * sin
    kf_ref[...] = (k * cos + pltpu.roll(k, half, 1) * sin) * (RET_KEY_DIM ** -0.5)

    nt = (((1,), (1,)), ((), ()))
    chunks = [slice(c * RET_CHUNK, (c + 1) * RET_CHUNK) for c in range(N_CHUNKS)]

    scores = [(lax.dot_general(qf_ref[rows, :].astype(BF16), kf_ref[rows, :].astype(BF16), nt,
                               preferred_element_type=F32) * dmask_ref[0]).astype(BF16)
              for rows in chunks]
    for c, rows in enumerate(chunks):
        vc = v_ref[0, rows, :]
        y_ref[rows, :] = jnp.dot(scores[c], vc, preferred_element_type=F32)
        kd_t = (kf_ref[rows, :] * kdec_ref[0]).T.astype(BF16)
        kv_ref[c] = jnp.dot(kd_t, vc, preferred_element_type=F32)

    state = jnp.zeros((RET_KEY_DIM, RET_VAL_DIM), F32)
    for c, rows in enumerate(chunks):
        y_ref[rows, :] += jnp.dot((qf_ref[rows, :] * qdec_ref[0]).astype(BF16),
                                  state.astype(BF16), preferred_element_type=F32)
        state = chunk_decay * state + kv_ref[c]

    for rows in chunks:
        y = y_ref[rows, :]
        mu = jnp.mean(y, axis=-1, keepdims=True)
        yc = y - mu
        var = jnp.mean(yc * yc, axis=-1, keepdims=True)
        yn = (yc * lax.rsqrt(var + EPS)) * gn_ref[...]
        o_ref[0, rows, :] = (yn * _silu(gr_ref[0, rows, :]).astype(F32)).astype(o_ref.dtype)


def _retention_tables():
    half = RET_KEY_DIM // 2
    freqs = jnp.power(ROPE_BASE, -jnp.arange(half, dtype=F32) / half)
    ang = jnp.arange(SEQ).astype(F32)[:, None] * freqs[None, :]
    cos, sin = jnp.cos(ang), jnp.sin(ang)
    cos_full = jnp.concatenate([cos, cos], axis=-1)
    sin_signed = jnp.concatenate([-sin, sin], axis=-1)

    log_decay = jnp.log(1.0 - jnp.power(2.0, -5.0 - jnp.arange(RET_HEADS, dtype=F32)))
    i = jnp.arange(RET_CHUNK, dtype=F32)
    diff = i[:, None] - i[None, :]
    ld = log_decay[:, None, None]
    inner_decay = jnp.where(diff >= 0, jnp.exp(ld * jnp.maximum(diff, 0.0)), 0.0)
    q_decay = jnp.exp(log_decay[:, None] * (i + 1.0))
    k_decay = jnp.exp(log_decay[:, None] * (RET_CHUNK - 1.0 - i))
    chunk_decay = jnp.exp(log_decay * RET_CHUNK)
    bcast = lambda t: jnp.broadcast_to(t[:, :, None], (RET_HEADS, RET_CHUNK, RET_KEY_DIM))
    return cos_full, sin_signed, inner_decay, bcast(q_decay), bcast(k_decay), chunk_decay


def _retention(proj3, ret_norm_g):
    dk, dv = RET_KEY_DIM, RET_VAL_DIM
    cos, sin, dmask, qdec, kdec, chunk_decay = _retention_tables()
    head = lambda off, w: (lambda b, h: (b, 0, off // w + h))
    per_head = lambda b, h: (h, 0, 0)
    return pl.pallas_call(
        _retention_kernel,
        grid=(BATCH, RET_HEADS),
        in_specs=[pl.BlockSpec(memory_space=pltpu.SMEM),
                  pl.BlockSpec((1, SEQ, dk), head(OFF_QR, dk)),
                  pl.BlockSpec((1, SEQ, dk), head(OFF_KR, dk)),
                  pl.BlockSpec((1, SEQ, dv), head(OFF_VR, dv)),
                  pl.BlockSpec((1, SEQ, dv), head(OFF_GR, dv)),
                  pl.BlockSpec((SEQ, dk), lambda b, h: (0, 0)),
                  pl.BlockSpec((SEQ, dk), lambda b, h: (0, 0)),
                  pl.BlockSpec((1, RET_CHUNK, RET_CHUNK), per_head),
                  pl.BlockSpec((1, RET_CHUNK, dk), per_head),
                  pl.BlockSpec((1, RET_CHUNK, dk), per_head),
                  pl.BlockSpec((1, dv), lambda b, h: (0, h))],
        out_specs=pl.BlockSpec((1, SEQ, dv), lambda b, h: (b, 0, h)),
        out_shape=jax.ShapeDtypeStruct((BATCH, SEQ, RET_V_WIDTH), BF16),
        scratch_shapes=[pltpu.VMEM((SEQ, dk), F32), pltpu.VMEM((SEQ, dk), F32),
                        pltpu.VMEM((SEQ, dv), F32), pltpu.VMEM((N_CHUNKS, dk, dv), F32)],
        compiler_params=_params(("arbitrary", "arbitrary"), 40 * MIB),
        name="retention",
    )(chunk_decay, proj3, proj3, proj3, proj3, cos, sin, dmask, qdec, kdec, ret_norm_g)


def _mixer_kernel(ya_ref, yr_ref, ga_ref, gb_ref, wa_ref, wr_ref, wo_ref, x_ref, mod_ref,
                  o_ref):
    j = pl.program_id(1)

    @pl.when(j == 0)
    def _():
        o_ref[...] = jnp.zeros_like(o_ref)

    a = jnp.dot(ya_ref[...], wa_ref[...], preferred_element_type=F32)
    r = jnp.dot(yr_ref[...], wr_ref[...], preferred_element_type=F32)
    merged = (_sigmoid(ga_ref[...].astype(F32)) * a
              + _sigmoid(gb_ref[...].astype(F32)) * r)
    o_ref[...] += jnp.dot(merged.astype(BF16), wo_ref[...], preferred_element_type=F32)

    @pl.when(j == pl.num_programs(1) - 1)
    def _():
        o_ref[...] = x_ref[...] + mod_ref[0, GATE1:GATE1 + 1, :] * o_ref[...]


def _mixer(ya2d, yr2d, proj2d, w_attn_br, w_ret_br, w_o, x2d, mod3):
    tm, tn = MIX_TM, MIX_TN
    vmem = (2 * tm * (ATTN_WIDTH + RET_V_WIDTH + 2 * tn) * 2
            + 2 * (ATTN_WIDTH + RET_V_WIDTH + D_MODEL) * tn * 2
            + 4 * tm * D_MODEL * 4 + tm * D_MODEL * 4 + 6 * tm * tn * 4 + 4 * MIB)
    return pl.pallas_call(
        _mixer_kernel,
        grid=(TOKENS // tm, D_MODEL // tn),
        in_specs=[pl.BlockSpec((tm, ATTN_WIDTH), lambda i, j: (i, 0)),
                  pl.BlockSpec((tm, RET_V_WIDTH), lambda i, j: (i, 0)),
                  pl.BlockSpec((tm, tn), lambda i, j: (i, OFF_GA // tn + j)),
                  pl.BlockSpec((tm, tn), lambda i, j: (i, OFF_GB // tn + j)),
                  pl.BlockSpec((ATTN_WIDTH, tn), lambda i, j: (0, j)),
                  pl.BlockSpec((RET_V_WIDTH, tn), lambda i, j: (0, j)),
                  pl.BlockSpec((tn, D_MODEL), lambda i, j: (j, 0)),
                  pl.BlockSpec((tm, D_MODEL), lambda i, j: (i, 0)),
                  pl.BlockSpec((1, N_MOD, D_MODEL), lambda i, j: (i // (SEQ // tm), 0, 0))],
        out_specs=pl.BlockSpec((tm, D_MODEL), lambda i, j: (i, 0)),
        out_shape=jax.ShapeDtypeStruct((TOKENS, D_MODEL), F32),
        compiler_params=_params(("arbitrary", "arbitrary"), vmem),
        name="mixer",
    )(ya2d, yr2d, proj2d, proj2d, w_attn_br, w_ret_br, w_o, x2d, mod3)


def _ffn_kernel(x_ref, halo_ref, mod_ref, g_ref, wv_ref, wg_ref, cwv_ref, cwg_ref,
                cbv_ref, cbg_ref, wd_ref, o_ref, h_ref, u_ref):
    i = pl.program_id(0)
    j = pl.program_id(1)
    tm, halo = FFN_TM, FFN_HALO

    @pl.when(j == 0)
    def _():
        o_ref[...] = jnp.zeros_like(o_ref)
        g = g_ref[...]
        shift = mod_ref[0, SHIFT2:SHIFT2 + 1, :]
        scale = mod_ref[0, SCALE2:SCALE2 + 1, :]
        seq_start = (i % (SEQ // tm)) == 0
        h_halo = _rms_mod(halo_ref[...], g, scale, shift)
        h_ref[0:halo, :] = jnp.where(seq_start, 0.0, h_halo).astype(BF16)

        def body(r, carry):
            src = pl.ds(pl.multiple_of(r * NORM_ROWS, NORM_ROWS), NORM_ROWS)
            dst = pl.ds(pl.multiple_of(halo + r * NORM_ROWS, halo), NORM_ROWS)
            h_ref[dst, :] = _rms_mod(x_ref[src, :], g, scale, shift).astype(BF16)
            return carry

        lax.fori_loop(0, tm // NORM_ROWS, body, 0)

    def conv(half, w_ref, cw_ref, cb_ref):
        u_ref[half] = jnp.dot(h_ref[...], w_ref[...], preferred_element_type=F32)
        y = cb_ref[...]
        for t in range(CONV_WIDTH):
            lag = CONV_WIDTH - 1 - t
            y = y + cw_ref[0, t:t + 1, :] * u_ref[half, halo - lag:halo - lag + tm, :]
        return y

    val = conv(0, wv_ref, cwv_ref, cbv_ref)
    gt = conv(1, wg_ref, cwg_ref, cbg_ref)
    act = (_silu(gt) * val).astype(BF16)
    o_ref[...] += jnp.dot(act, wd_ref[...], preferred_element_type=F32)

    @pl.when(j == pl.num_programs(1) - 1)
    def _():
        o_ref[...] = x_ref[...] + mod_ref[0, GATE2:GATE2 + 1, :] * o_ref[...]


def _ffn(x1, mod3, norm_g, w_up, conv_w, conv_b, w_down):
    tm, tn, halo, nj = FFN_TM, FFN_TN, FFN_HALO, FFN_NJ
    vmem = (4 * tm * D_MODEL * 4 + 2 * halo * D_MODEL * 4
            + 2 * 3 * D_MODEL * tn * 2
            + (tm + halo) * D_MODEL * 2 + 2 * (tm + halo) * tn * 4
            + tm * D_MODEL * 4 + 8 * tm * tn * 4 + 4 * MIB)
    return pl.pallas_call(
        _ffn_kernel,
        grid=(TOKENS // tm, nj),
        in_specs=[pl.BlockSpec((tm, D_MODEL), lambda i, j: (i, 0)),
                  pl.BlockSpec((halo, D_MODEL),
                               lambda i, j: (jnp.maximum(i * (tm // halo) - 1, 0), 0)),
                  pl.BlockSpec((1, N_MOD, D_MODEL), lambda i, j: (i // (SEQ // tm), 0, 0)),
                  pl.BlockSpec((1, D_MODEL), lambda i, j: (0, 0)),
                  pl.BlockSpec((D_MODEL, tn), lambda i, j: (0, j)),
                  pl.BlockSpec((D_MODEL, tn), lambda i, j: (0, nj + j)),
                  pl.BlockSpec((1, CONV_WIDTH, tn), lambda i, j: (0, 0, j)),
                  pl.BlockSpec((1, CONV_WIDTH, tn), lambda i, j: (0, 0, nj + j)),
                  pl.BlockSpec((1, tn), lambda i, j: (0, j)),
                  pl.BlockSpec((1, tn), lambda i, j: (0, nj + j)),
                  pl.BlockSpec((tn, D_MODEL), lambda i, j: (j, 0))],
        out_specs=pl.BlockSpec((tm, D_MODEL), lambda i, j: (i, 0)),
        out_shape=jax.ShapeDtypeStruct((TOKENS, D_MODEL), F32),
        scratch_shapes=[pltpu.VMEM((tm + halo, D_MODEL), BF16),
                        pltpu.VMEM((2, tm + halo, tn), F32)],
        compiler_params=_params(("arbitrary", "arbitrary"), vmem, FFN_FLAGS),
        name="ffn",
    )(x1, x1, mod3, norm_g, w_up, w_up, conv_w, conv_w, conv_b, conv_b, w_down)


def kernel(x, c, w_ada, b_ada, norm1_g, w_in, q_norm_g, k_norm_g, rel_bias, ret_norm_g,
           w_attn_br, w_ret_br, w_o, norm2_g, w_up, conv_w, conv_b, w_down):
    assert x.shape == (BATCH, SEQ, D_MODEL) and w_ada.shape[0] == 1
    assert _far_bucket_is_last()
    layer = 0
    x2d = x.reshape(TOKENS, D_MODEL)

    c_pad = jnp.pad(c, ((0, 8 - BATCH), (0, 0)))
    mod = _modulation(c_pad, w_ada[layer], b_ada)[:BATCH]
    mod3 = mod.reshape(BATCH, N_MOD, D_MODEL)

    proj = _input_projection(x2d, mod3, norm1_g, w_in[layer])
    proj3 = proj.reshape(BATCH, SEQ, IN_WIDTH)

    bias_tiles = _relbias_tiles(rel_bias)
    ya = _moba(proj3, rel_bias, q_norm_g, k_norm_g, bias_tiles)
    yr = _retention(proj3, ret_norm_g)

    x1 = _mixer(ya.reshape(TOKENS, ATTN_WIDTH), yr.reshape(TOKENS, RET_V_WIDTH), proj,
                w_attn_br[layer].astype(BF16), w_ret_br[layer].astype(BF16),
                w_o[layer].astype(BF16), x2d, mod3)

    out = _ffn(x1, mod3, norm2_g, w_up[layer].astype(BF16), conv_w, conv_b,
               w_down[layer].astype(BF16))
    return out.reshape(BATCH, SEQ, D_MODEL)
```

```python
import functools
import math

import numpy as np
import jax
import jax.numpy as jnp
from jax import lax
from jax.experimental import pallas as pl
from jax.experimental.pallas import tpu as pltpu

F32 = jnp.float32
BF16 = jnp.bfloat16

D_MODEL = 2048
BATCH = 4
SEQ = 2048
ATTN_HEADS = 8
ATTN_HEAD_DIM = 128
MOBA_BLOCK = 256
MOBA_TOPK = 3
REL_BUCKETS = 32
REL_MAX_DIST = 128
RET_HEADS = 8
RET_KEY_DIM = 128
RET_VAL_DIM = 256
RET_CHUNK = 128
ROPE_BASE = 10000.0
FFN_DIM = 5632
CONV_WIDTH = 3
EPS = 1e-6
N_MOD = 6

ATTN_WIDTH = ATTN_HEADS * ATTN_HEAD_DIM
RET_QK_WIDTH = RET_HEADS * RET_KEY_DIM
RET_V_WIDTH = RET_HEADS * RET_VAL_DIM
OFF_QA = 0
OFF_KA = OFF_QA + ATTN_WIDTH
OFF_VA = OFF_KA + ATTN_WIDTH
OFF_QR = OFF_VA + ATTN_WIDTH
OFF_KR = OFF_QR + RET_QK_WIDTH
OFF_VR = OFF_KR + RET_QK_WIDTH
OFF_GR = OFF_VR + RET_V_WIDTH
OFF_GA = OFF_GR + RET_V_WIDTH
OFF_GB = OFF_GA + D_MODEL
IN_WIDTH = OFF_GB + D_MODEL

TOKENS = BATCH * SEQ
N_BLOCKS = SEQ // MOBA_BLOCK
N_CHUNKS = SEQ // RET_CHUNK
MASK_VALUE = -1e30
LOG2E = math.log2(math.e)
MOBA_ONES_ROWS = 16
MOBA_TILE_GROUPS = ((0, 7, 3, 4, 1, 6, 2, 5),)
MOBA_GROUP_SIZE = max(len(g) for g in MOBA_TILE_GROUPS)
MIB = 1024 * 1024

SHIFT1, SCALE1, GATE1, SHIFT2, SCALE2, GATE2 = range(N_MOD)

MOD_TN = 1024
INPROJ_TM, INPROJ_TN = 1024, 1024
MIX_TM, MIX_TN = 512, 512
FFN_TM, FFN_TN = 1024, 512
FFN_HALO = 16
FFN_NJ = FFN_DIM // FFN_TN
FFN_STEPS = (TOKENS // FFN_TM) * FFN_NJ
FFN_PIECES = 2
FFN_FLAGS = None
NORM_ROWS = 64


def _params(semantics, vmem_bytes, flags=None):
    return pltpu.CompilerParams(dimension_semantics=semantics,
                                vmem_limit_bytes=int(vmem_bytes), flags=flags)


def _sigmoid(v):
    return 0.5 + 0.5 * jnp.tanh(0.5 * v)


def _silu(v):
    hv = 0.5 * v
    return hv + hv * jnp.tanh(hv)


def _cast_rider(weights, steps, step_of):
    in_specs, out_specs, out_shapes = [], [], []
    for w in weights:
        rows, cols = w.shape
        assert rows % (steps * 16) == 0, (rows, steps)
        slab = rows // steps
        in_specs.append(pl.BlockSpec((slab, cols), lambda *g: (step_of(*g), 0)))
        out_specs.append(pl.BlockSpec((slab, cols), lambda *g: (step_of(*g), 0)))
        out_shapes.append(jax.ShapeDtypeStruct((rows, cols), BF16))
    return in_specs, out_specs, out_shapes


def _run_cast_rider(src_refs, dst_refs):
    for src, dst in zip(src_refs, dst_refs):
        dst[...] = src[...].astype(BF16)


def _rms_mod(x, g, scale, shift):
    ms = jnp.mean(x * x, axis=-1, keepdims=True)
    return ((x * lax.rsqrt(ms + EPS)) * g) * (1.0 + scale) + shift


def _split_bf16(v):
    hi = v.astype(BF16)
    lo = (v - hi.astype(F32)).astype(BF16)
    return hi, lo


def _mod_kernel(c_ref, w_ref, b_ref, o_ref):
    rows = c_ref.shape[0]
    s_hi, s_lo = _split_bf16(_silu(c_ref[...]))
    w_hi, w_lo = _split_bf16(w_ref[...])
    both = jnp.dot(jnp.concatenate([s_hi, s_lo], axis=0), w_hi, preferred_element_type=F32)
    cross = jnp.dot(s_hi, w_lo, preferred_element_type=F32)
    o_ref[...] = (both[0:rows] + both[rows:]) + cross + b_ref[...]


def _modulation(c_pad, w_ada, b_ada):
    rows = c_pad.shape[0]
    n = w_ada.shape[1]
    return pl.pallas_call(
        _mod_kernel,
        grid=(n // MOD_TN,),
        in_specs=[pl.BlockSpec((rows, D_MODEL), lambda j: (0, 0)),
                  pl.BlockSpec((D_MODEL, MOD_TN), lambda j: (0, j)),
                  pl.BlockSpec((1, MOD_TN), lambda j: (0, j))],
        out_specs=pl.BlockSpec((rows, MOD_TN), lambda j: (0, j)),
        out_shape=jax.ShapeDtypeStruct((rows, n), F32),
        compiler_params=_params(("arbitrary",), 2 * D_MODEL * MOD_TN * 4 + 8 * MIB),
        name="mod",
    )(c_pad, w_ada, b_ada)


def _inproj_kernel(x_ref, mod_ref, g_ref, w_ref, o_ref, h_ref):
    @pl.when(pl.program_id(1) == 0)
    def _():
        g = g_ref[...]
        shift = mod_ref[0, SHIFT1:SHIFT1 + 1, :]
        scale = mod_ref[0, SCALE1:SCALE1 + 1, :]

        def body(r, carry):
            rows = pl.ds(pl.multiple_of(r * NORM_ROWS, NORM_ROWS), NORM_ROWS)
            h_ref[rows, :] = _rms_mod(x_ref[rows, :], g, scale, shift).astype(BF16)
            return carry

        lax.fori_loop(0, INPROJ_TM // NORM_ROWS, body, 0)

    o_ref[...] = jnp.dot(h_ref[...], w_ref[...].astype(BF16),
                         preferred_element_type=F32).astype(o_ref.dtype)


def _input_projection(x2d, mod3, norm_g, w_in):
    tm, tn = INPROJ_TM, INPROJ_TN
    vmem = (2 * tm * D_MODEL * 4 + 2 * D_MODEL * tn * 4 + 2 * tm * tn * 2
            + tm * D_MODEL * 2 + D_MODEL * tn * 2 + tm * tn * 4 + 4 * MIB)
    return pl.pallas_call(
        _inproj_kernel,
        grid=(TOKENS // tm, IN_WIDTH // tn),
        in_specs=[pl.BlockSpec((tm, D_MODEL), lambda i, j: (i, 0)),
                  pl.BlockSpec((1, N_MOD, D_MODEL), lambda i, j: (i // (SEQ // tm), 0, 0)),
                  pl.BlockSpec((1, D_MODEL), lambda i, j: (0, 0)),
                  pl.BlockSpec((D_MODEL, tn), lambda i, j: (0, j))],
        out_specs=pl.BlockSpec((tm, tn), lambda i, j: (i, j)),
        out_shape=jax.ShapeDtypeStruct((TOKENS, IN_WIDTH), BF16),
        scratch_shapes=[pltpu.VMEM((tm, D_MODEL), BF16)],
        compiler_params=_params(("arbitrary", "arbitrary"), vmem),
        name="inproj",
    )(x2d, mod3, norm_g, w_in)


def _relbias_kernel(rb_ref, o_ref):
    h = pl.program_id(0)
    shape = (2 * MOBA_BLOCK, MOBA_BLOCK)
    key = lax.broadcasted_iota(jnp.int32, shape, 0)
    qry = lax.broadcasted_iota(jnp.int32, shape, 1)
    dist = qry - key + MOBA_BLOCK
    n = jnp.maximum(dist, 0)
    max_exact = REL_BUCKETS // 2
    nf = jnp.maximum(n, 1).astype(F32)
    large = max_exact + (jnp.log(nf / max_exact) / math.log(REL_MAX_DIST / max_exact)
                         * (REL_BUCKETS - max_exact)).astype(jnp.int32)
    large = jnp.minimum(large, REL_BUCKETS - 1)
    bucket = jnp.where(n < max_exact, n, large)
    bias = jnp.zeros(shape, F32)
    for b in range(REL_BUCKETS):
        bias = jnp.where(bucket == b, rb_ref[b, h], bias)
    o_ref[0] = jnp.where(dist >= 0, bias * LOG2E, MASK_VALUE)


def _relbias_tiles(rel_bias):
    return pl.pallas_call(
        _relbias_kernel,
        grid=(ATTN_HEADS,),
        in_specs=[pl.BlockSpec(memory_space=pltpu.SMEM)],
        out_specs=pl.BlockSpec((1, 2 * MOBA_BLOCK, MOBA_BLOCK), lambda h: (h, 0, 0)),
        out_shape=jax.ShapeDtypeStruct((ATTN_HEADS, 2 * MOBA_BLOCK, MOBA_BLOCK), F32),
        compiler_params=_params(("arbitrary",), 16 * MIB),
        name="relbias",
    )(rel_bias)


def _far_bucket_is_last():
    d = np.arange(MOBA_BLOCK + 1, SEQ, dtype=np.float32)
    max_exact = REL_BUCKETS // 2
    large = max_exact + (np.log(d / max_exact) / math.log(REL_MAX_DIST / max_exact)
                         * (REL_BUCKETS - max_exact)).astype(np.int32)
    return bool(np.all(np.minimum(large, REL_BUCKETS - 1) == REL_BUCKETS - 1))


def _moba_kernel(n_cast, rb_ref, q_ref, k_ref, v_ref, gq_ref, gk_ref, bt_ref, *refs):
    cast_src, (o_ref, *cast_dst) = refs[:n_cast], refs[n_cast:2 * n_cast + 1]
    qb_ref, kb_ref, vt_ref, *s_refs = refs[2 * n_cast + 1:]
    _run_cast_rider(cast_src, cast_dst)

    h = pl.program_id(1)
    far_bias = rb_ref[REL_BUCKETS - 1, h] * LOG2E
    blk, hd = MOBA_BLOCK, ATTN_HEAD_DIM
    nt = (((1,), (1,)), ((), ()))

    q = q_ref[0].astype(F32)
    k = k_ref[0].astype(F32)
    qn = (q * lax.rsqrt(jnp.mean(q * q, axis=-1, keepdims=True) + EPS)) * gq_ref[...]
    kn = (k * lax.rsqrt(jnp.mean(k * k, axis=-1, keepdims=True) + EPS)) * gk_ref[...]
    qb_ref[...] = (qn * (hd ** -0.5 * LOG2E)).astype(BF16)
    kb_ref[...] = kn.astype(BF16)
    vt_ref[0:hd, :] = v_ref[0].astype(F32).T.astype(BF16)
    vt_ref[hd:, :] = jnp.ones((MOBA_ONES_ROWS, SEQ), BF16)

    k_mean = jnp.concatenate(
        [jnp.sum(kn[n * blk:(n + 1) * blk], axis=0, keepdims=True) for n in range(N_BLOCKS)],
        axis=0) * (1.0 / blk)
    gate = lax.dot_general(k_mean, qn, nt, preferred_element_type=F32,
                           precision=lax.Precision.HIGHEST)
    row = lax.broadcasted_iota(jnp.int32, (N_BLOCKS, blk), 0)

    def scores(qi, s_ref):
        cols = slice(qi * blk, (qi + 1) * blk)
        nk = (qi + 1) * blk

        mask_add = None
        if qi > MOBA_TOPK:
            g = gate[:, cols]
            rank = jnp.zeros((N_BLOCKS, blk), F32)
            for m in range(qi):
                gm = g[m:m + 1, :]
                beats = (gm > g) | ((gm == g) & (row > m))
                rank = rank + jnp.where(beats, 1.0, 0.0)
            mask_add = jnp.where(rank < MOBA_TOPK, 0.0, MASK_VALUE)

        s_all = lax.dot_general(kb_ref[0:nk, :], qb_ref[cols, :], nt,
                                preferred_element_type=F32)
        m8 = None
        for n in range(qi + 1):
            s_blk = s_all[n * blk:(n + 1) * blk]
            if n == qi:
                s_blk = s_blk + bt_ref[0, blk:2 * blk, :]
            else:
                if n == qi - 1:
                    s_blk = s_blk + bt_ref[0, 0:blk, :]
                    if mask_add is not None:
                        s_blk = s_blk + mask_add[n:n + 1, :]
                elif mask_add is not None:
                    s_blk = s_blk + (mask_add[n:n + 1, :] + far_bias)
                else:
                    s_blk = s_blk + far_bias
            s_ref[n * blk:(n + 1) * blk, :] = s_blk
            b8 = jnp.max(s_blk.reshape(blk // 8, 8, blk), axis=0)
            m8 = b8 if m8 is None else jnp.maximum(m8, b8)
        return jnp.max(m8, axis=0, keepdims=True)

    def attend(qi, s_ref, m):
        cols = slice(qi * blk, (qi + 1) * blk)
        nk = (qi + 1) * blk
        p = jnp.exp2(s_ref[0:nk, :] - m).astype(BF16)
        o_aug = jnp.dot(vt_ref[:, 0:nk], p, preferred_element_type=F32)
        o_t = o_aug[0:hd] * (1.0 / o_aug[hd:hd + 1])
        o_ref[0, cols, :] = o_t.T.astype(o_ref.dtype)

    for group in MOBA_TILE_GROUPS:
        maxes = [scores(qi, s_ref) for qi, s_ref in zip(group, s_refs)]
        for qi, s_ref, m in zip(group, s_refs, maxes):
            attend(qi, s_ref, m)


def _moba(proj3, rel_bias, q_norm_g, k_norm_g, bias_tiles, cast_weights):
    hd = ATTN_HEAD_DIM
    head = lambda off: (lambda b, h: (b, 0, off // hd + h))
    cast_in, cast_out, cast_shapes = _cast_rider(
        cast_weights, BATCH * ATTN_HEADS, lambda b, h: b * ATTN_HEADS + h)
    cast_bytes = sum(2 * (4 + 2) * w.size // (BATCH * ATTN_HEADS) for w in cast_weights)
    outs = pl.pallas_call(
        functools.partial(_moba_kernel, len(cast_weights)),
        grid=(BATCH, ATTN_HEADS),
        in_specs=[pl.BlockSpec(memory_space=pltpu.SMEM),
                  pl.BlockSpec((1, SEQ, hd), head(OFF_QA)),
                  pl.BlockSpec((1, SEQ, hd), head(OFF_KA)),
                  pl.BlockSpec((1, SEQ, hd), head(OFF_VA)),
                  pl.BlockSpec((1, hd), lambda b, h: (0, 0)),
                  pl.BlockSpec((1, hd), lambda b, h: (0, 0)),
                  pl.BlockSpec((1, 2 * MOBA_BLOCK, MOBA_BLOCK), lambda b, h: (h, 0, 0))]
                 + cast_in,
        out_specs=[pl.BlockSpec((1, SEQ, hd), lambda b, h: (b, 0, h))] + cast_out,
        out_shape=[jax.ShapeDtypeStruct((BATCH, SEQ, ATTN_WIDTH), BF16)] + cast_shapes,
        scratch_shapes=[pltpu.VMEM((SEQ, hd), BF16),
                        pltpu.VMEM((SEQ, hd), BF16),
                        pltpu.VMEM((hd + MOBA_ONES_ROWS, SEQ), BF16)]
                       + [pltpu.VMEM((SEQ, MOBA_BLOCK), F32)] * MOBA_GROUP_SIZE,
        compiler_params=_params(("arbitrary", "arbitrary"), 24 * MIB + cast_bytes),
        name="moba",
    )(rel_bias, proj3, proj3, proj3, q_norm_g, k_norm_g, bias_tiles, *cast_weights)
    return outs[0], outs[1:]


def _retention_kernel(n_cast, cd_ref, q_ref, k_ref, v_ref, gr_ref, cos_ref, sin_ref,
                      dmask_ref, qdec_ref, kdec_ref, gn_ref, *refs):
    cast_src, (o_ref, *cast_dst) = refs[:n_cast], refs[n_cast:2 * n_cast + 1]
    qf_ref, kf_ref, y_ref, kv_ref = refs[2 * n_cast + 1:]
    _run_cast_rider(cast_src, cast_dst)
    h = pl.program_id(1)
    chunk_decay = cd_ref[h]
    half = RET_KEY_DIM // 2
    cos = cos_ref[...]
    sin = sin_ref[...]
    q = q_ref[0].astype(F32)
    k = k_ref[0].astype(F32)
    qf_ref[...] = q * cos + pltpu.roll(q, half, 1) * sin
    kf_ref[...] = (k * cos + pltpu.roll(k, half, 1) * sin) * (RET_KEY_DIM ** -0.5)

    nt = (((1,), (1,)), ((), ()))
    chunks = [slice(c * RET_CHUNK, (c + 1) * RET_CHUNK) for c in range(N_CHUNKS)]

    scores = [(lax.dot_general(qf_ref[rows, :].astype(BF16), kf_ref[rows, :].astype(BF16), nt,
                               preferred_element_type=F32) * dmask_ref[0]).astype(BF16)
              for rows in chunks]
    for c, rows in enumerate(chunks):
        vc = v_ref[0, rows, :]
        y_ref[rows, :] = jnp.dot(scores[c], vc, preferred_element_type=F32)
        kd_t = (kf_ref[rows, :] * kdec_ref[0]).T.astype(BF16)
        kv_ref[c] = jnp.dot(kd_t, vc, preferred_element_type=F32)

    state = jnp.zeros((RET_KEY_DIM, RET_VAL_DIM), F32)
    for c, rows in enumerate(chunks):
        y_ref[rows, :] += jnp.dot((qf_ref[rows, :] * qdec_ref[0]).astype(BF16),
                                  state.astype(BF16), preferred_element_type=F32)
        state = chunk_decay * state + kv_ref[c]

    for rows in chunks:
        y = y_ref[rows, :]
        mu = jnp.mean(y, axis=-1, keepdims=True)
        yc = y - mu
        var = jnp.mean(yc * yc, axis=-1, keepdims=True)
        yn = (yc * lax.rsqrt(var + EPS)) * gn_ref[...]
        o_ref[0, rows, :] = (yn * _silu(gr_ref[0, rows, :]).astype(F32)).astype(o_ref.dtype)


def _retention_tables():
    half = RET_KEY_DIM // 2
    freqs = jnp.power(ROPE_BASE, -jnp.arange(half, dtype=F32) / half)
    ang = jnp.arange(SEQ).astype(F32)[:, None] * freqs[None, :]
    cos, sin = jnp.cos(ang), jnp.sin(ang)
    cos_full = jnp.concatenate([cos, cos], axis=-1)
    sin_signed = jnp.concatenate([-sin, sin], axis=-1)

    log_decay = jnp.log(1.0 - jnp.power(2.0, -5.0 - jnp.arange(RET_HEADS, dtype=F32)))
    i = jnp.arange(RET_CHUNK, dtype=F32)
    diff = i[:, None] - i[None, :]
    ld = log_decay[:, None, None]
    inner_decay = jnp.where(diff >= 0, jnp.exp(ld * jnp.maximum(diff, 0.0)), 0.0)
    q_decay = jnp.exp(log_decay[:, None] * (i + 1.0))
    k_decay = jnp.exp(log_decay[:, None] * (RET_CHUNK - 1.0 - i))
    chunk_decay = jnp.exp(log_decay * RET_CHUNK)
    bcast = lambda t: jnp.broadcast_to(t[:, :, None], (RET_HEADS, RET_CHUNK, RET_KEY_DIM))
    return cos_full, sin_signed, inner_decay, bcast(q_decay), bcast(k_decay), chunk_decay


def _retention(proj3, ret_norm_g, cast_weights):
    dk, dv = RET_KEY_DIM, RET_VAL_DIM
    cos, sin, dmask, qdec, kdec, chunk_decay = _retention_tables()
    head = lambda off, w: (lambda b, h: (b, 0, off // w + h))
    per_head = lambda b, h: (h, 0, 0)
    cast_in, cast_out, cast_shapes = _cast_rider(
        cast_weights, BATCH * RET_HEADS, lambda b, h: b * RET_HEADS + h)
    cast_bytes = sum(2 * (4 + 2) * w.size // (BATCH * RET_HEADS) for w in cast_weights)
    outs = pl.pallas_call(
        functools.partial(_retention_kernel, len(cast_weights)),
        grid=(BATCH, RET_HEADS),
        in_specs=[pl.BlockSpec(memory_space=pltpu.SMEM),
                  pl.BlockSpec((1, SEQ, dk), head(OFF_QR, dk)),
                  pl.BlockSpec((1, SEQ, dk), head(OFF_KR, dk)),
                  pl.BlockSpec((1, SEQ, dv), head(OFF_VR, dv)),
                  pl.BlockSpec((1, SEQ, dv), head(OFF_GR, dv)),
                  pl.BlockSpec((SEQ, dk), lambda b, h: (0, 0)),
                  pl.BlockSpec((SEQ, dk), lambda b, h: (0, 0)),
                  pl.BlockSpec((1, RET_CHUNK, RET_CHUNK), per_head),
                  pl.BlockSpec((1, RET_CHUNK, dk), per_head),
                  pl.BlockSpec((1, RET_CHUNK, dk), per_head),
                  pl.BlockSpec((1, dv), lambda b, h: (0, h))] + cast_in,
        out_specs=[pl.BlockSpec((1, SEQ, dv), lambda b, h: (b, 0, h))] + cast_out,
        out_shape=[jax.ShapeDtypeStruct((BATCH, SEQ, RET_V_WIDTH), BF16)] + cast_shapes,
        scratch_shapes=[pltpu.VMEM((SEQ, dk), F32), pltpu.VMEM((SEQ, dk), F32),
                        pltpu.VMEM((SEQ, dv), F32), pltpu.VMEM((N_CHUNKS, dk, dv), F32)],
        compiler_params=_params(("arbitrary", "arbitrary"), 24 * MIB + cast_bytes),
        name="retention",
    )(chunk_decay, proj3, proj3, proj3, proj3, cos, sin, dmask, qdec, kdec, ret_norm_g,
      *cast_weights)
    return outs[0], outs[1:]


def _mixer_kernel(ya_ref, yr_ref, ga_ref, gb_ref, wa_ref, wr_ref, wo_ref, x_ref, mod_ref,
                  o_ref):
    j = pl.program_id(1)

    @pl.when(j == 0)
    def _():
        o_ref[...] = jnp.zeros_like(o_ref)

    a = jnp.dot(ya_ref[...], wa_ref[...], preferred_element_type=F32)
    r = jnp.dot(yr_ref[...], wr_ref[...], preferred_element_type=F32)
    merged = (_sigmoid(ga_ref[...].astype(F32)) * a
              + _sigmoid(gb_ref[...].astype(F32)) * r)
    o_ref[...] += jnp.dot(merged.astype(BF16), wo_ref[...], preferred_element_type=F32)

    @pl.when(j == pl.num_programs(1) - 1)
    def _():
        o_ref[...] = x_ref[...] + mod_ref[0, GATE1:GATE1 + 1, :] * o_ref[...]


def _mixer(ya2d, yr2d, proj2d, w_attn_br, w_ret_br, w_o, x2d, mod3):
    tm, tn = MIX_TM, MIX_TN
    vmem = (2 * tm * (ATTN_WIDTH + RET_V_WIDTH + 2 * tn) * 2
            + 2 * (ATTN_WIDTH + RET_V_WIDTH + D_MODEL) * tn * 2
            + 4 * tm * D_MODEL * 4 + tm * D_MODEL * 4 + 6 * tm * tn * 4 + 4 * MIB)
    return pl.pallas_call(
        _mixer_kernel,
        grid=(TOKENS // tm, D_MODEL // tn),
        in_specs=[pl.BlockSpec((tm, ATTN_WIDTH), lambda i, j: (i, 0)),
                  pl.BlockSpec((tm, RET_V_WIDTH), lambda i, j: (i, 0)),
                  pl.BlockSpec((tm, tn), lambda i, j: (i, OFF_GA // tn + j)),
                  pl.BlockSpec((tm, tn), lambda i, j: (i, OFF_GB // tn + j)),
                  pl.BlockSpec((ATTN_WIDTH, tn), lambda i, j: (0, j)),
                  pl.BlockSpec((RET_V_WIDTH, tn), lambda i, j: (0, j)),
                  pl.BlockSpec((tn, D_MODEL), lambda i, j: (j, 0)),
                  pl.BlockSpec((tm, D_MODEL), lambda i, j: (i, 0)),
                  pl.BlockSpec((1, N_MOD, D_MODEL), lambda i, j: (i // (SEQ // tm), 0, 0))],
        out_specs=pl.BlockSpec((tm, D_MODEL), lambda i, j: (i, 0)),
        out_shape=jax.ShapeDtypeStruct((TOKENS, D_MODEL), F32),
        compiler_params=_params(("arbitrary", "arbitrary"), vmem),
        name="mixer",
    )(ya2d, yr2d, proj2d, proj2d, w_attn_br, w_ret_br, w_o, x2d, mod3)


def _ffn_kernel(x_ref, halo_ref, mod_ref, g_ref, wv_ref, wg_ref, cwv_ref, cwg_ref,
                cbv_ref, cbg_ref, wd_ref, o_ref, h_ref, u_ref):
    i = pl.program_id(0)
    j = pl.program_id(1)
    tm, halo = FFN_TM, FFN_HALO

    @pl.when(j == 0)
    def _():
        o_ref[...] = jnp.zeros_like(o_ref)
        g = g_ref[...]
        shift = mod_ref[0, SHIFT2:SHIFT2 + 1, :]
        scale = mod_ref[0, SCALE2:SCALE2 + 1, :]
        seq_start = (i % (SEQ // tm)) == 0
        h_halo = _rms_mod(halo_ref[...], g, scale, shift)
        h_ref[0:halo, :] = jnp.where(seq_start, 0.0, h_halo).astype(BF16)

        def body(r, carry):
            src = pl.ds(pl.multiple_of(r * NORM_ROWS, NORM_ROWS), NORM_ROWS)
            dst = pl.ds(pl.multiple_of(halo + r * NORM_ROWS, halo), NORM_ROWS)
            h_ref[dst, :] = _rms_mod(x_ref[src, :], g, scale, shift).astype(BF16)
            return carry

        lax.fori_loop(0, tm // NORM_ROWS, body, 0)

    def conv(half, w_ref, cw_ref, cb_ref):
        u_ref[half] = jnp.dot(h_ref[...], w_ref[...], preferred_element_type=F32)
        y = cb_ref[...]
        for t in range(CONV_WIDTH):
            lag = CONV_WIDTH - 1 - t
            y = y + cw_ref[0, t:t + 1, :] * u_ref[half, halo - lag:halo - lag + tm, :]
        return y

    val = conv(0, wv_ref, cwv_ref, cbv_ref)
    gt = conv(1, wg_ref, cwg_ref, cbg_ref)
    act = (_silu(gt) * val).astype(BF16)
    o_ref[...] += jnp.dot(act, wd_ref[...], preferred_element_type=F32)

    @pl.when(j == pl.num_programs(1) - 1)
    def _():
        o_ref[...] = x_ref[...] + mod_ref[0, GATE2:GATE2 + 1, :] * o_ref[...]


def _ffn(x1, mod3, norm_g, w_up, conv_w, conv_b, w_down):
    tm, tn, halo, nj = FFN_TM, FFN_TN, FFN_HALO, FFN_NJ
    vmem = (4 * tm * D_MODEL * 4 + 2 * halo * D_MODEL * 4
            + 2 * 3 * D_MODEL * tn * 2
            + (tm + halo) * D_MODEL * 2 + 2 * (tm + halo) * tn * 4
            + tm * tn * 4 + 4 * MIB)
    return pl.pallas_call(
        _ffn_kernel,
        grid=(TOKENS // tm, nj),
        in_specs=[pl.BlockSpec((tm, D_MODEL), lambda i, j: (i, 0)),
                  pl.BlockSpec((halo, D_MODEL),
                               lambda i, j: (jnp.maximum(i * (tm // halo) - 1, 0), 0)),
                  pl.BlockSpec((1, N_MOD, D_MODEL), lambda i, j: (i // (SEQ // tm), 0, 0)),
                  pl.BlockSpec((1, D_MODEL), lambda i, j: (0, 0)),
                  pl.BlockSpec((D_MODEL, tn), lambda i, j: (0, j)),
                  pl.BlockSpec((D_MODEL, tn), lambda i, j: (0, nj + j)),
                  pl.BlockSpec((1, CONV_WIDTH, tn), lambda i, j: (0, 0, j)),
                  pl.BlockSpec((1, CONV_WIDTH, tn), lambda i, j: (0, 0, nj + j)),
                  pl.BlockSpec((1, tn), lambda i, j: (0, j)),
                  pl.BlockSpec((1, tn), lambda i, j: (0, nj + j)),
                  pl.BlockSpec((tn, D_MODEL), lambda i, j: (j, 0))],
        out_specs=pl.BlockSpec((tm, D_MODEL), lambda i, j: (i, 0)),
        out_shape=jax.ShapeDtypeStruct((TOKENS, D_MODEL), F32),
        scratch_shapes=[pltpu.VMEM((tm + halo, D_MODEL), BF16),
                        pltpu.VMEM((2, tm + halo, tn), F32)],
        compiler_params=_params(("arbitrary", "arbitrary"), vmem, FFN_FLAGS),
        name="ffn",
    )(x1, x1, mod3, norm_g, w_up, w_up, conv_w, conv_w, conv_b, conv_b, w_down)


def kernel(x, c, w_ada, b_ada, norm1_g, w_in, q_norm_g, k_norm_g, rel_bias, ret_norm_g,
           w_attn_br, w_ret_br, w_o, norm2_g, w_up, conv_w, conv_b, w_down):
    assert x.shape == (BATCH, SEQ, D_MODEL) and w_ada.shape[0] == 1
    assert _far_bucket_is_last()
    layer = 0
    x2d = x.reshape(TOKENS, D_MODEL)

    c_pad = jnp.pad(c, ((0, 8 - BATCH), (0, 0)))
    mod = _modulation(c_pad, w_ada[layer], b_ada)[:BATCH]
    mod3 = mod.reshape(BATCH, N_MOD, D_MODEL)

    proj = _input_projection(x2d, mod3, norm1_g, w_in[layer])
    proj3 = proj.reshape(BATCH, SEQ, IN_WIDTH)

    bias_tiles = _relbias_tiles(rel_bias)
    yr, (w_attn_b, w_ret_b, w_o_b) = _retention(
        proj3, ret_norm_g, (w_attn_br[layer], w_ret_br[layer], w_o[layer]))
    ya, (w_up_b, w_down_b) = _moba(proj3, rel_bias, q_norm_g, k_norm_g, bias_tiles,
                                   (w_up[layer], w_down[layer]))

    x1 = _mixer(ya.reshape(TOKENS, ATTN_WIDTH), yr.reshape(TOKENS, RET_V_WIDTH), proj,
                w_attn_b, w_ret_b, w_o_b, x2d, mod3)

    out = _ffn(x1, mod3, norm2_g, w_up_b, conv_w, conv_b, w_down_b)
    return out.reshape(BATCH, SEQ, D_MODEL)
```

```python
import functools
import math

import numpy as np
import jax
import jax.numpy as jnp
from jax import lax
from jax.experimental import pallas as pl
from jax.experimental.pallas import tpu as pltpu

F32 = jnp.float32
BF16 = jnp.bfloat16

D_MODEL = 2048
BATCH = 4
SEQ = 2048
ATTN_HEADS = 8
ATTN_HEAD_DIM = 128
MOBA_BLOCK = 256
MOBA_TOPK = 3
REL_BUCKETS = 32
REL_MAX_DIST = 128
RET_HEADS = 8
RET_KEY_DIM = 128
RET_VAL_DIM = 256
RET_CHUNK = 128
ROPE_BASE = 10000.0
FFN_DIM = 5632
CONV_WIDTH = 3
EPS = 1e-6
N_MOD = 6

ATTN_WIDTH = ATTN_HEADS * ATTN_HEAD_DIM
RET_QK_WIDTH = RET_HEADS * RET_KEY_DIM
RET_V_WIDTH = RET_HEADS * RET_VAL_DIM
OFF_QA = 0
OFF_KA = OFF_QA + ATTN_WIDTH
OFF_VA = OFF_KA + ATTN_WIDTH
OFF_QR = OFF_VA + ATTN_WIDTH
OFF_KR = OFF_QR + RET_QK_WIDTH
OFF_VR = OFF_KR + RET_QK_WIDTH
OFF_GR = OFF_VR + RET_V_WIDTH
OFF_GA = OFF_GR + RET_V_WIDTH
OFF_GB = OFF_GA + D_MODEL
IN_WIDTH = OFF_GB + D_MODEL

TOKENS = BATCH * SEQ
N_BLOCKS = SEQ // MOBA_BLOCK
N_CHUNKS = SEQ // RET_CHUNK
MASK_VALUE = -1e30
LOG2E = math.log2(math.e)
MOBA_ONES_ROWS = 16
MOBA_TILE_GROUPS = ((0, 7, 3, 4, 1, 6, 2, 5),)
MOBA_GROUP_SIZE = max(len(g) for g in MOBA_TILE_GROUPS)
MIB = 1024 * 1024

SHIFT1, SCALE1, GATE1, SHIFT2, SCALE2, GATE2 = range(N_MOD)

MOD_TN = 1024
INPROJ_TM, INPROJ_TN = 1024, 1024
MIX_TM, MIX_TN = 512, 512
FFN_TM, FFN_TN = 1024, 512
FFN_HALO = 16
FFN_NJ = FFN_DIM // FFN_TN
FFN_STEPS = (TOKENS // FFN_TM) * FFN_NJ
FFN_PIECES = 2
FFN_FLAGS = None
NORM_ROWS = 16
NORM_UNROLL = 16


def _params(semantics, vmem_bytes, flags=None):
    return pltpu.CompilerParams(dimension_semantics=semantics,
                                vmem_limit_bytes=int(vmem_bytes), flags=flags)


def _sigmoid(v):
    return 0.5 + 0.5 * jnp.tanh(0.5 * v)


def _silu(v):
    hv = 0.5 * v
    return hv + hv * jnp.tanh(hv)


def _cast_rider(weights, col_tiles, steps, step_of):
    in_specs, out_specs, out_shapes = [], [], []
    for w, tile in zip(weights, col_tiles):
        rows, cols = w.shape
        assert rows % (steps * 16) == 0, (rows, steps)
        slab = rows // steps
        in_specs.append(pl.BlockSpec((slab, cols), lambda *g: (step_of(*g), 0)))
        if tile is None:
            out_specs.append(pl.BlockSpec((slab, cols), lambda *g: (step_of(*g), 0)))
            out_shapes.append(jax.ShapeDtypeStruct((rows, cols), BF16))
        else:
            out_specs.append(pl.BlockSpec((cols // tile, slab, tile),
                                          lambda *g: (0, step_of(*g), 0)))
            out_shapes.append(jax.ShapeDtypeStruct((cols // tile, rows, tile), BF16))
    return in_specs, out_specs, out_shapes


def _run_cast_rider(src_refs, dst_refs):
    for src, dst in zip(src_refs, dst_refs):
        if len(dst.shape) == 2:
            dst[...] = src[...].astype(BF16)
        else:
            tile = dst.shape[-1]
            for t in range(dst.shape[0]):
                dst[t] = src[:, t * tile:(t + 1) * tile].astype(BF16)


def _rms_mod(x, gain, shift):
    ms = jnp.mean(x * x, axis=-1, keepdims=True)
    return (x * lax.rsqrt(ms + EPS)) * gain + shift


def _split_bf16(v):
    hi = v.astype(BF16)
    lo = (v - hi.astype(F32)).astype(BF16)
    return hi, lo


def _mod_kernel(c_ref, w_ref, b_ref, o_ref):
    rows = c_ref.shape[0]
    s_hi, s_lo = _split_bf16(_silu(c_ref[...]))
    w_hi, w_lo = _split_bf16(w_ref[...])
    both = jnp.dot(jnp.concatenate([s_hi, s_lo], axis=0), w_hi, preferred_element_type=F32)
    cross = jnp.dot(s_hi, w_lo, preferred_element_type=F32)
    o_ref[...] = (both[0:rows] + both[rows:]) + cross + b_ref[...]


def _modulation(c_pad, w_ada, b_ada):
    rows = c_pad.shape[0]
    n = w_ada.shape[1]
    return pl.pallas_call(
        _mod_kernel,
        grid=(n // MOD_TN,),
        in_specs=[pl.BlockSpec((rows, D_MODEL), lambda j: (0, 0)),
                  pl.BlockSpec((D_MODEL, MOD_TN), lambda j: (0, j)),
                  pl.BlockSpec((1, MOD_TN), lambda j: (0, j))],
        out_specs=pl.BlockSpec((rows, MOD_TN), lambda j: (0, j)),
        out_shape=jax.ShapeDtypeStruct((rows, n), F32),
        compiler_params=_params(("arbitrary",), 2 * D_MODEL * MOD_TN * 4 + 8 * MIB),
        name="mod",
    )(c_pad, w_ada, b_ada)


def _inproj_kernel(x_ref, mod_ref, g_ref, w_ref, o_ref, h_ref):
    @pl.when(pl.program_id(1) == 0)
    def _():
        shift = mod_ref[0, SHIFT1:SHIFT1 + 1, :]
        gain = g_ref[...] * (1.0 + mod_ref[0, SCALE1:SCALE1 + 1, :])

        def body(r, carry):
            rows = pl.ds(pl.multiple_of(r * NORM_ROWS, NORM_ROWS), NORM_ROWS)
            h_ref[rows, :] = _rms_mod(x_ref[rows, :], gain, shift).astype(BF16)
            return carry

        lax.fori_loop(0, INPROJ_TM // NORM_ROWS, body, 0, unroll=NORM_UNROLL)

    o_ref[...] = jnp.dot(h_ref[...], w_ref[...].astype(BF16),
                         preferred_element_type=F32).astype(o_ref.dtype)


def _input_projection(x2d, mod3, norm_g, w_in):
    tm, tn = INPROJ_TM, INPROJ_TN
    vmem = (2 * tm * D_MODEL * 4 + 2 * D_MODEL * tn * 4 + 2 * tm * tn * 2
            + tm * D_MODEL * 2 + D_MODEL * tn * 2 + tm * tn * 4 + 4 * MIB)
    return pl.pallas_call(
        _inproj_kernel,
        grid=(TOKENS // tm, IN_WIDTH // tn),
        in_specs=[pl.BlockSpec((tm, D_MODEL), lambda i, j: (i, 0)),
                  pl.BlockSpec((1, N_MOD, D_MODEL), lambda i, j: (i // (SEQ // tm), 0, 0)),
                  pl.BlockSpec((1, D_MODEL), lambda i, j: (0, 0)),
                  pl.BlockSpec((D_MODEL, tn), lambda i, j: (0, j))],
        out_specs=pl.BlockSpec((tm, tn), lambda i, j: (i, j)),
        out_shape=jax.ShapeDtypeStruct((TOKENS, IN_WIDTH), BF16),
        scratch_shapes=[pltpu.VMEM((tm, D_MODEL), BF16)],
        compiler_params=_params(("arbitrary", "arbitrary"), vmem),
        name="inproj",
    )(x2d, mod3, norm_g, w_in)


def _relbias_kernel(rb_ref, o_ref):
    h = pl.program_id(0)
    shape = (2 * MOBA_BLOCK, MOBA_BLOCK)
    key = lax.broadcasted_iota(jnp.int32, shape, 0)
    qry = lax.broadcasted_iota(jnp.int32, shape, 1)
    dist = qry - key + MOBA_BLOCK
    n = jnp.maximum(dist, 0)
    max_exact = REL_BUCKETS // 2
    nf = jnp.maximum(n, 1).astype(F32)
    large = max_exact + (jnp.log(nf / max_exact) / math.log(REL_MAX_DIST / max_exact)
                         * (REL_BUCKETS - max_exact)).astype(jnp.int32)
    large = jnp.minimum(large, REL_BUCKETS - 1)
    bucket = jnp.where(n < max_exact, n, large)
    bias = jnp.zeros(shape, F32)
    for b in range(REL_BUCKETS):
        bias = jnp.where(bucket == b, rb_ref[b, h], bias)
    o_ref[0] = jnp.where(dist >= 0, bias * LOG2E, MASK_VALUE)


def _relbias_tiles(rel_bias):
    return pl.pallas_call(
        _relbias_kernel,
        grid=(ATTN_HEADS,),
        in_specs=[pl.BlockSpec(memory_space=pltpu.SMEM)],
        out_specs=pl.BlockSpec((1, 2 * MOBA_BLOCK, MOBA_BLOCK), lambda h: (h, 0, 0)),
        out_shape=jax.ShapeDtypeStruct((ATTN_HEADS, 2 * MOBA_BLOCK, MOBA_BLOCK), F32),
        compiler_params=_params(("arbitrary",), 16 * MIB),
        name="relbias",
    )(rel_bias)


def _far_bucket_is_last():
    d = np.arange(MOBA_BLOCK + 1, SEQ, dtype=np.float32)
    max_exact = REL_BUCKETS // 2
    large = max_exact + (np.log(d / max_exact) / math.log(REL_MAX_DIST / max_exact)
                         * (REL_BUCKETS - max_exact)).astype(np.int32)
    return bool(np.all(np.minimum(large, REL_BUCKETS - 1) == REL_BUCKETS - 1))


def _moba_kernel(n_cast, rb_ref, q_ref, k_ref, v_ref, gq_ref, gk_ref, bt_ref, *refs):
    cast_src, (o_ref, *cast_dst) = refs[:n_cast], refs[n_cast:2 * n_cast + 1]
    qb_ref, kb_ref, vt_ref, *s_refs = refs[2 * n_cast + 1:]
    _run_cast_rider(cast_src, cast_dst)

    h = pl.program_id(1)
    far_bias = rb_ref[REL_BUCKETS - 1, h] * LOG2E
    blk, hd = MOBA_BLOCK, ATTN_HEAD_DIM
    nt = (((1,), (1,)), ((), ()))

    q = q_ref[0].astype(F32)
    k = k_ref[0].astype(F32)
    qn = (q * lax.rsqrt(jnp.mean(q * q, axis=-1, keepdims=True) + EPS)) * gq_ref[...]
    kn = (k * lax.rsqrt(jnp.mean(k * k, axis=-1, keepdims=True) + EPS)) * gk_ref[...]
    qb_ref[...] = (qn * (hd ** -0.5 * LOG2E)).astype(BF16)
    kb_ref[...] = kn.astype(BF16)
    vt_ref[0:hd, :] = v_ref[0].astype(F32).T.astype(BF16)
    vt_ref[hd:, :] = jnp.ones((MOBA_ONES_ROWS, SEQ), BF16)

    k_mean = jnp.concatenate(
        [jnp.sum(kn[n * blk:(n + 1) * blk], axis=0, keepdims=True) for n in range(N_BLOCKS)],
        axis=0) * (1.0 / blk)
    gate = lax.dot_general(k_mean, qn, nt, preferred_element_type=F32,
                           precision=lax.Precision.HIGHEST)
    row = lax.broadcasted_iota(jnp.int32, (N_BLOCKS, blk), 0)

    def scores(qi, s_ref):
        cols = slice(qi * blk, (qi + 1) * blk)
        nk = (qi + 1) * blk

        mask_add = None
        if qi > MOBA_TOPK:
            g = gate[:, cols]
            rank = jnp.zeros((N_BLOCKS, blk), F32)
            for m in range(qi):
                gm = g[m:m + 1, :]
                beats = (gm > g) | ((gm == g) & (row > m))
                rank = rank + jnp.where(beats, 1.0, 0.0)
            mask_add = jnp.where(rank < MOBA_TOPK, 0.0, MASK_VALUE)

        s_all = lax.dot_general(kb_ref[0:nk, :], qb_ref[cols, :], nt,
                                preferred_element_type=F32)
        m8 = None
        for n in range(qi + 1):
            s_blk = s_all[n * blk:(n + 1) * blk]
            if n == qi:
                s_blk = s_blk + bt_ref[0, blk:2 * blk, :]
            else:
                if n == qi - 1:
                    s_blk = s_blk + bt_ref[0, 0:blk, :]
                    if mask_add is not None:
                        s_blk = s_blk + mask_add[n:n + 1, :]
                elif mask_add is not None:
                    s_blk = s_blk + (mask_add[n:n + 1, :] + far_bias)
                else:
                    s_blk = s_blk + far_bias
            s_ref[n * blk:(n + 1) * blk, :] = s_blk
            b8 = jnp.max(s_blk.reshape(blk // 8, 8, blk), axis=0)
            m8 = b8 if m8 is None else jnp.maximum(m8, b8)
        return jnp.max(m8, axis=0, keepdims=True)

    def attend(qi, s_ref, m):
        cols = slice(qi * blk, (qi + 1) * blk)
        nk = (qi + 1) * blk
        p = jnp.exp2(s_ref[0:nk, :] - m).astype(BF16)
        o_aug = jnp.dot(vt_ref[:, 0:nk], p, preferred_element_type=F32)
        o_t = o_aug[0:hd] * (1.0 / o_aug[hd:hd + 1])
        o_ref[0, cols, :] = o_t.T.astype(o_ref.dtype)

    for group in MOBA_TILE_GROUPS:
        maxes = [scores(qi, s_ref) for qi, s_ref in zip(group, s_refs)]
        for qi, s_ref, m in zip(group, s_refs, maxes):
            attend(qi, s_ref, m)


def _moba(proj3, rel_bias, q_norm_g, k_norm_g, bias_tiles, cast_weights, cast_tiles):
    hd = ATTN_HEAD_DIM
    head = lambda off: (lambda b, h: (b, 0, off // hd + h))
    cast_in, cast_out, cast_shapes = _cast_rider(
        cast_weights, cast_tiles, BATCH * ATTN_HEADS, lambda b, h: b * ATTN_HEADS + h)
    cast_bytes = sum(2 * (4 + 2) * w.size // (BATCH * ATTN_HEADS) for w in cast_weights)
    outs = pl.pallas_call(
        functools.partial(_moba_kernel, len(cast_weights)),
        grid=(BATCH, ATTN_HEADS),
        in_specs=[pl.BlockSpec(memory_space=pltpu.SMEM),
                  pl.BlockSpec((1, SEQ, hd), head(OFF_QA)),
                  pl.BlockSpec((1, SEQ, hd), head(OFF_KA)),
                  pl.BlockSpec((1, SEQ, hd), head(OFF_VA)),
                  pl.BlockSpec((1, hd), lambda b, h: (0, 0)),
                  pl.BlockSpec((1, hd), lambda b, h: (0, 0)),
                  pl.BlockSpec((1, 2 * MOBA_BLOCK, MOBA_BLOCK), lambda b, h: (h, 0, 0))]
                 + cast_in,
        out_specs=[pl.BlockSpec((1, SEQ, hd), lambda b, h: (b, 0, h))] + cast_out,
        out_shape=[jax.ShapeDtypeStruct((BATCH, SEQ, ATTN_WIDTH), BF16)] + cast_shapes,
        scratch_shapes=[pltpu.VMEM((SEQ, hd), BF16),
                        pltpu.VMEM((SEQ, hd), BF16),
                        pltpu.VMEM((hd + MOBA_ONES_ROWS, SEQ), BF16)]
                       + [pltpu.VMEM((SEQ, MOBA_BLOCK), F32)] * MOBA_GROUP_SIZE,
        compiler_params=_params(("arbitrary", "arbitrary"), 24 * MIB + cast_bytes),
        name="moba",
    )(rel_bias, proj3, proj3, proj3, q_norm_g, k_norm_g, bias_tiles, *cast_weights)
    return outs[0], outs[1:]


def _retention_kernel(n_cast, cd_ref, q_ref, k_ref, v_ref, gr_ref, cos_ref, sin_ref,
                      dmask_ref, qdec_ref, kdec_ref, gn_ref, *refs):
    cast_src, (o_ref, *cast_dst) = refs[:n_cast], refs[n_cast:2 * n_cast + 1]
    qf_ref, kf_ref, y_ref, kv_ref = refs[2 * n_cast + 1:]
    _run_cast_rider(cast_src, cast_dst)
    h = pl.program_id(1)
    chunk_decay = cd_ref[h]
    half = RET_KEY_DIM // 2
    cos = cos_ref[...]
    sin = sin_ref[...]
    q = q_ref[0].astype(F32)
    k = k_ref[0].astype(F32)
    qf_ref[...] = q * cos + pltpu.roll(q, half, 1) * sin
    kf_ref[...] = (k * cos + pltpu.roll(k, half, 1) * sin) * (RET_KEY_DIM ** -0.5)

    nt = (((1,), (1,)), ((), ()))
    chunks = [slice(c * RET_CHUNK, (c + 1) * RET_CHUNK) for c in range(N_CHUNKS)]

    scores = [(lax.dot_general(qf_ref[rows, :].astype(BF16), kf_ref[rows, :].astype(BF16), nt,
                               preferred_element_type=F32) * dmask_ref[0]).astype(BF16)
              for rows in chunks]
    for c, rows in enumerate(chunks):
        vc = v_ref[0, rows, :]
        y_ref[rows, :] = jnp.dot(scores[c], vc, preferred_element_type=F32)
        kd_t = (kf_ref[rows, :] * kdec_ref[0]).T.astype(BF16)
        kv_ref[c] = jnp.dot(kd_t, vc, preferred_element_type=F32)

    state = jnp.zeros((RET_KEY_DIM, RET_VAL_DIM), F32)
    for c, rows in enumerate(chunks):
        y_ref[rows, :] += jnp.dot((qf_ref[rows, :] * qdec_ref[0]).astype(BF16),
                                  state.astype(BF16), preferred_element_type=F32)
        state = chunk_decay * state + kv_ref[c]

    for rows in chunks:
        y = y_ref[rows, :]
        mu = jnp.mean(y, axis=-1, keepdims=True)
        yc = y - mu
        var = jnp.mean(yc * yc, axis=-1, keepdims=True)
        yn = (yc * lax.rsqrt(var + EPS)) * gn_ref[...]
        o_ref[0, rows, :] = (yn * _silu(gr_ref[0, rows, :]).astype(F32)).astype(o_ref.dtype)


def _retention_tables():
    half = RET_KEY_DIM // 2
    freqs = jnp.power(ROPE_BASE, -jnp.arange(half, dtype=F32) / half)
    ang = jnp.arange(SEQ).astype(F32)[:, None] * freqs[None, :]
    cos, sin = jnp.cos(ang), jnp.sin(ang)
    cos_full = jnp.concatenate([cos, cos], axis=-1)
    sin_signed = jnp.concatenate([-sin, sin], axis=-1)

    log_decay = jnp.log(1.0 - jnp.power(2.0, -5.0 - jnp.arange(RET_HEADS, dtype=F32)))
    i = jnp.arange(RET_CHUNK, dtype=F32)
    diff = i[:, None] - i[None, :]
    ld = log_decay[:, None, None]
    inner_decay = jnp.where(diff >= 0, jnp.exp(ld * jnp.maximum(diff, 0.0)), 0.0)
    q_decay = jnp.exp(log_decay[:, None] * (i + 1.0))
    k_decay = jnp.exp(log_decay[:, None] * (RET_CHUNK - 1.0 - i))
    chunk_decay = jnp.exp(log_decay * RET_CHUNK)
    bcast = lambda t: jnp.broadcast_to(t[:, :, None], (RET_HEADS, RET_CHUNK, RET_KEY_DIM))
    return cos_full, sin_signed, inner_decay, bcast(q_decay), bcast(k_decay), chunk_decay


def _retention(proj3, ret_norm_g, cast_weights, cast_tiles):
    dk, dv = RET_KEY_DIM, RET_VAL_DIM
    cos, sin, dmask, qdec, kdec, chunk_decay = _retention_tables()
    head = lambda off, w: (lambda b, h: (b, 0, off // w + h))
    per_head = lambda b, h: (h, 0, 0)
    cast_in, cast_out, cast_shapes = _cast_rider(
        cast_weights, cast_tiles, BATCH * RET_HEADS, lambda b, h: b * RET_HEADS + h)
    cast_bytes = sum(2 * (4 + 2) * w.size // (BATCH * RET_HEADS) for w in cast_weights)
    outs = pl.pallas_call(
        functools.partial(_retention_kernel, len(cast_weights)),
        grid=(BATCH, RET_HEADS),
        in_specs=[pl.BlockSpec(memory_space=pltpu.SMEM),
                  pl.BlockSpec((1, SEQ, dk), head(OFF_QR, dk)),
                  pl.BlockSpec((1, SEQ, dk), head(OFF_KR, dk)),
                  pl.BlockSpec((1, SEQ, dv), head(OFF_VR, dv)),
                  pl.BlockSpec((1, SEQ, dv), head(OFF_GR, dv)),
                  pl.BlockSpec((SEQ, dk), lambda b, h: (0, 0)),
                  pl.BlockSpec((SEQ, dk), lambda b, h: (0, 0)),
                  pl.BlockSpec((1, RET_CHUNK, RET_CHUNK), per_head),
                  pl.BlockSpec((1, RET_CHUNK, dk), per_head),
                  pl.BlockSpec((1, RET_CHUNK, dk), per_head),
                  pl.BlockSpec((1, dv), lambda b, h: (0, h))] + cast_in,
        out_specs=[pl.BlockSpec((1, SEQ, dv), lambda b, h: (b, 0, h))] + cast_out,
        out_shape=[jax.ShapeDtypeStruct((BATCH, SEQ, RET_V_WIDTH), BF16)] + cast_shapes,
        scratch_shapes=[pltpu.VMEM((SEQ, dk), F32), pltpu.VMEM((SEQ, dk), F32),
                        pltpu.VMEM((SEQ, dv), F32), pltpu.VMEM((N_CHUNKS, dk, dv), F32)],
        compiler_params=_params(("arbitrary", "arbitrary"), 24 * MIB + cast_bytes),
        name="retention",
    )(chunk_decay, proj3, proj3, proj3, proj3, cos, sin, dmask, qdec, kdec, ret_norm_g,
      *cast_weights)
    return outs[0], outs[1:]


def _mixer_kernel(ya_ref, yr_ref, ga_ref, gb_ref, wa_ref, wr_ref, wo_ref, x_ref, mod_ref,
                  o_ref):
    j = pl.program_id(1)

    @pl.when(j == 0)
    def _():
        o_ref[...] = jnp.zeros_like(o_ref)

    a = jnp.dot(ya_ref[...], wa_ref[j], preferred_element_type=F32)
    r = jnp.dot(yr_ref[...], wr_ref[j], preferred_element_type=F32)
    merged = (_sigmoid(ga_ref[...].astype(F32)) * a
              + _sigmoid(gb_ref[...].astype(F32)) * r)
    wo_rows = pl.ds(pl.multiple_of(j * MIX_TN, MIX_TN), MIX_TN)
    o_ref[...] += jnp.dot(merged.astype(BF16), wo_ref[wo_rows, :],
                          preferred_element_type=F32)

    @pl.when(j == pl.num_programs(1) - 1)
    def _():
        o_ref[...] = x_ref[...] + mod_ref[0, GATE1:GATE1 + 1, :] * o_ref[...]


def _mixer(ya2d, yr2d, proj2d, w_attn_br, w_ret_br, w_o, x2d, mod3):
    tm, tn = MIX_TM, MIX_TN
    nj = D_MODEL // tn
    assert w_attn_br.shape == (nj, ATTN_WIDTH, tn) and w_ret_br.shape == (nj, RET_V_WIDTH, tn)
    resident = lambda shape: pl.BlockSpec(shape, lambda i, j: (0,) * len(shape),
                                          pipeline_mode=pl.Buffered(1))
    vmem = (2 * tm * (ATTN_WIDTH + RET_V_WIDTH + 2 * tn) * 2
            + (ATTN_WIDTH + RET_V_WIDTH + D_MODEL) * D_MODEL * 2
            + 4 * tm * D_MODEL * 4 + tm * D_MODEL * 4 + 6 * tm * tn * 4 + 4 * MIB)
    return pl.pallas_call(
        _mixer_kernel,
        grid=(TOKENS // tm, D_MODEL // tn),
        in_specs=[pl.BlockSpec((tm, ATTN_WIDTH), lambda i, j: (i, 0)),
                  pl.BlockSpec((tm, RET_V_WIDTH), lambda i, j: (i, 0)),
                  pl.BlockSpec((tm, tn), lambda i, j: (i, OFF_GA // tn + j)),
                  pl.BlockSpec((tm, tn), lambda i, j: (i, OFF_GB // tn + j)),
                  resident((nj, ATTN_WIDTH, tn)),
                  resident((nj, RET_V_WIDTH, tn)),
                  resident((D_MODEL, D_MODEL)),
                  pl.BlockSpec((tm, D_MODEL), lambda i, j: (i, 0)),
                  pl.BlockSpec((1, N_MOD, D_MODEL), lambda i, j: (i // (SEQ // tm), 0, 0))],
        out_specs=pl.BlockSpec((tm, D_MODEL), lambda i, j: (i, 0)),
        out_shape=jax.ShapeDtypeStruct((TOKENS, D_MODEL), F32),
        compiler_params=_params(("arbitrary", "arbitrary"), vmem),
        name="mixer",
    )(ya2d, yr2d, proj2d, proj2d, w_attn_br, w_ret_br, w_o, x2d, mod3)


def _ffn_kernel(x_ref, halo_ref, mod_ref, g_ref, wv_ref, wg_ref, cwv_ref, cwg_ref,
                cbv_ref, cbg_ref, wd_ref, o_ref, h_ref, u_ref):
    i = pl.program_id(0)
    j = pl.program_id(1)
    tm, halo = FFN_TM, FFN_HALO

    @pl.when(j == 0)
    def _():
        o_ref[...] = jnp.zeros_like(o_ref)
        shift = mod_ref[0, SHIFT2:SHIFT2 + 1, :]
        gain = g_ref[...] * (1.0 + mod_ref[0, SCALE2:SCALE2 + 1, :])
        seq_start = (i % (SEQ // tm)) == 0
        h_halo = _rms_mod(halo_ref[...], gain, shift)
        h_ref[0:halo, :] = jnp.where(seq_start, 0.0, h_halo).astype(BF16)

        def body(r, carry):
            src = pl.ds(pl.multiple_of(r * NORM_ROWS, NORM_ROWS), NORM_ROWS)
            dst = pl.ds(pl.multiple_of(halo + r * NORM_ROWS, halo), NORM_ROWS)
            h_ref[dst, :] = _rms_mod(x_ref[src, :], gain, shift).astype(BF16)
            return carry

        lax.fori_loop(0, tm // NORM_ROWS, body, 0, unroll=NORM_UNROLL)

    def conv(half, w_ref, cw_ref, cb_ref):
        u_ref[half] = jnp.dot(h_ref[...], w_ref[...], preferred_element_type=F32)
        y = cb_ref[...]
        for t in range(CONV_WIDTH):
            lag = CONV_WIDTH - 1 - t
            y = y + cw_ref[0, t:t + 1, :] * u_ref[half, halo - lag:halo - lag + tm, :]
        return y

    val = conv(0, wv_ref, cwv_ref, cbv_ref)
    gt = conv(1, wg_ref, cwg_ref, cbg_ref)
    act = (_silu(gt) * val).astype(BF16)
    o_ref[...] += jnp.dot(act, wd_ref[...], preferred_element_type=F32)

    @pl.when(j == pl.num_programs(1) - 1)
    def _():
        o_ref[...] = x_ref[...] + mod_ref[0, GATE2:GATE2 + 1, :] * o_ref[...]


def _ffn(x1, mod3, norm_g, w_up, conv_w, conv_b, w_down):
    tm, tn, halo, nj = FFN_TM, FFN_TN, FFN_HALO, FFN_NJ
    vmem = (4 * tm * D_MODEL * 4 + 2 * halo * D_MODEL * 4
            + 2 * 3 * D_MODEL * tn * 2
            + (tm + halo) * D_MODEL * 2 + 2 * (tm + halo) * tn * 4
            + tm * tn * 4 + 4 * MIB)
    return pl.pallas_call(
        _ffn_kernel,
        grid=(TOKENS // tm, nj),
        in_specs=[pl.BlockSpec((tm, D_MODEL), lambda i, j: (i, 0)),
                  pl.BlockSpec((halo, D_MODEL),
                               lambda i, j: (jnp.maximum(i * (tm // halo) - 1, 0), 0)),
                  pl.BlockSpec((1, N_MOD, D_MODEL), lambda i, j: (i // (SEQ // tm), 0, 0)),
                  pl.BlockSpec((1, D_MODEL), lambda i, j: (0, 0)),
                  pl.BlockSpec((D_MODEL, tn), lambda i, j: (0, j)),
                  pl.BlockSpec((D_MODEL, tn), lambda i, j: (0, nj + j)),
                  pl.BlockSpec((1, CONV_WIDTH, tn), lambda i, j: (0, 0, j)),
                  pl.BlockSpec((1, CONV_WIDTH, tn), lambda i, j: (0, 0, nj + j)),
                  pl.BlockSpec((1, tn), lambda i, j: (0, j)),
                  pl.BlockSpec((1, tn), lambda i, j: (0, nj + j)),
                  pl.BlockSpec((tn, D_MODEL), lambda i, j: (j, 0))],
        out_specs=pl.BlockSpec((tm, D_MODEL), lambda i, j: (i, 0)),
        out_shape=jax.ShapeDtypeStruct((TOKENS, D_MODEL), F32),
        scratch_shapes=[pltpu.VMEM((tm + halo, D_MODEL), BF16),
                        pltpu.VMEM((2, tm + halo, tn), F32)],
        compiler_params=_params(("arbitrary", "arbitrary"), vmem, FFN_FLAGS),
        name="ffn",
    )(x1, x1, mod3, norm_g, w_up, w_up, conv_w, conv_w, conv_b, conv_b, w_down)


def kernel(x, c, w_ada, b_ada, norm1_g, w_in, q_norm_g, k_norm_g, rel_bias, ret_norm_g,
           w_attn_br, w_ret_br, w_o, norm2_g, w_up, conv_w, conv_b, w_down):
    assert x.shape == (BATCH, SEQ, D_MODEL) and w_ada.shape[0] == 1
    assert _far_bucket_is_last()
    layer = 0
    x2d = x.reshape(TOKENS, D_MODEL)

    c_pad = jnp.pad(c, ((0, 8 - BATCH), (0, 0)))
    mod = _modulation(c_pad, w_ada[layer], b_ada)[:BATCH]
    mod3 = mod.reshape(BATCH, N_MOD, D_MODEL)

    proj = _input_projection(x2d, mod3, norm1_g, w_in[layer])
    proj3 = proj.reshape(BATCH, SEQ, IN_WIDTH)

    bias_tiles = _relbias_tiles(rel_bias)
    yr, (w_attn_b, w_ret_b, w_o_b) = _retention(
        proj3, ret_norm_g, (w_attn_br[layer], w_ret_br[layer], w_o[layer]),
        (MIX_TN, MIX_TN, None))
    ya, (w_up_b, w_down_b) = _moba(proj3, rel_bias, q_norm_g, k_norm_g, bias_tiles,
                                   (w_up[layer], w_down[layer]), (None, None))

    x1 = _mixer(ya.reshape(TOKENS, ATTN_WIDTH), yr.reshape(TOKENS, RET_V_WIDTH), proj,
                w_attn_b, w_ret_b, w_o_b, x2d, mod3)

    out = _ffn(x1, mod3, norm2_g, w_up_b, conv_w, conv_b, w_down_b)
    return out.reshape(BATCH, SEQ, D_MODEL)
```

```python
import functools
import math

import numpy as np
import jax
import jax.numpy as jnp
from jax import lax
from jax.experimental import pallas as pl
from jax.experimental.pallas import tpu as pltpu

F32 = jnp.float32
BF16 = jnp.bfloat16

D_MODEL = 2048
BATCH = 4
SEQ = 2048
ATTN_HEADS = 8
ATTN_HEAD_DIM = 128
MOBA_BLOCK = 256
MOBA_TOPK = 3
REL_BUCKETS = 32
REL_MAX_DIST = 128
RET_HEADS = 8
RET_KEY_DIM = 128
RET_VAL_DIM = 256
RET_CHUNK = 128
ROPE_BASE = 10000.0
FFN_DIM = 5632
CONV_WIDTH = 3
EPS = 1e-6
N_MOD = 6

ATTN_WIDTH = ATTN_HEADS * ATTN_HEAD_DIM
RET_QK_WIDTH = RET_HEADS * RET_KEY_DIM
RET_V_WIDTH = RET_HEADS * RET_VAL_DIM
OFF_QA = 0
OFF_KA = OFF_QA + ATTN_WIDTH
OFF_VA = OFF_KA + ATTN_WIDTH
OFF_QR = OFF_VA + ATTN_WIDTH
OFF_KR = OFF_QR + RET_QK_WIDTH
OFF_VR = OFF_KR + RET_QK_WIDTH
OFF_GR = OFF_VR + RET_V_WIDTH
OFF_GA = OFF_GR + RET_V_WIDTH
OFF_GB = OFF_GA + D_MODEL
IN_WIDTH = OFF_GB + D_MODEL

TOKENS = BATCH * SEQ
N_BLOCKS = SEQ // MOBA_BLOCK
N_CHUNKS = SEQ // RET_CHUNK
MASK_VALUE = -1e30
LOG2E = math.log2(math.e)
MOBA_ONES_ROWS = 16
MOBA_TILE_GROUPS = ((0, 7, 3, 4, 1, 6, 2, 5),)
MOBA_GROUP_SIZE = max(len(g) for g in MOBA_TILE_GROUPS)
MIB = 1024 * 1024

SHIFT1, SCALE1, GATE1, SHIFT2, SCALE2, GATE2 = range(N_MOD)

MOD_TN = 1024
INPROJ_TM, INPROJ_TN = 1024, 1024
MIX_TM, MIX_TN = 512, 512
FFN_TM, FFN_TN = 1024, 512
FFN_HALO = 16
FFN_NJ = FFN_DIM // FFN_TN
FFN_DOWN_SPLITS = 2
NORM_ROWS = 16
NORM_UNROLL = 16


def _params(semantics, vmem_bytes, flags=None):
    return pltpu.CompilerParams(dimension_semantics=semantics,
                                vmem_limit_bytes=int(vmem_bytes), flags=flags)


def _sigmoid(v):
    return 0.5 + 0.5 * jnp.tanh(0.5 * v)


def _silu(v):
    hv = 0.5 * v
    return hv + hv * jnp.tanh(hv)


def _cast_rider(weights, col_tiles, steps, step_of):
    in_specs, out_specs, out_shapes = [], [], []
    for w, tile in zip(weights, col_tiles):
        rows, cols = w.shape
        assert rows % (steps * 16) == 0, (rows, steps)
        slab = rows // steps
        in_specs.append(pl.BlockSpec((slab, cols), lambda *g: (step_of(*g), 0)))
        if tile is None:
            out_specs.append(pl.BlockSpec((slab, cols), lambda *g: (step_of(*g), 0)))
            out_shapes.append(jax.ShapeDtypeStruct((rows, cols), BF16))
        else:
            out_specs.append(pl.BlockSpec((cols // tile, slab, tile),
                                          lambda *g: (0, step_of(*g), 0)))
            out_shapes.append(jax.ShapeDtypeStruct((cols // tile, rows, tile), BF16))
    return in_specs, out_specs, out_shapes


def _run_cast_rider(src_refs, dst_refs):
    for src, dst in zip(src_refs, dst_refs):
        if len(dst.shape) == 2:
            dst[...] = src[...].astype(BF16)
        else:
            tile = dst.shape[-1]
            for t in range(dst.shape[0]):
                dst[t] = src[:, t * tile:(t + 1) * tile].astype(BF16)


def _rms_mod(x, gain, shift):
    ms = jnp.mean(x * x, axis=-1, keepdims=True)
    return (x * lax.rsqrt(ms + EPS)) * gain + shift


def _split_bf16(v):
    hi = v.astype(BF16)
    lo = (v - hi.astype(F32)).astype(BF16)
    return hi, lo


def _mod_kernel(c_ref, w_ref, b_ref, o_ref):
    rows = c_ref.shape[0]
    s_hi, s_lo = _split_bf16(_silu(c_ref[...]))
    w_hi, w_lo = _split_bf16(w_ref[...])
    both = jnp.dot(jnp.concatenate([s_hi, s_lo], axis=0), w_hi, preferred_element_type=F32)
    cross = jnp.dot(s_hi, w_lo, preferred_element_type=F32)
    o_ref[...] = (both[0:rows] + both[rows:]) + cross + b_ref[...]


def _modulation(c_pad, w_ada, b_ada):
    rows = c_pad.shape[0]
    n = w_ada.shape[1]
    return pl.pallas_call(
        _mod_kernel,
        grid=(n // MOD_TN,),
        in_specs=[pl.BlockSpec((rows, D_MODEL), lambda j: (0, 0)),
                  pl.BlockSpec((D_MODEL, MOD_TN), lambda j: (0, j)),
                  pl.BlockSpec((1, MOD_TN), lambda j: (0, j))],
        out_specs=pl.BlockSpec((rows, MOD_TN), lambda j: (0, j)),
        out_shape=jax.ShapeDtypeStruct((rows, n), F32),
        compiler_params=_params(("arbitrary",), 2 * D_MODEL * MOD_TN * 4 + 8 * MIB),
        name="mod",
    )(c_pad, w_ada, b_ada)


def _inproj_kernel(x_ref, mod_ref, g_ref, w_ref, o_ref, h_ref):
    @pl.when(pl.program_id(1) == 0)
    def _():
        shift = mod_ref[0, SHIFT1:SHIFT1 + 1, :]
        gain = g_ref[...] * (1.0 + mod_ref[0, SCALE1:SCALE1 + 1, :])

        def body(r, carry):
            rows = pl.ds(pl.multiple_of(r * NORM_ROWS, NORM_ROWS), NORM_ROWS)
            h_ref[rows, :] = _rms_mod(x_ref[rows, :], gain, shift).astype(BF16)
            return carry

        lax.fori_loop(0, INPROJ_TM // NORM_ROWS, body, 0, unroll=NORM_UNROLL)

    o_ref[...] = jnp.dot(h_ref[...], w_ref[...].astype(BF16),
                         preferred_element_type=F32).astype(o_ref.dtype)


def _input_projection(x2d, mod3, norm_g, w_in):
    tm, tn = INPROJ_TM, INPROJ_TN
    vmem = (2 * tm * D_MODEL * 4 + 2 * D_MODEL * tn * 4 + 2 * tm * tn * 2
            + tm * D_MODEL * 2 + D_MODEL * tn * 2 + tm * tn * 4 + 4 * MIB)
    return pl.pallas_call(
        _inproj_kernel,
        grid=(TOKENS // tm, IN_WIDTH // tn),
        in_specs=[pl.BlockSpec((tm, D_MODEL), lambda i, j: (i, 0)),
                  pl.BlockSpec((1, N_MOD, D_MODEL), lambda i, j: (i // (SEQ // tm), 0, 0)),
                  pl.BlockSpec((1, D_MODEL), lambda i, j: (0, 0)),
                  pl.BlockSpec((D_MODEL, tn), lambda i, j: (0, j))],
        out_specs=pl.BlockSpec((tm, tn), lambda i, j: (i, j)),
        out_shape=jax.ShapeDtypeStruct((TOKENS, IN_WIDTH), BF16),
        scratch_shapes=[pltpu.VMEM((tm, D_MODEL), BF16)],
        compiler_params=_params(("arbitrary", "arbitrary"), vmem),
        name="inproj",
    )(x2d, mod3, norm_g, w_in)


def _relbias_kernel(rb_ref, o_ref):
    h = pl.program_id(0)
    shape = (2 * MOBA_BLOCK, MOBA_BLOCK)
    key = lax.broadcasted_iota(jnp.int32, shape, 0)
    qry = lax.broadcasted_iota(jnp.int32, shape, 1)
    dist = qry - key + MOBA_BLOCK
    n = jnp.maximum(dist, 0)
    max_exact = REL_BUCKETS // 2
    nf = jnp.maximum(n, 1).astype(F32)
    large = max_exact + (jnp.log(nf / max_exact) / math.log(REL_MAX_DIST / max_exact)
                         * (REL_BUCKETS - max_exact)).astype(jnp.int32)
    large = jnp.minimum(large, REL_BUCKETS - 1)
    bucket = jnp.where(n < max_exact, n, large)
    bias = jnp.zeros(shape, F32)
    for b in range(REL_BUCKETS):
        bias = jnp.where(bucket == b, rb_ref[b, h], bias)
    o_ref[0] = jnp.where(dist >= 0, bias * LOG2E, MASK_VALUE)


def _relbias_tiles(rel_bias):
    return pl.pallas_call(
        _relbias_kernel,
        grid=(ATTN_HEADS,),
        in_specs=[pl.BlockSpec(memory_space=pltpu.SMEM)],
        out_specs=pl.BlockSpec((1, 2 * MOBA_BLOCK, MOBA_BLOCK), lambda h: (h, 0, 0)),
        out_shape=jax.ShapeDtypeStruct((ATTN_HEADS, 2 * MOBA_BLOCK, MOBA_BLOCK), F32),
        compiler_params=_params(("arbitrary",), 16 * MIB),
        name="relbias",
    )(rel_bias)


def _far_bucket_is_last():
    d = np.arange(MOBA_BLOCK + 1, SEQ, dtype=np.float32)
    max_exact = REL_BUCKETS // 2
    large = max_exact + (np.log(d / max_exact) / math.log(REL_MAX_DIST / max_exact)
                         * (REL_BUCKETS - max_exact)).astype(np.int32)
    return bool(np.all(np.minimum(large, REL_BUCKETS - 1) == REL_BUCKETS - 1))


def _moba_kernel(n_cast, rb_ref, q_ref, k_ref, v_ref, gq_ref, gk_ref, bt_ref, *refs):
    cast_src, (o_ref, *cast_dst) = refs[:n_cast], refs[n_cast:2 * n_cast + 1]
    qb_ref, kb_ref, vt_ref, *s_refs = refs[2 * n_cast + 1:]
    _run_cast_rider(cast_src, cast_dst)

    h = pl.program_id(1)
    far_bias = rb_ref[REL_BUCKETS - 1, h] * LOG2E
    blk, hd = MOBA_BLOCK, ATTN_HEAD_DIM
    nt = (((1,), (1,)), ((), ()))

    q = q_ref[0].astype(F32)
    k = k_ref[0].astype(F32)
    qn = (q * lax.rsqrt(jnp.mean(q * q, axis=-1, keepdims=True) + EPS)) * gq_ref[...]
    kn = (k * lax.rsqrt(jnp.mean(k * k, axis=-1, keepdims=True) + EPS)) * gk_ref[...]
    qb_ref[...] = (qn * (hd ** -0.5 * LOG2E)).astype(BF16)
    kb_ref[...] = kn.astype(BF16)
    vt_ref[0:hd, :] = v_ref[0].astype(F32).T.astype(BF16)
    vt_ref[hd:, :] = jnp.ones((MOBA_ONES_ROWS, SEQ), BF16)

    k_mean = jnp.concatenate(
        [jnp.sum(kn[n * blk:(n + 1) * blk], axis=0, keepdims=True) for n in range(N_BLOCKS)],
        axis=0) * (1.0 / blk)
    gate = lax.dot_general(k_mean, qn, nt, preferred_element_type=F32,
                           precision=lax.Precision.HIGHEST)
    row = lax.broadcasted_iota(jnp.int32, (N_BLOCKS, blk), 0)

    def scores(qi, s_ref):
        cols = slice(qi * blk, (qi + 1) * blk)
        nk = (qi + 1) * blk

        mask_add = None
        if qi > MOBA_TOPK:
            g = gate[:, cols]
            rank = jnp.zeros((N_BLOCKS, blk), F32)
            for m in range(qi):
                gm = g[m:m + 1, :]
                beats = (gm > g) | ((gm == g) & (row > m))
                rank = rank + jnp.where(beats, 1.0, 0.0)
            mask_add = jnp.where(rank < MOBA_TOPK, 0.0, MASK_VALUE)

        s_all = lax.dot_general(kb_ref[0:nk, :], qb_ref[cols, :], nt,
                                preferred_element_type=F32)
        m8 = None
        for n in range(qi + 1):
            s_blk = s_all[n * blk:(n + 1) * blk]
            if n == qi:
                s_blk = s_blk + bt_ref[0, blk:2 * blk, :]
            else:
                if n == qi - 1:
                    s_blk = s_blk + bt_ref[0, 0:blk, :]
                    if mask_add is not None:
                        s_blk = s_blk + mask_add[n:n + 1, :]
                elif mask_add is not None:
                    s_blk = s_blk + (mask_add[n:n + 1, :] + far_bias)
                else:
                    s_blk = s_blk + far_bias
            s_ref[n * blk:(n + 1) * blk, :] = s_blk
            b8 = jnp.max(s_blk.reshape(blk // 8, 8, blk), axis=0)
            m8 = b8 if m8 is None else jnp.maximum(m8, b8)
        return jnp.max(m8, axis=0, keepdims=True)

    def attend(qi, s_ref, m):
        cols = slice(qi * blk, (qi + 1) * blk)
        nk = (qi + 1) * blk
        p = jnp.exp2(s_ref[0:nk, :] - m).astype(BF16)
        o_aug = jnp.dot(vt_ref[:, 0:nk], p, preferred_element_type=F32)
        o_t = o_aug[0:hd] * (1.0 / o_aug[hd:hd + 1])
        o_ref[0, cols, :] = o_t.T.astype(o_ref.dtype)

    for group in MOBA_TILE_GROUPS:
        maxes = [scores(qi, s_ref) for qi, s_ref in zip(group, s_refs)]
        for qi, s_ref, m in zip(group, s_refs, maxes):
            attend(qi, s_ref, m)


def _moba(proj3, rel_bias, q_norm_g, k_norm_g, bias_tiles, cast_weights, cast_tiles):
    hd = ATTN_HEAD_DIM
    head = lambda off: (lambda b, h: (b, 0, off // hd + h))
    cast_in, cast_out, cast_shapes = _cast_rider(
        cast_weights, cast_tiles, BATCH * ATTN_HEADS, lambda b, h: b * ATTN_HEADS + h)
    cast_bytes = sum(2 * (4 + 2) * w.size // (BATCH * ATTN_HEADS) for w in cast_weights)
    outs = pl.pallas_call(
        functools.partial(_moba_kernel, len(cast_weights)),
        grid=(BATCH, ATTN_HEADS),
        in_specs=[pl.BlockSpec(memory_space=pltpu.SMEM),
                  pl.BlockSpec((1, SEQ, hd), head(OFF_QA)),
                  pl.BlockSpec((1, SEQ, hd), head(OFF_KA)),
                  pl.BlockSpec((1, SEQ, hd), head(OFF_VA)),
                  pl.BlockSpec((1, hd), lambda b, h: (0, 0)),
                  pl.BlockSpec((1, hd), lambda b, h: (0, 0)),
                  pl.BlockSpec((1, 2 * MOBA_BLOCK, MOBA_BLOCK), lambda b, h: (h, 0, 0))]
                 + cast_in,
        out_specs=[pl.BlockSpec((1, SEQ, hd), lambda b, h: (b, 0, h))] + cast_out,
        out_shape=[jax.ShapeDtypeStruct((BATCH, SEQ, ATTN_WIDTH), BF16)] + cast_shapes,
        scratch_shapes=[pltpu.VMEM((SEQ, hd), BF16),
                        pltpu.VMEM((SEQ, hd), BF16),
                        pltpu.VMEM((hd + MOBA_ONES_ROWS, SEQ), BF16)]
                       + [pltpu.VMEM((SEQ, MOBA_BLOCK), F32)] * MOBA_GROUP_SIZE,
        compiler_params=_params(("arbitrary", "arbitrary"), 24 * MIB + cast_bytes),
        name="moba",
    )(rel_bias, proj3, proj3, proj3, q_norm_g, k_norm_g, bias_tiles, *cast_weights)
    return outs[0], outs[1:]


def _retention_kernel(n_cast, cd_ref, q_ref, k_ref, v_ref, gr_ref, cos_ref, sin_ref,
                      dmask_ref, qdec_ref, kdec_ref, gn_ref, *refs):
    cast_src, (o_ref, *cast_dst) = refs[:n_cast], refs[n_cast:2 * n_cast + 1]
    qf_ref, kf_ref, y_ref, kv_ref = refs[2 * n_cast + 1:]
    _run_cast_rider(cast_src, cast_dst)
    h = pl.program_id(1)
    chunk_decay = cd_ref[h]
    half = RET_KEY_DIM // 2
    cos = cos_ref[...]
    sin = sin_ref[...]
    q = q_ref[0].astype(F32)
    k = k_ref[0].astype(F32)
    qf_ref[...] = q * cos + pltpu.roll(q, half, 1) * sin
    kf_ref[...] = (k * cos + pltpu.roll(k, half, 1) * sin) * (RET_KEY_DIM ** -0.5)

    nt = (((1,), (1,)), ((), ()))
    chunks = [slice(c * RET_CHUNK, (c + 1) * RET_CHUNK) for c in range(N_CHUNKS)]

    scores = [(lax.dot_general(qf_ref[rows, :].astype(BF16), kf_ref[rows, :].astype(BF16), nt,
                               preferred_element_type=F32) * dmask_ref[0]).astype(BF16)
              for rows in chunks]
    for c, rows in enumerate(chunks):
        vc = v_ref[0, rows, :]
        y_ref[rows, :] = jnp.dot(scores[c], vc, preferred_element_type=F32)
        kd_t = (kf_ref[rows, :] * kdec_ref[0]).T.astype(BF16)
        kv_ref[c] = jnp.dot(kd_t, vc, preferred_element_type=F32)

    state = jnp.zeros((RET_KEY_DIM, RET_VAL_DIM), F32)
    for c, rows in enumerate(chunks):
        y_ref[rows, :] += jnp.dot((qf_ref[rows, :] * qdec_ref[0]).astype(BF16),
                                  state.astype(BF16), preferred_element_type=F32)
        state = chunk_decay * state + kv_ref[c]

    for rows in chunks:
        y = y_ref[rows, :]
        mu = jnp.mean(y, axis=-1, keepdims=True)
        yc = y - mu
        var = jnp.mean(yc * yc, axis=-1, keepdims=True)
        yn = (yc * lax.rsqrt(var + EPS)) * gn_ref[...]
        o_ref[0, rows, :] = (yn * _silu(gr_ref[0, rows, :]).astype(F32)).astype(o_ref.dtype)


def _retention_tables():
    half = RET_KEY_DIM // 2
    freqs = jnp.power(ROPE_BASE, -jnp.arange(half, dtype=F32) / half)
    ang = jnp.arange(SEQ).astype(F32)[:, None] * freqs[None, :]
    cos, sin = jnp.cos(ang), jnp.sin(ang)
    cos_full = jnp.concatenate([cos, cos], axis=-1)
    sin_signed = jnp.concatenate([-sin, sin], axis=-1)

    log_decay = jnp.log(1.0 - jnp.power(2.0, -5.0 - jnp.arange(RET_HEADS, dtype=F32)))
    i = jnp.arange(RET_CHUNK, dtype=F32)
    diff = i[:, None] - i[None, :]
    ld = log_decay[:, None, None]
    inner_decay = jnp.where(diff >= 0, jnp.exp(ld * jnp.maximum(diff, 0.0)), 0.0)
    q_decay = jnp.exp(log_decay[:, None] * (i + 1.0))
    k_decay = jnp.exp(log_decay[:, None] * (RET_CHUNK - 1.0 - i))
    chunk_decay = jnp.exp(log_decay * RET_CHUNK)
    bcast = lambda t: jnp.broadcast_to(t[:, :, None], (RET_HEADS, RET_CHUNK, RET_KEY_DIM))
    return cos_full, sin_signed, inner_decay, bcast(q_decay), bcast(k_decay), chunk_decay


def _retention(proj3, ret_norm_g, cast_weights, cast_tiles):
    dk, dv = RET_KEY_DIM, RET_VAL_DIM
    cos, sin, dmask, qdec, kdec, chunk_decay = _retention_tables()
    head = lambda off, w: (lambda b, h: (b, 0, off // w + h))
    per_head = lambda b, h: (h, 0, 0)
    cast_in, cast_out, cast_shapes = _cast_rider(
        cast_weights, cast_tiles, BATCH * RET_HEADS, lambda b, h: b * RET_HEADS + h)
    cast_bytes = sum(2 * (4 + 2) * w.size // (BATCH * RET_HEADS) for w in cast_weights)
    outs = pl.pallas_call(
        functools.partial(_retention_kernel, len(cast_weights)),
        grid=(BATCH, RET_HEADS),
        in_specs=[pl.BlockSpec(memory_space=pltpu.SMEM),
                  pl.BlockSpec((1, SEQ, dk), head(OFF_QR, dk)),
                  pl.BlockSpec((1, SEQ, dk), head(OFF_KR, dk)),
                  pl.BlockSpec((1, SEQ, dv), head(OFF_VR, dv)),
                  pl.BlockSpec((1, SEQ, dv), head(OFF_GR, dv)),
                  pl.BlockSpec((SEQ, dk), lambda b, h: (0, 0)),
                  pl.BlockSpec((SEQ, dk), lambda b, h: (0, 0)),
                  pl.BlockSpec((1, RET_CHUNK, RET_CHUNK), per_head),
                  pl.BlockSpec((1, RET_CHUNK, dk), per_head),
                  pl.BlockSpec((1, RET_CHUNK, dk), per_head),
                  pl.BlockSpec((1, dv), lambda b, h: (0, h))] + cast_in,
        out_specs=[pl.BlockSpec((1, SEQ, dv), lambda b, h: (b, 0, h))] + cast_out,
        out_shape=[jax.ShapeDtypeStruct((BATCH, SEQ, RET_V_WIDTH), BF16)] + cast_shapes,
        scratch_shapes=[pltpu.VMEM((SEQ, dk), F32), pltpu.VMEM((SEQ, dk), F32),
                        pltpu.VMEM((SEQ, dv), F32), pltpu.VMEM((N_CHUNKS, dk, dv), F32)],
        compiler_params=_params(("arbitrary", "arbitrary"), 24 * MIB + cast_bytes),
        name="retention",
    )(chunk_decay, proj3, proj3, proj3, proj3, cos, sin, dmask, qdec, kdec, ret_norm_g,
      *cast_weights)
    return outs[0], outs[1:]


def _mixer_kernel(ya_ref, yr_ref, ga_ref, gb_ref, wa_ref, wr_ref, wo_ref, x_ref, mod_ref,
                  o_ref):
    j = pl.program_id(1)

    @pl.when(j == 0)
    def _():
        o_ref[...] = x_ref[...]

    a = jnp.dot(ya_ref[...], wa_ref[j], preferred_element_type=F32)
    r = jnp.dot(yr_ref[...], wr_ref[j], preferred_element_type=F32)
    merged = (_sigmoid(ga_ref[...].astype(F32)) * a
              + _sigmoid(gb_ref[...].astype(F32)) * r)
    wo_rows = pl.ds(pl.multiple_of(j * MIX_TN, MIX_TN), MIX_TN)
    o_ref[...] += mod_ref[0, GATE1:GATE1 + 1, :] * jnp.dot(
        merged.astype(BF16), wo_ref[wo_rows, :], preferred_element_type=F32)


def _mixer(ya2d, yr2d, proj2d, w_attn_br, w_ret_br, w_o, x2d, mod3):
    tm, tn = MIX_TM, MIX_TN
    nj = D_MODEL // tn
    assert w_attn_br.shape == (nj, ATTN_WIDTH, tn) and w_ret_br.shape == (nj, RET_V_WIDTH, tn)
    resident = lambda shape: pl.BlockSpec(shape, lambda i, j: (0,) * len(shape),
                                          pipeline_mode=pl.Buffered(1))
    vmem = (2 * tm * (ATTN_WIDTH + RET_V_WIDTH + 2 * tn) * 2
            + (ATTN_WIDTH + RET_V_WIDTH + D_MODEL) * D_MODEL * 2
            + 4 * tm * D_MODEL * 4 + tm * D_MODEL * 4 + 6 * tm * tn * 4 + 4 * MIB)
    return pl.pallas_call(
        _mixer_kernel,
        grid=(TOKENS // tm, D_MODEL // tn),
        in_specs=[pl.BlockSpec((tm, ATTN_WIDTH), lambda i, j: (i, 0)),
                  pl.BlockSpec((tm, RET_V_WIDTH), lambda i, j: (i, 0)),
                  pl.BlockSpec((tm, tn), lambda i, j: (i, OFF_GA // tn + j)),
                  pl.BlockSpec((tm, tn), lambda i, j: (i, OFF_GB // tn + j)),
                  resident((nj, ATTN_WIDTH, tn)),
                  resident((nj, RET_V_WIDTH, tn)),
                  resident((D_MODEL, D_MODEL)),
                  pl.BlockSpec((tm, D_MODEL), lambda i, j: (i, 0)),
                  pl.BlockSpec((1, N_MOD, D_MODEL), lambda i, j: (i // (SEQ // tm), 0, 0))],
        out_specs=pl.BlockSpec((tm, D_MODEL), lambda i, j: (i, 0)),
        out_shape=jax.ShapeDtypeStruct((TOKENS, D_MODEL), F32),
        compiler_params=_params(("arbitrary", "arbitrary"), vmem),
        name="mixer",
    )(ya2d, yr2d, proj2d, proj2d, w_attn_br, w_ret_br, w_o, x2d, mod3)


def _ffn_kernel(x_ref, halo_ref, mod_ref, g_ref, wv_ref, wg_ref, cwv_ref, cwg_ref,
                cbv_ref, cbg_ref, wd_ref, o_ref, h_ref, u_ref):
    i = pl.program_id(0)
    j = pl.program_id(1)
    tm, halo = FFN_TM, FFN_HALO

    @pl.when(j == 0)
    def _():
        o_ref[...] = jnp.zeros_like(o_ref)
        shift = mod_ref[0, SHIFT2:SHIFT2 + 1, :]
        gain = g_ref[...] * (1.0 + mod_ref[0, SCALE2:SCALE2 + 1, :])
        seq_start = (i % (SEQ // tm)) == 0
        h_halo = _rms_mod(halo_ref[...], gain, shift)
        h_ref[0:halo, :] = jnp.where(seq_start, 0.0, h_halo).astype(BF16)

        def body(r, carry):
            src = pl.ds(pl.multiple_of(r * NORM_ROWS, NORM_ROWS), NORM_ROWS)
            dst = pl.ds(pl.multiple_of(halo + r * NORM_ROWS, halo), NORM_ROWS)
            h_ref[dst, :] = _rms_mod(x_ref[src, :], gain, shift).astype(BF16)
            return carry

        lax.fori_loop(0, tm // NORM_ROWS, body, 0, unroll=NORM_UNROLL)

    def conv(half, w_ref, cw_ref, cb_ref):
        u_ref[half] = jnp.dot(h_ref[...], w_ref[...], preferred_element_type=F32)
        y = cb_ref[...]
        for t in range(CONV_WIDTH):
            lag = CONV_WIDTH - 1 - t
            y = y + cw_ref[0, t:t + 1, :] * u_ref[half, halo - lag:halo - lag + tm, :]
        return y

    val = conv(0, wv_ref, cwv_ref, cbv_ref)
    gt = conv(1, wg_ref, cwg_ref, cbg_ref)
    kw = FFN_TN // FFN_DOWN_SPLITS
    for kh in range(FFN_DOWN_SPLITS):
        cols = slice(kh * kw, (kh + 1) * kw)
        act = (_silu(gt[:, cols]) * val[:, cols]).astype(BF16)
        o_ref[...] += jnp.dot(act, wd_ref[cols, :], preferred_element_type=F32)

    @pl.when(j == pl.num_programs(1) - 1)
    def _():
        o_ref[...] = x_ref[...] + mod_ref[0, GATE2:GATE2 + 1, :] * o_ref[...]


def _ffn(x1, mod3, norm_g, w_up, conv_w, conv_b, w_down):
    tm, tn, halo, nj = FFN_TM, FFN_TN, FFN_HALO, FFN_NJ
    vmem = (4 * tm * D_MODEL * 4 + 2 * halo * D_MODEL * 4
            + 2 * 3 * D_MODEL * tn * 2
            + (tm + halo) * D_MODEL * 2 + 2 * (tm + halo) * tn * 4
            + tm * tn * 4 + 4 * MIB)
    return pl.pallas_call(
        _ffn_kernel,
        grid=(TOKENS // tm, nj),
        in_specs=[pl.BlockSpec((tm, D_MODEL), lambda i, j: (i, 0)),
                  pl.BlockSpec((halo, D_MODEL),
                               lambda i, j: (jnp.maximum(i * (tm // halo) - 1, 0), 0)),
                  pl.BlockSpec((1, N_MOD, D_MODEL), lambda i, j: (i // (SEQ // tm), 0, 0)),
                  pl.BlockSpec((1, D_MODEL), lambda i, j: (0, 0)),
                  pl.BlockSpec((D_MODEL, tn), lambda i, j: (0, j)),
                  pl.BlockSpec((D_MODEL, tn), lambda i, j: (0, nj + j)),
                  pl.BlockSpec((1, CONV_WIDTH, tn), lambda i, j: (0, 0, j)),
                  pl.BlockSpec((1, CONV_WIDTH, tn), lambda i, j: (0, 0, nj + j)),
                  pl.BlockSpec((1, tn), lambda i, j: (0, j)),
                  pl.BlockSpec((1, tn), lambda i, j: (0, nj + j)),
                  pl.BlockSpec((tn, D_MODEL), lambda i, j: (j, 0))],
        out_specs=pl.BlockSpec((tm, D_MODEL), lambda i, j: (i, 0)),
        out_shape=jax.ShapeDtypeStruct((TOKENS, D_MODEL), F32),
        scratch_shapes=[pltpu.VMEM((tm + halo, D_MODEL), BF16),
                        pltpu.VMEM((2, tm + halo, tn), F32)],
        compiler_params=_params(("arbitrary", "arbitrary"), vmem),
        name="ffn",
    )(x1, x1, mod3, norm_g, w_up, w_up, conv_w, conv_w, conv_b, conv_b, w_down)


def kernel(x, c, w_ada, b_ada, norm1_g, w_in, q_norm_g, k_norm_g, rel_bias, ret_norm_g,
           w_attn_br, w_ret_br, w_o, norm2_g, w_up, conv_w, conv_b, w_down):
    assert x.shape == (BATCH, SEQ, D_MODEL) and w_ada.shape[0] == 1
    assert _far_bucket_is_last()
    layer = 0
    x2d = x.reshape(TOKENS, D_MODEL)

    c_pad = jnp.pad(c, ((0, 8 - BATCH), (0, 0)))
    mod = _modulation(c_pad, w_ada[layer], b_ada)[:BATCH]
    mod3 = mod.reshape(BATCH, N_MOD, D_MODEL)

    proj = _input_projection(x2d, mod3, norm1_g, w_in[layer])
    proj3 = proj.reshape(BATCH, SEQ, IN_WIDTH)

    bias_tiles = _relbias_tiles(rel_bias)
    yr, (w_attn_b, w_ret_b, w_o_b) = _retention(
        proj3, ret_norm_g, (w_attn_br[layer], w_ret_br[layer], w_o[layer]),
        (MIX_TN, MIX_TN, None))
    ya, (w_up_b, w_down_b) = _moba(proj3, rel_bias, q_norm_g, k_norm_g, bias_tiles,
                                   (w_up[layer], w_down[layer]), (None, None))

    x1 = _mixer(ya.reshape(TOKENS, ATTN_WIDTH), yr.reshape(TOKENS, RET_V_WIDTH), proj,
                w_attn_b, w_ret_b, w_o_b, x2d, mod3)

    out = _ffn(x1, mod3, norm2_g, w_up_b, conv_w, conv_b, w_down_b)
    return out.reshape(BATCH, SEQ, D_MODEL)
```

```python
import functools
import math

import numpy as np
import jax
import jax.numpy as jnp
from jax import lax
from jax.experimental import pallas as pl
from jax.experimental.pallas import tpu as pltpu

F32 = jnp.float32
BF16 = jnp.bfloat16

D_MODEL = 2048
BATCH = 4
SEQ = 2048
ATTN_HEADS = 8
ATTN_HEAD_DIM = 128
MOBA_BLOCK = 256
MOBA_TOPK = 3
REL_BUCKETS = 32
REL_MAX_DIST = 128
RET_HEADS = 8
RET_KEY_DIM = 128
RET_VAL_DIM = 256
RET_CHUNK = 128
ROPE_BASE = 10000.0
FFN_DIM = 5632
CONV_WIDTH = 3
EPS = 1e-6
N_MOD = 6

ATTN_WIDTH = ATTN_HEADS * ATTN_HEAD_DIM
RET_QK_WIDTH = RET_HEADS * RET_KEY_DIM
RET_V_WIDTH = RET_HEADS * RET_VAL_DIM
OFF_QA = 0
OFF_KA = OFF_QA + ATTN_WIDTH
OFF_VA = OFF_KA + ATTN_WIDTH
OFF_QR = OFF_VA + ATTN_WIDTH
OFF_KR = OFF_QR + RET_QK_WIDTH
OFF_VR = OFF_KR + RET_QK_WIDTH
OFF_GR = OFF_VR + RET_V_WIDTH
OFF_GA = OFF_GR + RET_V_WIDTH
OFF_GB = OFF_GA + D_MODEL
IN_WIDTH = OFF_GB + D_MODEL

TOKENS = BATCH * SEQ
N_BLOCKS = SEQ // MOBA_BLOCK
N_CHUNKS = SEQ // RET_CHUNK
MASK_VALUE = -1e30
LOG2E = math.log2(math.e)
MOBA_ONES_ROWS = 16
MOBA_TILE_ORDER = (0, 7, 3, 4, 1, 6, 2, 5)
MIB = 1024 * 1024

SHIFT1, SCALE1, GATE1, SHIFT2, SCALE2, GATE2 = range(N_MOD)

MOD_TN = 1024
INPROJ_TM, INPROJ_TN = 1024, 1024
MIX_TM, MIX_TN = 512, 1024
FFN_TM, FFN_TN = 1024, 512
FFN_HALO = 16
FFN_NJ = FFN_DIM // FFN_TN
FFN_DOWN_SPLITS = 2
NORM_ROWS = 16
NORM_UNROLL = 16


def _params(semantics, vmem_bytes):
    return pltpu.CompilerParams(dimension_semantics=semantics,
                                vmem_limit_bytes=int(vmem_bytes))


def _sigmoid(v):
    return 0.5 + 0.5 * jnp.tanh(0.5 * v)


def _silu(v):
    hv = 0.5 * v
    return hv + hv * jnp.tanh(hv)


def _cast_rider(weights, col_tiles, steps, step_of):
    in_specs, out_specs, out_shapes = [], [], []
    for w, tile in zip(weights, col_tiles):
        rows, cols = w.shape
        assert rows % (steps * 16) == 0, (rows, steps)
        slab = rows // steps
        in_specs.append(pl.BlockSpec((slab, cols), lambda *g: (step_of(*g), 0)))
        if tile is None:
            out_specs.append(pl.BlockSpec((slab, cols), lambda *g: (step_of(*g), 0)))
            out_shapes.append(jax.ShapeDtypeStruct((rows, cols), BF16))
        else:
            out_specs.append(pl.BlockSpec((cols // tile, slab, tile),
                                          lambda *g: (0, step_of(*g), 0)))
            out_shapes.append(jax.ShapeDtypeStruct((cols // tile, rows, tile), BF16))
    return in_specs, out_specs, out_shapes


def _run_cast_rider(src_refs, dst_refs):
    for src, dst in zip(src_refs, dst_refs):
        if len(dst.shape) == 2:
            dst[...] = src[...].astype(BF16)
        else:
            tile = dst.shape[-1]
            for t in range(dst.shape[0]):
                dst[t] = src[:, t * tile:(t + 1) * tile].astype(BF16)


def _rms_mod(x, gain, shift):
    ms = jnp.mean(x * x, axis=-1, keepdims=True)
    return (x * lax.rsqrt(ms + EPS)) * gain + shift


def _split_bf16(v):
    hi = v.astype(BF16)
    lo = (v - hi.astype(F32)).astype(BF16)
    return hi, lo


def _mod_kernel(c_ref, w_ref, b_ref, o_ref):
    rows = c_ref.shape[0]
    s_hi, s_lo = _split_bf16(_silu(c_ref[...]))
    w_hi, w_lo = _split_bf16(w_ref[...])
    both = jnp.dot(jnp.concatenate([s_hi, s_lo], axis=0), w_hi, preferred_element_type=F32)
    cross = jnp.dot(s_hi, w_lo, preferred_element_type=F32)
    o_ref[...] = (both[0:rows] + both[rows:]) + cross + b_ref[...]


def _modulation(c_pad, w_ada, b_ada):
    rows = c_pad.shape[0]
    n = w_ada.shape[1]
    return pl.pallas_call(
        _mod_kernel,
        grid=(n // MOD_TN,),
        in_specs=[pl.BlockSpec((rows, D_MODEL), lambda j: (0, 0)),
                  pl.BlockSpec((D_MODEL, MOD_TN), lambda j: (0, j)),
                  pl.BlockSpec((1, MOD_TN), lambda j: (0, j))],
        out_specs=pl.BlockSpec((rows, MOD_TN), lambda j: (0, j)),
        out_shape=jax.ShapeDtypeStruct((rows, n), F32),
        compiler_params=_params(("arbitrary",), 2 * D_MODEL * MOD_TN * 4 + 8 * MIB),
        name="mod",
    )(c_pad, w_ada, b_ada)


def _inproj_kernel(x_ref, mod_ref, g_ref, w_ref, o_ref, h_ref):
    @pl.when(pl.program_id(1) == 0)
    def _():
        shift = mod_ref[0, SHIFT1:SHIFT1 + 1, :]
        gain = g_ref[...] * (1.0 + mod_ref[0, SCALE1:SCALE1 + 1, :])

        def body(r, carry):
            rows = pl.ds(pl.multiple_of(r * NORM_ROWS, NORM_ROWS), NORM_ROWS)
            h_ref[rows, :] = _rms_mod(x_ref[rows, :], gain, shift).astype(BF16)
            return carry

        lax.fori_loop(0, INPROJ_TM // NORM_ROWS, body, 0, unroll=NORM_UNROLL)

    o_ref[...] = jnp.dot(h_ref[...], w_ref[...].astype(BF16),
                         preferred_element_type=F32).astype(o_ref.dtype)


def _input_projection(x2d, mod3, norm_g, w_in):
    tm, tn = INPROJ_TM, INPROJ_TN
    vmem = (2 * tm * D_MODEL * 4 + 2 * D_MODEL * tn * 4 + 2 * tm * tn * 2
            + tm * D_MODEL * 2 + D_MODEL * tn * 2 + tm * tn * 4 + 4 * MIB)
    return pl.pallas_call(
        _inproj_kernel,
        grid=(TOKENS // tm, IN_WIDTH // tn),
        in_specs=[pl.BlockSpec((tm, D_MODEL), lambda i, j: (i, 0)),
                  pl.BlockSpec((1, N_MOD, D_MODEL), lambda i, j: (i // (SEQ // tm), 0, 0)),
                  pl.BlockSpec((1, D_MODEL), lambda i, j: (0, 0)),
                  pl.BlockSpec((D_MODEL, tn), lambda i, j: (0, j))],
        out_specs=pl.BlockSpec((tm, tn), lambda i, j: (i, j)),
        out_shape=jax.ShapeDtypeStruct((TOKENS, IN_WIDTH), BF16),
        scratch_shapes=[pltpu.VMEM((tm, D_MODEL), BF16)],
        compiler_params=_params(("arbitrary", "arbitrary"), vmem),
        name="inproj",
    )(x2d, mod3, norm_g, w_in)


def _relbias_kernel(rb_ref, o_ref):
    h = pl.program_id(0)
    shape = (2 * MOBA_BLOCK, MOBA_BLOCK)
    key = lax.broadcasted_iota(jnp.int32, shape, 0)
    qry = lax.broadcasted_iota(jnp.int32, shape, 1)
    dist = qry - key + MOBA_BLOCK
    n = jnp.maximum(dist, 0)
    max_exact = REL_BUCKETS // 2
    nf = jnp.maximum(n, 1).astype(F32)
    large = max_exact + (jnp.log(nf / max_exact) / math.log(REL_MAX_DIST / max_exact)
                         * (REL_BUCKETS - max_exact)).astype(jnp.int32)
    large = jnp.minimum(large, REL_BUCKETS - 1)
    bucket = jnp.where(n < max_exact, n, large)
    bias = jnp.zeros(shape, F32)
    for b in range(REL_BUCKETS):
        bias = jnp.where(bucket == b, rb_ref[b, h], bias)
    o_ref[0] = jnp.where(dist >= 0, bias * LOG2E, MASK_VALUE)


def _relbias_tiles(rel_bias):
    return pl.pallas_call(
        _relbias_kernel,
        grid=(ATTN_HEADS,),
        in_specs=[pl.BlockSpec(memory_space=pltpu.SMEM)],
        out_specs=pl.BlockSpec((1, 2 * MOBA_BLOCK, MOBA_BLOCK), lambda h: (h, 0, 0)),
        out_shape=jax.ShapeDtypeStruct((ATTN_HEADS, 2 * MOBA_BLOCK, MOBA_BLOCK), F32),
        compiler_params=_params(("arbitrary",), 16 * MIB),
        name="relbias",
    )(rel_bias)


def _far_bucket_is_last():
    d = np.arange(MOBA_BLOCK + 1, SEQ, dtype=np.float32)
    max_exact = REL_BUCKETS // 2
    large = max_exact + (np.log(d / max_exact) / math.log(REL_MAX_DIST / max_exact)
                         * (REL_BUCKETS - max_exact)).astype(np.int32)
    return bool(np.all(np.minimum(large, REL_BUCKETS - 1) == REL_BUCKETS - 1))


def _moba_kernel(n_cast, rb_ref, q_ref, k_ref, v_ref, gq_ref, gk_ref, bt_ref, *refs):
    cast_src, (o_ref, *cast_dst) = refs[:n_cast], refs[n_cast:2 * n_cast + 1]
    qb_ref, kb_ref, vt_ref, *s_refs = refs[2 * n_cast + 1:]
    _run_cast_rider(cast_src, cast_dst)

    h = pl.program_id(1)
    far_bias = rb_ref[REL_BUCKETS - 1, h] * LOG2E
    blk, hd = MOBA_BLOCK, ATTN_HEAD_DIM
    nt = (((1,), (1,)), ((), ()))

    q = q_ref[0].astype(F32)
    k = k_ref[0].astype(F32)
    qn = (q * lax.rsqrt(jnp.mean(q * q, axis=-1, keepdims=True) + EPS)) * gq_ref[...]
    kn = (k * lax.rsqrt(jnp.mean(k * k, axis=-1, keepdims=True) + EPS)) * gk_ref[...]
    qb_ref[...] = (qn * (hd ** -0.5 * LOG2E)).astype(BF16)
    kb_ref[...] = kn.astype(BF16)
    vt_ref[0:hd, :] = v_ref[0].astype(F32).T.astype(BF16)
    vt_ref[hd:, :] = jnp.ones((MOBA_ONES_ROWS, SEQ), BF16)

    k_mean = jnp.concatenate(
        [jnp.sum(kn[n * blk:(n + 1) * blk], axis=0, keepdims=True) for n in range(N_BLOCKS)],
        axis=0) * (1.0 / blk)
    gate = lax.dot_general(k_mean, qn, nt, preferred_element_type=F32,
                           precision=lax.Precision.HIGHEST)
    row = lax.broadcasted_iota(jnp.int32, (N_BLOCKS, blk), 0)

    def scores(qi, s_ref):
        cols = slice(qi * blk, (qi + 1) * blk)
        nk = (qi + 1) * blk

        mask_add = None
        if qi > MOBA_TOPK:
            g = gate[:, cols]
            rank = jnp.zeros((N_BLOCKS, blk), F32)
            for m in range(qi):
                gm = g[m:m + 1, :]
                beats = (gm > g) | ((gm == g) & (row > m))
                rank = rank + jnp.where(beats, 1.0, 0.0)
            mask_add = jnp.where(rank < MOBA_TOPK, 0.0, MASK_VALUE)

        s_all = lax.dot_general(kb_ref[0:nk, :], qb_ref[cols, :], nt,
                                preferred_element_type=F32)
        m8 = None
        for n in range(qi + 1):
            s_blk = s_all[n * blk:(n + 1) * blk]
            if n == qi:
                s_blk = s_blk + bt_ref[0, blk:2 * blk, :]
            else:
                if n == qi - 1:
                    s_blk = s_blk + bt_ref[0, 0:blk, :]
                    if mask_add is not None:
                        s_blk = s_blk + mask_add[n:n + 1, :]
                elif mask_add is not None:
                    s_blk = s_blk + (mask_add[n:n + 1, :] + far_bias)
                else:
                    s_blk = s_blk + far_bias
            s_ref[n * blk:(n + 1) * blk, :] = s_blk
            b8 = jnp.max(s_blk.reshape(blk // 8, 8, blk), axis=0)
            m8 = b8 if m8 is None else jnp.maximum(m8, b8)
        return jnp.max(m8, axis=0, keepdims=True)

    def attend(qi, s_ref, m):
        cols = slice(qi * blk, (qi + 1) * blk)
        nk = (qi + 1) * blk
        p = jnp.exp2(s_ref[0:nk, :] - m).astype(BF16)
        o_aug = jnp.dot(vt_ref[:, 0:nk], p, preferred_element_type=F32)
        o_t = o_aug[0:hd] * (1.0 / o_aug[hd:hd + 1])
        o_ref[0, cols, :] = o_t.T.astype(o_ref.dtype)

    tiles = list(zip(MOBA_TILE_ORDER, s_refs))
    maxes = [scores(qi, s_ref) for qi, s_ref in tiles]
    for (qi, s_ref), m in zip(tiles, maxes):
        attend(qi, s_ref, m)


def _moba(proj3, rel_bias, q_norm_g, k_norm_g, bias_tiles, cast_weights, cast_tiles):
    hd = ATTN_HEAD_DIM
    head = lambda off: (lambda b, h: (b, 0, off // hd + h))
    cast_in, cast_out, cast_shapes = _cast_rider(
        cast_weights, cast_tiles, BATCH * ATTN_HEADS, lambda b, h: b * ATTN_HEADS + h)
    cast_bytes = sum(2 * (4 + 2) * w.size // (BATCH * ATTN_HEADS) for w in cast_weights)
    outs = pl.pallas_call(
        functools.partial(_moba_kernel, len(cast_weights)),
        grid=(BATCH, ATTN_HEADS),
        in_specs=[pl.BlockSpec(memory_space=pltpu.SMEM),
                  pl.BlockSpec((1, SEQ, hd), head(OFF_QA)),
                  pl.BlockSpec((1, SEQ, hd), head(OFF_KA)),
                  pl.BlockSpec((1, SEQ, hd), head(OFF_VA)),
                  pl.BlockSpec((1, hd), lambda b, h: (0, 0)),
                  pl.BlockSpec((1, hd), lambda b, h: (0, 0)),
                  pl.BlockSpec((1, 2 * MOBA_BLOCK, MOBA_BLOCK), lambda b, h: (h, 0, 0))]
                 + cast_in,
        out_specs=[pl.BlockSpec((1, SEQ, hd), lambda b, h: (b, 0, h))] + cast_out,
        out_shape=[jax.ShapeDtypeStruct((BATCH, SEQ, ATTN_WIDTH), BF16)] + cast_shapes,
        scratch_shapes=[pltpu.VMEM((SEQ, hd), BF16),
                        pltpu.VMEM((SEQ, hd), BF16),
                        pltpu.VMEM((hd + MOBA_ONES_ROWS, SEQ), BF16)]
                       + [pltpu.VMEM((SEQ, MOBA_BLOCK), F32)] * len(MOBA_TILE_ORDER),
        compiler_params=_params(("arbitrary", "arbitrary"), 24 * MIB + cast_bytes),
        name="moba",
    )(rel_bias, proj3, proj3, proj3, q_norm_g, k_norm_g, bias_tiles, *cast_weights)
    return outs[0], outs[1:]


def _retention_kernel(n_cast, cd_ref, q_ref, k_ref, v_ref, gr_ref, cos_ref, sin_ref,
                      dmask_ref, qdec_ref, kdec_ref, gn_ref, *refs):
    cast_src, (o_ref, *cast_dst) = refs[:n_cast], refs[n_cast:2 * n_cast + 1]
    qf_ref, kf_ref, y_ref, kv_ref = refs[2 * n_cast + 1:]
    _run_cast_rider(cast_src, cast_dst)
    h = pl.program_id(1)
    chunk_decay = cd_ref[h]
    half = RET_KEY_DIM // 2
    cos = cos_ref[...]
    sin = sin_ref[...]
    q = q_ref[0].astype(F32)
    k = k_ref[0].astype(F32)
    qf_ref[...] = q * cos + pltpu.roll(q, half, 1) * sin
    kf_ref[...] = (k * cos + pltpu.roll(k, half, 1) * sin) * (RET_KEY_DIM ** -0.5)

    nt = (((1,), (1,)), ((), ()))
    chunks = [slice(c * RET_CHUNK, (c + 1) * RET_CHUNK) for c in range(N_CHUNKS)]

    scores = [(lax.dot_general(qf_ref[rows, :].astype(BF16), kf_ref[rows, :].astype(BF16), nt,
                               preferred_element_type=F32) * dmask_ref[0]).astype(BF16)
              for rows in chunks]
    for c, rows in enumerate(chunks):
        vc = v_ref[0, rows, :]
        y_ref[rows, :] = jnp.dot(scores[c], vc, preferred_element_type=F32)
        kd_t = (kf_ref[rows, :] * kdec_ref[0]).T.astype(BF16)
        kv_ref[c] = jnp.dot(kd_t, vc, preferred_element_type=F32)

    state = jnp.zeros((RET_KEY_DIM, RET_VAL_DIM), F32)
    for c, rows in enumerate(chunks):
        y_ref[rows, :] += jnp.dot((qf_ref[rows, :] * qdec_ref[0]).astype(BF16),
                                  state.astype(BF16), preferred_element_type=F32)
        state = chunk_decay * state + kv_ref[c]

    for rows in chunks:
        y = y_ref[rows, :]
        mu = jnp.mean(y, axis=-1, keepdims=True)
        yc = y - mu
        var = jnp.mean(yc * yc, axis=-1, keepdims=True)
        yn = (yc * lax.rsqrt(var + EPS)) * gn_ref[...]
        o_ref[0, rows, :] = (yn * _silu(gr_ref[0, rows, :]).astype(F32)).astype(o_ref.dtype)


def _retention_tables():
    half = RET_KEY_DIM // 2
    freqs = jnp.power(ROPE_BASE, -jnp.arange(half, dtype=F32) / half)
    ang = jnp.arange(SEQ).astype(F32)[:, None] * freqs[None, :]
    cos, sin = jnp.cos(ang), jnp.sin(ang)
    cos_full = jnp.concatenate([cos, cos], axis=-1)
    sin_signed = jnp.concatenate([-sin, sin], axis=-1)

    log_decay = jnp.log(1.0 - jnp.power(2.0, -5.0 - jnp.arange(RET_HEADS, dtype=F32)))
    i = jnp.arange(RET_CHUNK, dtype=F32)
    diff = i[:, None] - i[None, :]
    ld = log_decay[:, None, None]
    inner_decay = jnp.where(diff >= 0, jnp.exp(ld * jnp.maximum(diff, 0.0)), 0.0)
    q_decay = jnp.exp(log_decay[:, None] * (i + 1.0))
    k_decay = jnp.exp(log_decay[:, None] * (RET_CHUNK - 1.0 - i))
    chunk_decay = jnp.exp(log_decay * RET_CHUNK)
    bcast = lambda t: jnp.broadcast_to(t[:, :, None], (RET_HEADS, RET_CHUNK, RET_KEY_DIM))
    return cos_full, sin_signed, inner_decay, bcast(q_decay), bcast(k_decay), chunk_decay


def _retention(proj3, ret_norm_g, cast_weights, cast_tiles):
    dk, dv = RET_KEY_DIM, RET_VAL_DIM
    cos, sin, dmask, qdec, kdec, chunk_decay = _retention_tables()
    head = lambda off, w: (lambda b, h: (b, 0, off // w + h))
    per_head = lambda b, h: (h, 0, 0)
    cast_in, cast_out, cast_shapes = _cast_rider(
        cast_weights, cast_tiles, BATCH * RET_HEADS, lambda b, h: b * RET_HEADS + h)
    cast_bytes = sum(2 * (4 + 2) * w.size // (BATCH * RET_HEADS) for w in cast_weights)
    outs = pl.pallas_call(
        functools.partial(_retention_kernel, len(cast_weights)),
        grid=(BATCH, RET_HEADS),
        in_specs=[pl.BlockSpec(memory_space=pltpu.SMEM),
                  pl.BlockSpec((1, SEQ, dk), head(OFF_QR, dk)),
                  pl.BlockSpec((1, SEQ, dk), head(OFF_KR, dk)),
                  pl.BlockSpec((1, SEQ, dv), head(OFF_VR, dv)),
                  pl.BlockSpec((1, SEQ, dv), head(OFF_GR, dv)),
                  pl.BlockSpec((SEQ, dk), lambda b, h: (0, 0)),
                  pl.BlockSpec((SEQ, dk), lambda b, h: (0, 0)),
                  pl.BlockSpec((1, RET_CHUNK, RET_CHUNK), per_head),
                  pl.BlockSpec((1, RET_CHUNK, dk), per_head),
                  pl.BlockSpec((1, RET_CHUNK, dk), per_head),
                  pl.BlockSpec((1, dv), lambda b, h: (0, h))] + cast_in,
        out_specs=[pl.BlockSpec((1, SEQ, dv), lambda b, h: (b, 0, h))] + cast_out,
        out_shape=[jax.ShapeDtypeStruct((BATCH, SEQ, RET_V_WIDTH), BF16)] + cast_shapes,
        scratch_shapes=[pltpu.VMEM((SEQ, dk), F32), pltpu.VMEM((SEQ, dk), F32),
                        pltpu.VMEM((SEQ, dv), F32), pltpu.VMEM((N_CHUNKS, dk, dv), F32)],
        compiler_params=_params(("arbitrary", "arbitrary"), 24 * MIB + cast_bytes),
        name="retention",
    )(chunk_decay, proj3, proj3, proj3, proj3, cos, sin, dmask, qdec, kdec, ret_norm_g,
      *cast_weights)
    return outs[0], outs[1:]


def _mixer_kernel(ya_ref, yr_ref, ga_ref, gb_ref, wa_ref, wr_ref, wo_ref, x_ref, mod_ref,
                  o_ref):
    j = pl.program_id(1)

    @pl.when(j == 0)
    def _():
        o_ref[...] = x_ref[...]

    a = jnp.dot(ya_ref[...], wa_ref[j], preferred_element_type=F32)
    r = jnp.dot(yr_ref[...], wr_ref[j], preferred_element_type=F32)
    merged = (_sigmoid(ga_ref[...].astype(F32)) * a
              + _sigmoid(gb_ref[...].astype(F32)) * r)
    wo_rows = pl.ds(pl.multiple_of(j * MIX_TN, MIX_TN), MIX_TN)
    o_ref[...] += mod_ref[0, GATE1:GATE1 + 1, :] * jnp.dot(
        merged.astype(BF16), wo_ref[wo_rows, :], preferred_element_type=F32)


def _mixer(ya2d, yr2d, proj2d, w_attn_br, w_ret_br, w_o, x2d, mod3):
    tm, tn = MIX_TM, MIX_TN
    nj = D_MODEL // tn
    assert w_attn_br.shape == (nj, ATTN_WIDTH, tn) and w_ret_br.shape == (nj, RET_V_WIDTH, tn)
    assert OFF_GA % tn == 0 and OFF_GB % tn == 0
    resident = lambda shape: pl.BlockSpec(shape, lambda i, j: (0,) * len(shape),
                                          pipeline_mode=pl.Buffered(1))
    vmem = (2 * tm * (ATTN_WIDTH + RET_V_WIDTH + 2 * tn) * 2
            + (ATTN_WIDTH + RET_V_WIDTH + D_MODEL) * D_MODEL * 2
            + 4 * tm * D_MODEL * 4 + 3 * tm * tn * 4 + 4 * MIB)
    return pl.pallas_call(
        _mixer_kernel,
        grid=(TOKENS // tm, D_MODEL // tn),
        in_specs=[pl.BlockSpec((tm, ATTN_WIDTH), lambda i, j: (i, 0)),
                  pl.BlockSpec((tm, RET_V_WIDTH), lambda i, j: (i, 0)),
                  pl.BlockSpec((tm, tn), lambda i, j: (i, OFF_GA // tn + j)),
                  pl.BlockSpec((tm, tn), lambda i, j: (i, OFF_GB // tn + j)),
                  resident((nj, ATTN_WIDTH, tn)),
                  resident((nj, RET_V_WIDTH, tn)),
                  resident((D_MODEL, D_MODEL)),
                  pl.BlockSpec((tm, D_MODEL), lambda i, j: (i, 0)),
                  pl.BlockSpec((1, N_MOD, D_MODEL), lambda i, j: (i // (SEQ // tm), 0, 0))],
        out_specs=pl.BlockSpec((tm, D_MODEL), lambda i, j: (i, 0)),
        out_shape=jax.ShapeDtypeStruct((TOKENS, D_MODEL), F32),
        compiler_params=_params(("arbitrary", "arbitrary"), vmem),
        name="mixer",
    )(ya2d, yr2d, proj2d, proj2d, w_attn_br, w_ret_br, w_o, x2d, mod3)


def _ffn_kernel(x_ref, halo_ref, mod_ref, g_ref, wv_ref, wg_ref, cwv_ref, cwg_ref,
                cbv_ref, cbg_ref, wd_ref, o_ref, h_ref, u_ref):
    i = pl.program_id(0)
    j = pl.program_id(1)
    tm, halo = FFN_TM, FFN_HALO

    @pl.when(j == 0)
    def _():
        o_ref[...] = jnp.zeros_like(o_ref)
        shift = mod_ref[0, SHIFT2:SHIFT2 + 1, :]
        gain = g_ref[...] * (1.0 + mod_ref[0, SCALE2:SCALE2 + 1, :])
        seq_start = (i % (SEQ // tm)) == 0
        h_halo = _rms_mod(halo_ref[...], gain, shift)
        h_ref[0:halo, :] = jnp.where(seq_start, 0.0, h_halo).astype(BF16)

        def body(r, carry):
            src = pl.ds(pl.multiple_of(r * NORM_ROWS, NORM_ROWS), NORM_ROWS)
            dst = pl.ds(pl.multiple_of(halo + r * NORM_ROWS, halo), NORM_ROWS)
            h_ref[dst, :] = _rms_mod(x_ref[src, :], gain, shift).astype(BF16)
            return carry

        lax.fori_loop(0, tm // NORM_ROWS, body, 0, unroll=NORM_UNROLL)

    def conv(half, w_ref, cw_ref, cb_ref):
        u_ref[half] = jnp.dot(h_ref[...], w_ref[...], preferred_element_type=F32)
        y = cb_ref[...]
        for t in range(CONV_WIDTH):
            lag = CONV_WIDTH - 1 - t
            y = y + cw_ref[0, t:t + 1, :] * u_ref[half, halo - lag:halo - lag + tm, :]
        return y

    val = conv(0, wv_ref, cwv_ref, cbv_ref)
    gt = conv(1, wg_ref, cwg_ref, cbg_ref)
    kw = FFN_TN // FFN_DOWN_SPLITS
    for kh in range(FFN_DOWN_SPLITS):
        cols = slice(kh * kw, (kh + 1) * kw)
        act = (_silu(gt[:, cols]) * val[:, cols]).astype(BF16)
        o_ref[...] += jnp.dot(act, wd_ref[cols, :], preferred_element_type=F32)

    @pl.when(j == pl.num_programs(1) - 1)
    def _():
        o_ref[...] = x_ref[...] + mod_ref[0, GATE2:GATE2 + 1, :] * o_ref[...]


def _ffn(x1, mod3, norm_g, w_up, conv_w, conv_b, w_down):
    tm, tn, halo, nj = FFN_TM, FFN_TN, FFN_HALO, FFN_NJ
    vmem = (4 * tm * D_MODEL * 4 + 2 * halo * D_MODEL * 4
            + 2 * 3 * D_MODEL * tn * 2
            + (tm + halo) * D_MODEL * 2 + 2 * (tm + halo) * tn * 4
            + tm * tn * 4 + 4 * MIB)
    return pl.pallas_call(
        _ffn_kernel,
        grid=(TOKENS // tm, nj),
        in_specs=[pl.BlockSpec((tm, D_MODEL), lambda i, j: (i, 0)),
                  pl.BlockSpec((halo, D_MODEL),
                               lambda i, j: (jnp.maximum(i * (tm // halo) - 1, 0), 0)),
                  pl.BlockSpec((1, N_MOD, D_MODEL), lambda i, j: (i // (SEQ // tm), 0, 0)),
                  pl.BlockSpec((1, D_MODEL), lambda i, j: (0, 0)),
                  pl.BlockSpec((D_MODEL, tn), lambda i, j: (0, j)),
                  pl.BlockSpec((D_MODEL, tn), lambda i, j: (0, nj + j)),
                  pl.BlockSpec((1, CONV_WIDTH, tn), lambda i, j: (0, 0, j)),
                  pl.BlockSpec((1, CONV_WIDTH, tn), lambda i, j: (0, 0, nj + j)),
                  pl.BlockSpec((1, tn), lambda i, j: (0, j)),
                  pl.BlockSpec((1, tn), lambda i, j: (0, nj + j)),
                  pl.BlockSpec((tn, D_MODEL), lambda i, j: (j, 0))],
        out_specs=pl.BlockSpec((tm, D_MODEL), lambda i, j: (i, 0)),
        out_shape=jax.ShapeDtypeStruct((TOKENS, D_MODEL), F32),
        scratch_shapes=[pltpu.VMEM((tm + halo, D_MODEL), BF16),
                        pltpu.VMEM((2, tm + halo, tn), F32)],
        compiler_params=_params(("arbitrary", "arbitrary"), vmem),
        name="ffn",
    )(x1, x1, mod3, norm_g, w_up, w_up, conv_w, conv_w, conv_b, conv_b, w_down)


def kernel(x, c, w_ada, b_ada, norm1_g, w_in, q_norm_g, k_norm_g, rel_bias, ret_norm_g,
           w_attn_br, w_ret_br, w_o, norm2_g, w_up, conv_w, conv_b, w_down):
    assert x.shape == (BATCH, SEQ, D_MODEL) and w_ada.shape[0] == 1
    assert _far_bucket_is_last()
    layer = 0
    x2d = x.reshape(TOKENS, D_MODEL)

    c_pad = jnp.pad(c, ((0, 8 - BATCH), (0, 0)))
    mod = _modulation(c_pad, w_ada[layer], b_ada)[:BATCH]
    mod3 = mod.reshape(BATCH, N_MOD, D_MODEL)

    proj = _input_projection(x2d, mod3, norm1_g, w_in[layer])
    proj3 = proj.reshape(BATCH, SEQ, IN_WIDTH)

    bias_tiles = _relbias_tiles(rel_bias)
    yr, (w_attn_b, w_ret_b, w_o_b) = _retention(
        proj3, ret_norm_g, (w_attn_br[layer], w_ret_br[layer], w_o[layer]),
        (MIX_TN, MIX_TN, None))
    ya, (w_up_b, w_down_b) = _moba(proj3, rel_bias, q_norm_g, k_norm_g, bias_tiles,
                                   (w_up[layer], w_down[layer]), (None, None))

    x1 = _mixer(ya.reshape(TOKENS, ATTN_WIDTH), yr.reshape(TOKENS, RET_V_WIDTH), proj,
                w_attn_b, w_ret_b, w_o_b, x2d, mod3)

    out = _ffn(x1, mod3, norm2_g, w_up_b, conv_w, conv_b, w_down_b)
    return out.reshape(BATCH, SEQ, D_MODEL)
```

```python
import functools
import math

import numpy as np
import jax
import jax.numpy as jnp
from jax import lax
from jax.experimental import pallas as pl
from jax.experimental.pallas import tpu as pltpu

F32 = jnp.float32
BF16 = jnp.bfloat16

D_MODEL = 2048
BATCH = 4
SEQ = 2048
ATTN_HEADS = 8
ATTN_HEAD_DIM = 128
MOBA_BLOCK = 256
MOBA_TOPK = 3
REL_BUCKETS = 32
REL_MAX_DIST = 128
RET_HEADS = 8
RET_KEY_DIM = 128
RET_VAL_DIM = 256
RET_CHUNK = 128
ROPE_BASE = 10000.0
FFN_DIM = 5632
CONV_WIDTH = 3
EPS = 1e-6
N_MOD = 6

ATTN_WIDTH = ATTN_HEADS * ATTN_HEAD_DIM
RET_QK_WIDTH = RET_HEADS * RET_KEY_DIM
RET_V_WIDTH = RET_HEADS * RET_VAL_DIM
OFF_QA = 0
OFF_KA = OFF_QA + ATTN_WIDTH
OFF_VA = OFF_KA + ATTN_WIDTH
OFF_QR = OFF_VA + ATTN_WIDTH
OFF_KR = OFF_QR + RET_QK_WIDTH
OFF_VR = OFF_KR + RET_QK_WIDTH
OFF_GR = OFF_VR + RET_V_WIDTH
OFF_GA = OFF_GR + RET_V_WIDTH
OFF_GB = OFF_GA + D_MODEL
IN_WIDTH = OFF_GB + D_MODEL

TOKENS = BATCH * SEQ
N_BLOCKS = SEQ // MOBA_BLOCK
N_CHUNKS = SEQ // RET_CHUNK
MASK_VALUE = -1e30
LOG2E = math.log2(math.e)
MOBA_ONES_ROWS = 16
MOBA_TILE_ORDER = (0, 7, 3, 4, 1, 6, 2, 5)
MIB = 1024 * 1024

SHIFT1, SCALE1, GATE1, SHIFT2, SCALE2, GATE2 = range(N_MOD)

MOD_TN = 1024
PRENORM_TM = 512
INPROJ_TM, INPROJ_TN = 2048, 1024
MIX_TM, MIX_TN = 512, 1024
FFN_TM, FFN_TN = 1024, 512
FFN_HALO = 16
FFN_NJ = FFN_DIM // FFN_TN
FFN_DOWN_SPLITS = 2
NORM_ROWS = 16
NORM_UNROLL = 16


def _params(semantics, vmem_bytes):
    return pltpu.CompilerParams(dimension_semantics=semantics,
                                vmem_limit_bytes=int(vmem_bytes))


def _sigmoid(v):
    return 0.5 + 0.5 * jnp.tanh(0.5 * v)


def _silu(v):
    hv = 0.5 * v
    return hv + hv * jnp.tanh(hv)


def _cast_rider(weights, col_tiles, steps, step_of):
    in_specs, out_specs, out_shapes = [], [], []
    for w, tile in zip(weights, col_tiles):
        rows, cols = w.shape
        assert rows % (steps * 16) == 0, (rows, steps)
        slab = rows // steps
        in_specs.append(pl.BlockSpec((slab, cols), lambda *g: (step_of(*g), 0)))
        if tile is None:
            out_specs.append(pl.BlockSpec((slab, cols), lambda *g: (step_of(*g), 0)))
            out_shapes.append(jax.ShapeDtypeStruct((rows, cols), BF16))
        else:
            out_specs.append(pl.BlockSpec((cols // tile, slab, tile),
                                          lambda *g: (0, step_of(*g), 0)))
            out_shapes.append(jax.ShapeDtypeStruct((cols // tile, rows, tile), BF16))
    return in_specs, out_specs, out_shapes


def _run_cast_rider(src_refs, dst_refs):
    for src, dst in zip(src_refs, dst_refs):
        if len(dst.shape) == 2:
            dst[...] = src[...].astype(BF16)
        else:
            tile = dst.shape[-1]
            for t in range(dst.shape[0]):
                dst[t] = src[:, t * tile:(t + 1) * tile].astype(BF16)


def _rms_mod(x, gain, shift):
    ms = jnp.mean(x * x, axis=-1, keepdims=True)
    return (x * lax.rsqrt(ms + EPS)) * gain + shift


def _split_bf16(v):
    hi = v.astype(BF16)
    lo = (v - hi.astype(F32)).astype(BF16)
    return hi, lo


def _mod_kernel(c_ref, w_ref, b_ref, o_ref):
    rows = c_ref.shape[0]
    s_hi, s_lo = _split_bf16(_silu(c_ref[...]))
    w_hi, w_lo = _split_bf16(w_ref[...])
    both = jnp.dot(jnp.concatenate([s_hi, s_lo], axis=0), w_hi, preferred_element_type=F32)
    cross = jnp.dot(s_hi, w_lo, preferred_element_type=F32)
    o_ref[...] = (both[0:rows] + both[rows:]) + cross + b_ref[...]


def _modulation(c_pad, w_ada, b_ada):
    rows = c_pad.shape[0]
    n = w_ada.shape[1]
    return pl.pallas_call(
        _mod_kernel,
        grid=(n // MOD_TN,),
        in_specs=[pl.BlockSpec((rows, D_MODEL), lambda j: (0, 0)),
                  pl.BlockSpec((D_MODEL, MOD_TN), lambda j: (0, j)),
                  pl.BlockSpec((1, MOD_TN), lambda j: (0, j))],
        out_specs=pl.BlockSpec((rows, MOD_TN), lambda j: (0, j)),
        out_shape=jax.ShapeDtypeStruct((rows, n), F32),
        compiler_params=_params(("arbitrary",), 2 * D_MODEL * MOD_TN * 4 + 8 * MIB),
        name="mod",
    )(c_pad, w_ada, b_ada)


def _prenorm_kernel(x_ref, mod_ref, g_ref, h_ref):
    shift = mod_ref[0, SHIFT1:SHIFT1 + 1, :]
    gain = g_ref[...] * (1.0 + mod_ref[0, SCALE1:SCALE1 + 1, :])

    def body(r, carry):
        rows = pl.ds(pl.multiple_of(r * NORM_ROWS, NORM_ROWS), NORM_ROWS)
        h_ref[rows, :] = _rms_mod(x_ref[rows, :], gain, shift).astype(BF16)
        return carry

    lax.fori_loop(0, PRENORM_TM // NORM_ROWS, body, 0, unroll=NORM_UNROLL)


def _prenorm(x2d, mod3, norm_g):
    tm = PRENORM_TM
    return pl.pallas_call(
        _prenorm_kernel,
        grid=(TOKENS // tm,),
        in_specs=[pl.BlockSpec((tm, D_MODEL), lambda i: (i, 0)),
                  pl.BlockSpec((1, N_MOD, D_MODEL), lambda i: (i // (SEQ // tm), 0, 0)),
                  pl.BlockSpec((1, D_MODEL), lambda i: (0, 0))],
        out_specs=pl.BlockSpec((tm, D_MODEL), lambda i: (i, 0)),
        out_shape=jax.ShapeDtypeStruct((TOKENS, D_MODEL), BF16),
        compiler_params=_params(("arbitrary",), 2 * tm * D_MODEL * (4 + 2) + 4 * MIB),
        name="prenorm",
    )(x2d, mod3, norm_g)


def _inproj_kernel(h_ref, w_ref, o_ref):
    o_ref[...] = jnp.dot(h_ref[...], w_ref[...].astype(BF16),
                         preferred_element_type=F32).astype(o_ref.dtype)


def _input_projection(h2d, w_in):
    tm, tn = INPROJ_TM, INPROJ_TN
    vmem = (2 * tm * D_MODEL * 2 + 2 * D_MODEL * tn * 4 + 2 * tm * tn * 2
            + D_MODEL * tn * 2 + tm * tn * 4 + 4 * MIB)
    return pl.pallas_call(
        _inproj_kernel,
        grid=(IN_WIDTH // tn, TOKENS // tm),
        in_specs=[pl.BlockSpec((tm, D_MODEL), lambda j, i: (i, 0)),
                  pl.BlockSpec((D_MODEL, tn), lambda j, i: (0, j))],
        out_specs=pl.BlockSpec((tm, tn), lambda j, i: (i, j)),
        out_shape=jax.ShapeDtypeStruct((TOKENS, IN_WIDTH), BF16),
        compiler_params=_params(("arbitrary", "arbitrary"), vmem),
        name="inproj",
    )(h2d, w_in)


def _relbias_kernel(rb_ref, o_ref):
    h = pl.program_id(0)
    shape = (2 * MOBA_BLOCK, MOBA_BLOCK)
    key = lax.broadcasted_iota(jnp.int32, shape, 0)
    qry = lax.broadcasted_iota(jnp.int32, shape, 1)
    dist = qry - key + MOBA_BLOCK
    n = jnp.maximum(dist, 0)
    max_exact = REL_BUCKETS // 2
    nf = jnp.maximum(n, 1).astype(F32)
    large = max_exact + (jnp.log(nf / max_exact) / math.log(REL_MAX_DIST / max_exact)
                         * (REL_BUCKETS - max_exact)).astype(jnp.int32)
    large = jnp.minimum(large, REL_BUCKETS - 1)
    bucket = jnp.where(n < max_exact, n, large)
    bias = jnp.zeros(shape, F32)
    for b in range(REL_BUCKETS):
        bias = jnp.where(bucket == b, rb_ref[b, h], bias)
    o_ref[0] = jnp.where(dist >= 0, bias * LOG2E, MASK_VALUE)


def _relbias_tiles(rel_bias):
    return pl.pallas_call(
        _relbias_kernel,
        grid=(ATTN_HEADS,),
        in_specs=[pl.BlockSpec(memory_space=pltpu.SMEM)],
        out_specs=pl.BlockSpec((1, 2 * MOBA_BLOCK, MOBA_BLOCK), lambda h: (h, 0, 0)),
        out_shape=jax.ShapeDtypeStruct((ATTN_HEADS, 2 * MOBA_BLOCK, MOBA_BLOCK), F32),
        compiler_params=_params(("arbitrary",), 16 * MIB),
        name="relbias",
    )(rel_bias)


def _far_bucket_is_last():
    d = np.arange(MOBA_BLOCK + 1, SEQ, dtype=np.float32)
    max_exact = REL_BUCKETS // 2
    large = max_exact + (np.log(d / max_exact) / math.log(REL_MAX_DIST / max_exact)
                         * (REL_BUCKETS - max_exact)).astype(np.int32)
    return bool(np.all(np.minimum(large, REL_BUCKETS - 1) == REL_BUCKETS - 1))


def _moba_kernel(n_cast, rb_ref, q_ref, k_ref, v_ref, gq_ref, gk_ref, bt_ref, *refs):
    cast_src, (o_ref, *cast_dst) = refs[:n_cast], refs[n_cast:2 * n_cast + 1]
    qb_ref, kb_ref, vt_ref, *s_refs = refs[2 * n_cast + 1:]
    _run_cast_rider(cast_src, cast_dst)

    h = pl.program_id(1)
    far_bias = rb_ref[REL_BUCKETS - 1, h] * LOG2E
    blk, hd = MOBA_BLOCK, ATTN_HEAD_DIM
    nt = (((1,), (1,)), ((), ()))

    q = q_ref[0].astype(F32)
    k = k_ref[0].astype(F32)
    qn = (q * lax.rsqrt(jnp.mean(q * q, axis=-1, keepdims=True) + EPS)) * gq_ref[...]
    kn = (k * lax.rsqrt(jnp.mean(k * k, axis=-1, keepdims=True) + EPS)) * gk_ref[...]
    qb_ref[...] = (qn * (hd ** -0.5 * LOG2E)).astype(BF16)
    kb_ref[...] = kn.astype(BF16)
    vt_ref[0:hd, :] = v_ref[0].astype(F32).T.astype(BF16)
    vt_ref[hd:, :] = jnp.ones((MOBA_ONES_ROWS, SEQ), BF16)

    k_mean = jnp.concatenate(
        [jnp.sum(kn[n * blk:(n + 1) * blk], axis=0, keepdims=True) for n in range(N_BLOCKS)],
        axis=0) * (1.0 / blk)
    gate = lax.dot_general(k_mean, qn, nt, preferred_element_type=F32,
                           precision=lax.Precision.HIGHEST)
    row = lax.broadcasted_iota(jnp.int32, (N_BLOCKS, blk), 0)

    def scores(qi, s_ref):
        cols = slice(qi * blk, (qi + 1) * blk)
        nk = (qi + 1) * blk

        mask_add = None
        if qi > MOBA_TOPK:
            g = gate[:, cols]
            rank = jnp.zeros((N_BLOCKS, blk), F32)
            for m in range(qi):
                gm = g[m:m + 1, :]
                beats = (gm > g) | ((gm == g) & (row > m))
                rank = rank + jnp.where(beats, 1.0, 0.0)
            mask_add = jnp.where(rank < MOBA_TOPK, 0.0, MASK_VALUE)

        s_all = lax.dot_general(kb_ref[0:nk, :], qb_ref[cols, :], nt,
                                preferred_element_type=F32)
        m8 = None
        for n in range(qi + 1):
            s_blk = s_all[n * blk:(n + 1) * blk]
            if n == qi:
                s_blk = s_blk + bt_ref[0, blk:2 * blk, :]
            else:
                if n == qi - 1:
                    s_blk = s_blk + bt_ref[0, 0:blk, :]
                    if mask_add is not None:
                        s_blk = s_blk + mask_add[n:n + 1, :]
                elif mask_add is not None:
                    s_blk = s_blk + (mask_add[n:n + 1, :] + far_bias)
                else:
                    s_blk = s_blk + far_bias
            s_ref[n * blk:(n + 1) * blk, :] = s_blk
            b8 = jnp.max(s_blk.reshape(blk // 8, 8, blk), axis=0)
            m8 = b8 if m8 is None else jnp.maximum(m8, b8)
        return jnp.max(m8, axis=0, keepdims=True)

    def attend(qi, s_ref, m):
        cols = slice(qi * blk, (qi + 1) * blk)
        nk = (qi + 1) * blk
        p = jnp.exp2(s_ref[0:nk, :] - m).astype(BF16)
        o_aug = jnp.dot(vt_ref[:, 0:nk], p, preferred_element_type=F32)
        o_t = o_aug[0:hd] * (1.0 / o_aug[hd:hd + 1])
        o_ref[0, cols, :] = o_t.T.astype(o_ref.dtype)

    tiles = list(zip(MOBA_TILE_ORDER, s_refs))
    maxes = [scores(qi, s_ref) for qi, s_ref in tiles]
    for (qi, s_ref), m in zip(tiles, maxes):
        attend(qi, s_ref, m)


def _moba(proj3, rel_bias, q_norm_g, k_norm_g, bias_tiles, cast_weights, cast_tiles):
    hd = ATTN_HEAD_DIM
    head = lambda off: (lambda b, h: (b, 0, off // hd + h))
    cast_in, cast_out, cast_shapes = _cast_rider(
        cast_weights, cast_tiles, BATCH * ATTN_HEADS, lambda b, h: b * ATTN_HEADS + h)
    cast_bytes = sum(2 * (4 + 2) * w.size // (BATCH * ATTN_HEADS) for w in cast_weights)
    outs = pl.pallas_call(
        functools.partial(_moba_kernel, len(cast_weights)),
        grid=(BATCH, ATTN_HEADS),
        in_specs=[pl.BlockSpec(memory_space=pltpu.SMEM),
                  pl.BlockSpec((1, SEQ, hd), head(OFF_QA)),
                  pl.BlockSpec((1, SEQ, hd), head(OFF_KA)),
                  pl.BlockSpec((1, SEQ, hd), head(OFF_VA)),
                  pl.BlockSpec((1, hd), lambda b, h: (0, 0)),
                  pl.BlockSpec((1, hd), lambda b, h: (0, 0)),
                  pl.BlockSpec((1, 2 * MOBA_BLOCK, MOBA_BLOCK), lambda b, h: (h, 0, 0))]
                 + cast_in,
        out_specs=[pl.BlockSpec((1, SEQ, hd), lambda b, h: (b, 0, h))] + cast_out,
        out_shape=[jax.ShapeDtypeStruct((BATCH, SEQ, ATTN_WIDTH), BF16)] + cast_shapes,
        scratch_shapes=[pltpu.VMEM((SEQ, hd), BF16),
                        pltpu.VMEM((SEQ, hd), BF16),
                        pltpu.VMEM((hd + MOBA_ONES_ROWS, SEQ), BF16)]
                       + [pltpu.VMEM((SEQ, MOBA_BLOCK), F32)] * len(MOBA_TILE_ORDER),
        compiler_params=_params(("arbitrary", "arbitrary"), 24 * MIB + cast_bytes),
        name="moba",
    )(rel_bias, proj3, proj3, proj3, q_norm_g, k_norm_g, bias_tiles, *cast_weights)
    return outs[0], outs[1:]


def _retention_kernel(n_cast, cd_ref, q_ref, k_ref, v_ref, gr_ref, cos_ref, sin_ref,
                      dmask_ref, qdec_ref, kdec_ref, gn_ref, *refs):
    cast_src, (o_ref, *cast_dst) = refs[:n_cast], refs[n_cast:2 * n_cast + 1]
    qf_ref, kf_ref, y_ref, kv_ref = refs[2 * n_cast + 1:]
    _run_cast_rider(cast_src, cast_dst)
    h = pl.program_id(1)
    chunk_decay = cd_ref[h]
    half = RET_KEY_DIM // 2
    cos = cos_ref[...]
    sin = sin_ref[...]
    q = q_ref[0].astype(F32)
    k = k_ref[0].astype(F32)
    qf_ref[...] = q * cos + pltpu.roll(q, half, 1) * sin
    kf_ref[...] = (k * cos + pltpu.roll(k, half, 1) * sin) * (RET_KEY_DIM ** -0.5)

    nt = (((1,), (1,)), ((), ()))
    chunks = [slice(c * RET_CHUNK, (c + 1) * RET_CHUNK) for c in range(N_CHUNKS)]

    scores = [(lax.dot_general(qf_ref[rows, :].astype(BF16), kf_ref[rows, :].astype(BF16), nt,
                               preferred_element_type=F32) * dmask_ref[0]).astype(BF16)
              for rows in chunks]
    for c, rows in enumerate(chunks):
        vc = v_ref[0, rows, :]
        y_ref[rows, :] = jnp.dot(scores[c], vc, preferred_element_type=F32)
        kd_t = (kf_ref[rows, :] * kdec_ref[0]).T.astype(BF16)
        kv_ref[c] = jnp.dot(kd_t, vc, preferred_element_type=F32)

    state = jnp.zeros((RET_KEY_DIM, RET_VAL_DIM), F32)
    for c, rows in enumerate(chunks):
        y_ref[rows, :] += jnp.dot((qf_ref[rows, :] * qdec_ref[0]).astype(BF16),
                                  state.astype(BF16), preferred_element_type=F32)
        state = chunk_decay * state + kv_ref[c]

    for rows in chunks:
        y = y_ref[rows, :]
        mu = jnp.mean(y, axis=-1, keepdims=True)
        yc = y - mu
        var = jnp.mean(yc * yc, axis=-1, keepdims=True)
        yn = (yc * lax.rsqrt(var + EPS)) * gn_ref[...]
        o_ref[0, rows, :] = (yn * _silu(gr_ref[0, rows, :]).astype(F32)).astype(o_ref.dtype)


def _retention_tables():
    half = RET_KEY_DIM // 2
    freqs = jnp.power(ROPE_BASE, -jnp.arange(half, dtype=F32) / half)
    ang = jnp.arange(SEQ).astype(F32)[:, None] * freqs[None, :]
    cos, sin = jnp.cos(ang), jnp.sin(ang)
    cos_full = jnp.concatenate([cos, cos], axis=-1)
    sin_signed = jnp.concatenate([-sin, sin], axis=-1)

    log_decay = jnp.log(1.0 - jnp.power(2.0, -5.0 - jnp.arange(RET_HEADS, dtype=F32)))
    i = jnp.arange(RET_CHUNK, dtype=F32)
    diff = i[:, None] - i[None, :]
    ld = log_decay[:, None, None]
    inner_decay = jnp.where(diff >= 0, jnp.exp(ld * jnp.maximum(diff, 0.0)), 0.0)
    q_decay = jnp.exp(log_decay[:, None] * (i + 1.0))
    k_decay = jnp.exp(log_decay[:, None] * (RET_CHUNK - 1.0 - i))
    chunk_decay = jnp.exp(log_decay * RET_CHUNK)
    bcast = lambda t: jnp.broadcast_to(t[:, :, None], (RET_HEADS, RET_CHUNK, RET_KEY_DIM))
    return cos_full, sin_signed, inner_decay, bcast(q_decay), bcast(k_decay), chunk_decay


def _retention(proj3, ret_norm_g, cast_weights, cast_tiles):
    dk, dv = RET_KEY_DIM, RET_VAL_DIM
    cos, sin, dmask, qdec, kdec, chunk_decay = _retention_tables()
    head = lambda off, w: (lambda b, h: (b, 0, off // w + h))
    per_head = lambda b, h: (h, 0, 0)
    cast_in, cast_out, cast_shapes = _cast_rider(
        cast_weights, cast_tiles, BATCH * RET_HEADS, lambda b, h: b * RET_HEADS + h)
    cast_bytes = sum(2 * (4 + 2) * w.size // (BATCH * RET_HEADS) for w in cast_weights)
    outs = pl.pallas_call(
        functools.partial(_retention_kernel, len(cast_weights)),
        grid=(BATCH, RET_HEADS),
        in_specs=[pl.BlockSpec(memory_space=pltpu.SMEM),
                  pl.BlockSpec((1, SEQ, dk), head(OFF_QR, dk)),
                  pl.BlockSpec((1, SEQ, dk), head(OFF_KR, dk)),
                  pl.BlockSpec((1, SEQ, dv), head(OFF_VR, dv)),
                  pl.BlockSpec((1, SEQ, dv), head(OFF_GR, dv)),
                  pl.BlockSpec((SEQ, dk), lambda b, h: (0, 0)),
                  pl.BlockSpec((SEQ, dk), lambda b, h: (0, 0)),
                  pl.BlockSpec((1, RET_CHUNK, RET_CHUNK), per_head),
                  pl.BlockSpec((1, RET_CHUNK, dk), per_head),
                  pl.BlockSpec((1, RET_CHUNK, dk), per_head),
                  pl.BlockSpec((1, dv), lambda b, h: (0, h))] + cast_in,
        out_specs=[pl.BlockSpec((1, SEQ, dv), lambda b, h: (b, 0, h))] + cast_out,
        out_shape=[jax.ShapeDtypeStruct((BATCH, SEQ, RET_V_WIDTH), BF16)] + cast_shapes,
        scratch_shapes=[pltpu.VMEM((SEQ, dk), F32), pltpu.VMEM((SEQ, dk), F32),
                        pltpu.VMEM((SEQ, dv), F32), pltpu.VMEM((N_CHUNKS, dk, dv), F32)],
        compiler_params=_params(("arbitrary", "arbitrary"), 24 * MIB + cast_bytes),
        name="retention",
    )(chunk_decay, proj3, proj3, proj3, proj3, cos, sin, dmask, qdec, kdec, ret_norm_g,
      *cast_weights)
    return outs[0], outs[1:]


def _mixer_kernel(ya_ref, yr_ref, ga_ref, gb_ref, wa_ref, wr_ref, wo_ref, x_ref, mod_ref,
                  o_ref):
    j = pl.program_id(1)

    @pl.when(j == 0)
    def _():
        o_ref[...] = x_ref[...]

    a = jnp.dot(ya_ref[...], wa_ref[j], preferred_element_type=F32)
    r = jnp.dot(yr_ref[...], wr_ref[j], preferred_element_type=F32)
    merged = (_sigmoid(ga_ref[...].astype(F32)) * a
              + _sigmoid(gb_ref[...].astype(F32)) * r)
    wo_rows = pl.ds(pl.multiple_of(j * MIX_TN, MIX_TN), MIX_TN)
    o_ref[...] += mod_ref[0, GATE1:GATE1 + 1, :] * jnp.dot(
        merged.astype(BF16), wo_ref[wo_rows, :], preferred_element_type=F32)


def _mixer(ya2d, yr2d, proj2d, w_attn_br, w_ret_br, w_o, x2d, mod3):
    tm, tn = MIX_TM, MIX_TN
    nj = D_MODEL // tn
    assert w_attn_br.shape == (nj, ATTN_WIDTH, tn) and w_ret_br.shape == (nj, RET_V_WIDTH, tn)
    assert OFF_GA % tn == 0 and OFF_GB % tn == 0
    resident = lambda shape: pl.BlockSpec(shape, lambda i, j: (0,) * len(shape),
                                          pipeline_mode=pl.Buffered(1))
    vmem = (2 * tm * (ATTN_WIDTH + RET_V_WIDTH + 2 * tn) * 2
            + (ATTN_WIDTH + RET_V_WIDTH + D_MODEL) * D_MODEL * 2
            + 4 * tm * D_MODEL * 4 + 3 * tm * tn * 4 + 4 * MIB)
    return pl.pallas_call(
        _mixer_kernel,
        grid=(TOKENS // tm, D_MODEL // tn),
        in_specs=[pl.BlockSpec((tm, ATTN_WIDTH), lambda i, j: (i, 0)),
                  pl.BlockSpec((tm, RET_V_WIDTH), lambda i, j: (i, 0)),
                  pl.BlockSpec((tm, tn), lambda i, j: (i, OFF_GA // tn + j)),
                  pl.BlockSpec((tm, tn), lambda i, j: (i, OFF_GB // tn + j)),
                  resident((nj, ATTN_WIDTH, tn)),
                  resident((nj, RET_V_WIDTH, tn)),
                  resident((D_MODEL, D_MODEL)),
                  pl.BlockSpec((tm, D_MODEL), lambda i, j: (i, 0)),
                  pl.BlockSpec((1, N_MOD, D_MODEL), lambda i, j: (i // (SEQ // tm), 0, 0))],
        out_specs=pl.BlockSpec((tm, D_MODEL), lambda i, j: (i, 0)),
        out_shape=jax.ShapeDtypeStruct((TOKENS, D_MODEL), F32),
        compiler_params=_params(("arbitrary", "arbitrary"), vmem),
        name="mixer",
    )(ya2d, yr2d, proj2d, proj2d, w_attn_br, w_ret_br, w_o, x2d, mod3)


def _ffn_kernel(x_ref, halo_ref, mod_ref, g_ref, wv_ref, wg_ref, cwv_ref, cwg_ref,
                cbv_ref, cbg_ref, wd_ref, o_ref, h_ref, u_ref):
    i = pl.program_id(0)
    j = pl.program_id(1)
    tm, halo = FFN_TM, FFN_HALO

    @pl.when(j == 0)
    def _():
        o_ref[...] = jnp.zeros_like(o_ref)
        shift = mod_ref[0, SHIFT2:SHIFT2 + 1, :]
        gain = g_ref[...] * (1.0 + mod_ref[0, SCALE2:SCALE2 + 1, :])
        seq_start = (i % (SEQ // tm)) == 0
        h_halo = _rms_mod(halo_ref[...], gain, shift)
        h_ref[0:halo, :] = jnp.where(seq_start, 0.0, h_halo).astype(BF16)

        def body(r, carry):
            src = pl.ds(pl.multiple_of(r * NORM_ROWS, NORM_ROWS), NORM_ROWS)
            dst = pl.ds(pl.multiple_of(halo + r * NORM_ROWS, halo), NORM_ROWS)
            h_ref[dst, :] = _rms_mod(x_ref[src, :], gain, shift).astype(BF16)
            return carry

        lax.fori_loop(0, tm // NORM_ROWS, body, 0, unroll=NORM_UNROLL)

    def conv(half, w_ref, cw_ref, cb_ref):
        u_ref[half] = jnp.dot(h_ref[...], w_ref[...], preferred_element_type=F32)
        y = cb_ref[...]
        for t in range(CONV_WIDTH):
            lag = CONV_WIDTH - 1 - t
            y = y + cw_ref[0, t:t + 1, :] * u_ref[half, halo - lag:halo - lag + tm, :]
        return y

    val = conv(0, wv_ref, cwv_ref, cbv_ref)
    gt = conv(1, wg_ref, cwg_ref, cbg_ref)
    kw = FFN_TN // FFN_DOWN_SPLITS
    for kh in range(FFN_DOWN_SPLITS):
        cols = slice(kh * kw, (kh + 1) * kw)
        act = (_silu(gt[:, cols]) * val[:, cols]).astype(BF16)
        o_ref[...] += jnp.dot(act, wd_ref[cols, :], preferred_element_type=F32)

    @pl.when(j == pl.num_programs(1) - 1)
    def _():
        o_ref[...] = x_ref[...] + mod_ref[0, GATE2:GATE2 + 1, :] * o_ref[...]


def _ffn(x1, mod3, norm_g, w_up, conv_w, conv_b, w_down):
    tm, tn, halo, nj = FFN_TM, FFN_TN, FFN_HALO, FFN_NJ
    vmem = (4 * tm * D_MODEL * 4 + 2 * halo * D_MODEL * 4
            + 2 * 3 * D_MODEL * tn * 2
            + (tm + halo) * D_MODEL * 2 + 2 * (tm + halo) * tn * 4
            + tm * tn * 4 + 4 * MIB)
    return pl.pallas_call(
        _ffn_kernel,
        grid=(TOKENS // tm, nj),
        in_specs=[pl.BlockSpec((tm, D_MODEL), lambda i, j: (i, 0)),
                  pl.BlockSpec((halo, D_MODEL),
                               lambda i, j: (jnp.maximum(i * (tm // halo) - 1, 0), 0)),
                  pl.BlockSpec((1, N_MOD, D_MODEL), lambda i, j: (i // (SEQ // tm), 0, 0)),
                  pl.BlockSpec((1, D_MODEL), lambda i, j: (0, 0)),
                  pl.BlockSpec((D_MODEL, tn), lambda i, j: (0, j)),
                  pl.BlockSpec((D_MODEL, tn), lambda i, j: (0, nj + j)),
                  pl.BlockSpec((1, CONV_WIDTH, tn), lambda i, j: (0, 0, j)),
                  pl.BlockSpec((1, CONV_WIDTH, tn), lambda i, j: (0, 0, nj + j)),
                  pl.BlockSpec((1, tn), lambda i, j: (0, j)),
                  pl.BlockSpec((1, tn), lambda i, j: (0, nj + j)),
                  pl.BlockSpec((tn, D_MODEL), lambda i, j: (j, 0))],
        out_specs=pl.BlockSpec((tm, D_MODEL), lambda i, j: (i, 0)),
        out_shape=jax.ShapeDtypeStruct((TOKENS, D_MODEL), F32),
        scratch_shapes=[pltpu.VMEM((tm + halo, D_MODEL), BF16),
                        pltpu.VMEM((2, tm + halo, tn), F32)],
        compiler_params=_params(("arbitrary", "arbitrary"), vmem),
        name="ffn",
    )(x1, x1, mod3, norm_g, w_up, w_up, conv_w, conv_w, conv_b, conv_b, w_down)


def kernel(x, c, w_ada, b_ada, norm1_g, w_in, q_norm_g, k_norm_g, rel_bias, ret_norm_g,
           w_attn_br, w_ret_br, w_o, norm2_g, w_up, conv_w, conv_b, w_down):
    assert x.shape == (BATCH, SEQ, D_MODEL) and w_ada.shape[0] == 1
    assert _far_bucket_is_last()
    layer = 0
    x2d = x.reshape(TOKENS, D_MODEL)

    c_pad = jnp.pad(c, ((0, 8 - BATCH), (0, 0)))
    mod = _modulation(c_pad, w_ada[layer], b_ada)[:BATCH]
    mod3 = mod.reshape(BATCH, N_MOD, D_MODEL)

    proj = _input_projection(_prenorm(x2d, mod3, norm1_g), w_in[layer])
    proj3 = proj.reshape(BATCH, SEQ, IN_WIDTH)

    bias_tiles = _relbias_tiles(rel_bias)
    yr, (w_attn_b, w_ret_b, w_o_b) = _retention(
        proj3, ret_norm_g, (w_attn_br[layer], w_ret_br[layer], w_o[layer]),
        (MIX_TN, MIX_TN, None))
    ya, (w_up_b, w_down_b) = _moba(proj3, rel_bias, q_norm_g, k_norm_g, bias_tiles,
                                   (w_up[layer], w_down[layer]), (None, None))

    x1 = _mixer(ya.reshape(TOKENS, ATTN_WIDTH), yr.reshape(TOKENS, RET_V_WIDTH), proj,
                w_attn_b, w_ret_b, w_o_b, x2d, mod3)

    out = _ffn(x1, mod3, norm2_g, w_up_b, conv_w, conv_b, w_down_b)
    return out.reshape(BATCH, SEQ, D_MODEL)
```

```python
import functools
import math

import numpy as np
import jax
import jax.numpy as jnp
from jax import lax
from jax.experimental import pallas as pl
from jax.experimental.pallas import tpu as pltpu

F32 = jnp.float32
BF16 = jnp.bfloat16

D_MODEL = 2048
BATCH = 4
SEQ = 2048
ATTN_HEADS = 8
ATTN_HEAD_DIM = 128
MOBA_BLOCK = 256
MOBA_TOPK = 3
REL_BUCKETS = 32
REL_MAX_DIST = 128
RET_HEADS = 8
RET_KEY_DIM = 128
RET_VAL_DIM = 256
RET_CHUNK = 128
ROPE_BASE = 10000.0
FFN_DIM = 5632
CONV_WIDTH = 3
EPS = 1e-6
N_MOD = 6

ATTN_WIDTH = ATTN_HEADS * ATTN_HEAD_DIM
RET_QK_WIDTH = RET_HEADS * RET_KEY_DIM
RET_V_WIDTH = RET_HEADS * RET_VAL_DIM
OFF_QA = 0
OFF_KA = OFF_QA + ATTN_WIDTH
OFF_VA = OFF_KA + ATTN_WIDTH
OFF_QR = OFF_VA + ATTN_WIDTH
OFF_KR = OFF_QR + RET_QK_WIDTH
OFF_VR = OFF_KR + RET_QK_WIDTH
OFF_GR = OFF_VR + RET_V_WIDTH
OFF_GA = OFF_GR + RET_V_WIDTH
OFF_GB = OFF_GA + D_MODEL
IN_WIDTH = OFF_GB + D_MODEL

TOKENS = BATCH * SEQ
N_BLOCKS = SEQ // MOBA_BLOCK
N_CHUNKS = SEQ // RET_CHUNK
MASK_VALUE = -1e30
LOG2E = math.log2(math.e)
MOBA_ONES_ROWS = 16
MOBA_TILE_ORDER = (0, 7, 3, 4, 1, 6, 2, 5)
MIB = 1024 * 1024

SHIFT1, SCALE1, GATE1, SHIFT2, SCALE2, GATE2 = range(N_MOD)

MOD_TN = 1024
PRENORM_TM = 512
INPROJ_TM, INPROJ_TN = 2048, 1024
MIX_TM, MIX_TN = 512, 1024
FFN_TM, FFN_TN = 1024, 512
FFN_HALO = 16
FFN_NJ = FFN_DIM // FFN_TN
FFN_DOWN_SPLITS = 2
NORM_ROWS = 16
NORM_UNROLL = 16


def _params(semantics, vmem_bytes):
    return pltpu.CompilerParams(dimension_semantics=semantics,
                                vmem_limit_bytes=int(vmem_bytes))


def _sigmoid(v):
    return 0.5 + 0.5 * jnp.tanh(0.5 * v)


def _silu(v):
    hv = 0.5 * v
    return hv + hv * jnp.tanh(hv)


def _cast_rider(weights, col_tiles, steps, step_of):
    in_specs, out_specs, out_shapes = [], [], []
    for w, tile in zip(weights, col_tiles):
        rows, cols = w.shape
        assert rows % (steps * 16) == 0, (rows, steps)
        slab = rows // steps
        in_specs.append(pl.BlockSpec((slab, cols), lambda *g: (step_of(*g), 0)))
        if tile is None:
            out_specs.append(pl.BlockSpec((slab, cols), lambda *g: (step_of(*g), 0)))
            out_shapes.append(pltpu.HBM((rows, cols), BF16))
        else:
            out_specs.append(pl.BlockSpec((cols // tile, slab, tile),
                                          lambda *g: (0, step_of(*g), 0)))
            out_shapes.append(pltpu.HBM((cols // tile, rows, tile), BF16))
    return in_specs, out_specs, out_shapes


def _run_cast_rider(src_refs, dst_refs):
    for src, dst in zip(src_refs, dst_refs):
        if len(dst.shape) == 2:
            dst[...] = src[...].astype(BF16)
        else:
            tile = dst.shape[-1]
            for t in range(dst.shape[0]):
                dst[t] = src[:, t * tile:(t + 1) * tile].astype(BF16)


def _rms_mod(x, gain, shift):
    ms = jnp.mean(x * x, axis=-1, keepdims=True)
    return (x * lax.rsqrt(ms + EPS)) * gain + shift


def _split_bf16(v):
    hi = v.astype(BF16)
    lo = (v - hi.astype(F32)).astype(BF16)
    return hi, lo


def _mod_kernel(c_ref, w_ref, b_ref, o_ref):
    rows = c_ref.shape[0]
    s_hi, s_lo = _split_bf16(_silu(c_ref[...]))
    w_hi, w_lo = _split_bf16(w_ref[...])
    both = jnp.dot(jnp.concatenate([s_hi, s_lo], axis=0), w_hi, preferred_element_type=F32)
    cross = jnp.dot(s_hi, w_lo, preferred_element_type=F32)
    o_ref[...] = (both[0:rows] + both[rows:]) + cross + b_ref[...]


def _modulation(c_pad, w_ada, b_ada):
    rows = c_pad.shape[0]
    n = w_ada.shape[1]
    return pl.pallas_call(
        _mod_kernel,
        grid=(n // MOD_TN,),
        in_specs=[pl.BlockSpec((rows, D_MODEL), lambda j: (0, 0)),
                  pl.BlockSpec((D_MODEL, MOD_TN), lambda j: (0, j)),
                  pl.BlockSpec((1, MOD_TN), lambda j: (0, j))],
        out_specs=pl.BlockSpec((rows, MOD_TN), lambda j: (0, j)),
        out_shape=jax.ShapeDtypeStruct((rows, n), F32),
        compiler_params=_params(("arbitrary",), 2 * D_MODEL * MOD_TN * 4 + 8 * MIB),
        name="mod",
    )(c_pad, w_ada, b_ada)


def _prenorm_kernel(x_ref, mod_ref, g_ref, h_ref):
    shift = mod_ref[0, SHIFT1:SHIFT1 + 1, :]
    gain = g_ref[...] * (1.0 + mod_ref[0, SCALE1:SCALE1 + 1, :])

    def body(r, carry):
        rows = pl.ds(pl.multiple_of(r * NORM_ROWS, NORM_ROWS), NORM_ROWS)
        h_ref[rows, :] = _rms_mod(x_ref[rows, :], gain, shift).astype(BF16)
        return carry

    lax.fori_loop(0, PRENORM_TM // NORM_ROWS, body, 0, unroll=NORM_UNROLL)


def _prenorm(x2d, mod3, norm_g):
    tm = PRENORM_TM
    return pl.pallas_call(
        _prenorm_kernel,
        grid=(TOKENS // tm,),
        in_specs=[pl.BlockSpec((tm, D_MODEL), lambda i: (i, 0)),
                  pl.BlockSpec((1, N_MOD, D_MODEL), lambda i: (i // (SEQ // tm), 0, 0)),
                  pl.BlockSpec((1, D_MODEL), lambda i: (0, 0))],
        out_specs=pl.BlockSpec((tm, D_MODEL), lambda i: (i, 0)),
        out_shape=pltpu.HBM((TOKENS, D_MODEL), BF16),
        compiler_params=_params(("arbitrary",), 2 * tm * D_MODEL * (4 + 2) + 4 * MIB),
        name="prenorm",
    )(x2d, mod3, norm_g)


def _inproj_kernel(h_ref, w_ref, o_ref):
    o_ref[...] = jnp.dot(h_ref[...], w_ref[...].astype(BF16),
                         preferred_element_type=F32).astype(o_ref.dtype)


def _input_projection(h2d, w_in):
    tm, tn = INPROJ_TM, INPROJ_TN
    vmem = (2 * tm * D_MODEL * 2 + 2 * D_MODEL * tn * 4 + 2 * tm * tn * 2
            + D_MODEL * tn * 2 + tm * tn * 4 + 4 * MIB)
    return pl.pallas_call(
        _inproj_kernel,
        grid=(IN_WIDTH // tn, TOKENS // tm),
        in_specs=[pl.BlockSpec((tm, D_MODEL), lambda j, i: (i, 0)),
                  pl.BlockSpec((D_MODEL, tn), lambda j, i: (0, j))],
        out_specs=pl.BlockSpec((tm, tn), lambda j, i: (i, j)),
        out_shape=jax.ShapeDtypeStruct((TOKENS, IN_WIDTH), BF16),
        compiler_params=_params(("arbitrary", "arbitrary"), vmem),
        name="inproj",
    )(h2d, w_in)


def _relbias_kernel(rb_ref, o_ref):
    h = pl.program_id(0)
    shape = (2 * MOBA_BLOCK, MOBA_BLOCK)
    key = lax.broadcasted_iota(jnp.int32, shape, 0)
    qry = lax.broadcasted_iota(jnp.int32, shape, 1)
    dist = qry - key + MOBA_BLOCK
    n = jnp.maximum(dist, 0)
    max_exact = REL_BUCKETS // 2
    nf = jnp.maximum(n, 1).astype(F32)
    large = max_exact + (jnp.log(nf / max_exact) / math.log(REL_MAX_DIST / max_exact)
                         * (REL_BUCKETS - max_exact)).astype(jnp.int32)
    large = jnp.minimum(large, REL_BUCKETS - 1)
    bucket = jnp.where(n < max_exact, n, large)
    bias = jnp.zeros(shape, F32)
    for b in range(REL_BUCKETS):
        bias = jnp.where(bucket == b, rb_ref[b, h], bias)
    o_ref[0] = jnp.where(dist >= 0, bias * LOG2E, MASK_VALUE)


def _relbias_tiles(rel_bias):
    return pl.pallas_call(
        _relbias_kernel,
        grid=(ATTN_HEADS,),
        in_specs=[pl.BlockSpec(memory_space=pltpu.SMEM)],
        out_specs=pl.BlockSpec((1, 2 * MOBA_BLOCK, MOBA_BLOCK), lambda h: (h, 0, 0)),
        out_shape=jax.ShapeDtypeStruct((ATTN_HEADS, 2 * MOBA_BLOCK, MOBA_BLOCK), F32),
        compiler_params=_params(("arbitrary",), 16 * MIB),
        name="relbias",
    )(rel_bias)


def _far_bucket_is_last():
    d = np.arange(MOBA_BLOCK + 1, SEQ, dtype=np.float32)
    max_exact = REL_BUCKETS // 2
    large = max_exact + (np.log(d / max_exact) / math.log(REL_MAX_DIST / max_exact)
                         * (REL_BUCKETS - max_exact)).astype(np.int32)
    return bool(np.all(np.minimum(large, REL_BUCKETS - 1) == REL_BUCKETS - 1))


def _moba_kernel(n_cast, rb_ref, q_ref, k_ref, v_ref, gq_ref, gk_ref, bt_ref, *refs):
    cast_src, (o_ref, *cast_dst) = refs[:n_cast], refs[n_cast:2 * n_cast + 1]
    qb_ref, kb_ref, vt_ref, *s_refs = refs[2 * n_cast + 1:]
    _run_cast_rider(cast_src, cast_dst)

    h = pl.program_id(1)
    far_bias = rb_ref[REL_BUCKETS - 1, h] * LOG2E
    blk, hd = MOBA_BLOCK, ATTN_HEAD_DIM
    nt = (((1,), (1,)), ((), ()))

    q = q_ref[0].astype(F32)
    k = k_ref[0].astype(F32)
    qn = (q * lax.rsqrt(jnp.mean(q * q, axis=-1, keepdims=True) + EPS)) * gq_ref[...]
    kn = (k * lax.rsqrt(jnp.mean(k * k, axis=-1, keepdims=True) + EPS)) * gk_ref[...]
    qb_ref[...] = (qn * (hd ** -0.5 * LOG2E)).astype(BF16)
    kb_ref[...] = kn.astype(BF16)
    vt_ref[0:hd, :] = v_ref[0].astype(F32).T.astype(BF16)
    vt_ref[hd:, :] = jnp.ones((MOBA_ONES_ROWS, SEQ), BF16)

    k_mean = jnp.concatenate(
        [jnp.sum(kn[n * blk:(n + 1) * blk], axis=0, keepdims=True) for n in range(N_BLOCKS)],
        axis=0) * (1.0 / blk)
    gate = lax.dot_general(k_mean, qn, nt, preferred_element_type=F32,
                           precision=lax.Precision.HIGHEST)
    row = lax.broadcasted_iota(jnp.int32, (N_BLOCKS, blk), 0)

    def scores(qi, s_ref):
        cols = slice(qi * blk, (qi + 1) * blk)
        nk = (qi + 1) * blk

        mask_add = None
        if qi > MOBA_TOPK:
            g = gate[:, cols]
            rank = jnp.zeros((N_BLOCKS, blk), F32)
            for m in range(qi):
                gm = g[m:m + 1, :]
                beats = (gm > g) | ((gm == g) & (row > m))
                rank = rank + jnp.where(beats, 1.0, 0.0)
            mask_add = jnp.where(rank < MOBA_TOPK, 0.0, MASK_VALUE)

        s_all = lax.dot_general(kb_ref[0:nk, :], qb_ref[cols, :], nt,
                                preferred_element_type=F32)
        m8 = None
        for n in range(qi + 1):
            s_blk = s_all[n * blk:(n + 1) * blk]
            if n == qi:
                s_blk = s_blk + bt_ref[0, blk:2 * blk, :]
            else:
                if n == qi - 1:
                    s_blk = s_blk + bt_ref[0, 0:blk, :]
                    if mask_add is not None:
                        s_blk = s_blk + mask_add[n:n + 1, :]
                elif mask_add is not None:
                    s_blk = s_blk + (mask_add[n:n + 1, :] + far_bias)
                else:
                    s_blk = s_blk + far_bias
            s_ref[n * blk:(n + 1) * blk, :] = s_blk
            b8 = jnp.max(s_blk.reshape(blk // 8, 8, blk), axis=0)
            m8 = b8 if m8 is None else jnp.maximum(m8, b8)
        return jnp.max(m8, axis=0, keepdims=True)

    def attend(qi, s_ref, m):
        cols = slice(qi * blk, (qi + 1) * blk)
        nk = (qi + 1) * blk
        p = jnp.exp2(s_ref[0:nk, :] - m).astype(BF16)
        o_aug = jnp.dot(vt_ref[:, 0:nk], p, preferred_element_type=F32)
        o_t = o_aug[0:hd] * (1.0 / o_aug[hd:hd + 1])
        o_ref[0, cols, :] = o_t.T.astype(o_ref.dtype)

    tiles = list(zip(MOBA_TILE_ORDER, s_refs))
    maxes = [scores(qi, s_ref) for qi, s_ref in tiles]
    for (qi, s_ref), m in zip(tiles, maxes):
        attend(qi, s_ref, m)


def _moba(proj3, rel_bias, q_norm_g, k_norm_g, bias_tiles, cast_weights, cast_tiles):
    hd = ATTN_HEAD_DIM
    head = lambda off: (lambda b, h: (b, 0, off // hd + h))
    cast_in, cast_out, cast_shapes = _cast_rider(
        cast_weights, cast_tiles, BATCH * ATTN_HEADS, lambda b, h: b * ATTN_HEADS + h)
    cast_bytes = sum(2 * (4 + 2) * w.size // (BATCH * ATTN_HEADS) for w in cast_weights)
    outs = pl.pallas_call(
        functools.partial(_moba_kernel, len(cast_weights)),
        grid=(BATCH, ATTN_HEADS),
        in_specs=[pl.BlockSpec(memory_space=pltpu.SMEM),
                  pl.BlockSpec((1, SEQ, hd), head(OFF_QA)),
                  pl.BlockSpec((1, SEQ, hd), head(OFF_KA)),
                  pl.BlockSpec((1, SEQ, hd), head(OFF_VA)),
                  pl.BlockSpec((1, hd), lambda b, h: (0, 0)),
                  pl.BlockSpec((1, hd), lambda b, h: (0, 0)),
                  pl.BlockSpec((1, 2 * MOBA_BLOCK, MOBA_BLOCK), lambda b, h: (h, 0, 0))]
                 + cast_in,
        out_specs=[pl.BlockSpec((1, SEQ, hd), lambda b, h: (b, 0, h))] + cast_out,
        out_shape=[jax.ShapeDtypeStruct((BATCH, SEQ, ATTN_WIDTH), BF16)] + cast_shapes,
        scratch_shapes=[pltpu.VMEM((SEQ, hd), BF16),
                        pltpu.VMEM((SEQ, hd), BF16),
                        pltpu.VMEM((hd + MOBA_ONES_ROWS, SEQ), BF16)]
                       + [pltpu.VMEM((SEQ, MOBA_BLOCK), F32)] * len(MOBA_TILE_ORDER),
        compiler_params=_params(("arbitrary", "arbitrary"), 24 * MIB + cast_bytes),
        name="moba",
    )(rel_bias, proj3, proj3, proj3, q_norm_g, k_norm_g, bias_tiles, *cast_weights)
    return outs[0], outs[1:]


def _retention_kernel(n_cast, cd_ref, q_ref, k_ref, v_ref, gr_ref, cos_ref, sin_ref,
                      dmask_ref, qdec_ref, kdec_ref, gn_ref, *refs):
    cast_src, (o_ref, *cast_dst) = refs[:n_cast], refs[n_cast:2 * n_cast + 1]
    qf_ref, kf_ref, y_ref, kv_ref = refs[2 * n_cast + 1:]
    _run_cast_rider(cast_src, cast_dst)
    h = pl.program_id(1)
    chunk_decay = cd_ref[h]
    half = RET_KEY_DIM // 2
    cos = cos_ref[...]
    sin = sin_ref[...]
    q = q_ref[0].astype(F32)
    k = k_ref[0].astype(F32)
    qf_ref[...] = q * cos + pltpu.roll(q, half, 1) * sin
    kf_ref[...] = (k * cos + pltpu.roll(k, half, 1) * sin) * (RET_KEY_DIM ** -0.5)

    nt = (((1,), (1,)), ((), ()))
    chunks = [slice(c * RET_CHUNK, (c + 1) * RET_CHUNK) for c in range(N_CHUNKS)]

    scores = [(lax.dot_general(qf_ref[rows, :].astype(BF16), kf_ref[rows, :].astype(BF16), nt,
                               preferred_element_type=F32) * dmask_ref[0]).astype(BF16)
              for rows in chunks]
    for c, rows in enumerate(chunks):
        vc = v_ref[0, rows, :]
        y_ref[rows, :] = jnp.dot(scores[c], vc, preferred_element_type=F32)
        kd_t = (kf_ref[rows, :] * kdec_ref[0]).T.astype(BF16)
        kv_ref[c] = jnp.dot(kd_t, vc, preferred_element_type=F32)

    state = jnp.zeros((RET_KEY_DIM, RET_VAL_DIM), F32)
    for c, rows in enumerate(chunks):
        y_ref[rows, :] += jnp.dot((qf_ref[rows, :] * qdec_ref[0]).astype(BF16),
                                  state.astype(BF16), preferred_element_type=F32)
        state = chunk_decay * state + kv_ref[c]

    for rows in chunks:
        y = y_ref[rows, :]
        mu = jnp.mean(y, axis=-1, keepdims=True)
        yc = y - mu
        var = jnp.mean(yc * yc, axis=-1, keepdims=True)
        yn = (yc * lax.rsqrt(var + EPS)) * gn_ref[...]
        o_ref[0, rows, :] = (yn * _silu(gr_ref[0, rows, :]).astype(F32)).astype(o_ref.dtype)


def _retention_tables():
    f32 = np.float32
    half = RET_KEY_DIM // 2
    freqs = np.power(f32(ROPE_BASE), -np.arange(half, dtype=f32) / f32(half))
    ang = np.arange(SEQ, dtype=f32)[:, None] * freqs[None, :]
    cos, sin = np.cos(ang), np.sin(ang)
    cos_full = np.concatenate([cos, cos], axis=-1)
    sin_signed = np.concatenate([-sin, sin], axis=-1)

    log_decay = np.log(f32(1.0) - np.power(f32(2.0), f32(-5.0) - np.arange(RET_HEADS, dtype=f32)))
    i = np.arange(RET_CHUNK, dtype=f32)
    diff = i[:, None] - i[None, :]
    ld = log_decay[:, None, None]
    inner_decay = np.where(diff >= 0, np.exp(ld * np.maximum(diff, f32(0.0))), f32(0.0))
    q_decay = np.exp(log_decay[:, None] * (i + f32(1.0)))
    k_decay = np.exp(log_decay[:, None] * (f32(RET_CHUNK - 1.0) - i))
    chunk_decay = np.exp(log_decay * f32(RET_CHUNK))
    bcast = lambda t: np.ascontiguousarray(
        np.broadcast_to(t[:, :, None], (RET_HEADS, RET_CHUNK, RET_KEY_DIM)))
    tables = (cos_full, sin_signed, inner_decay, bcast(q_decay), bcast(k_decay), chunk_decay)
    assert all(t.dtype == f32 for t in tables)
    return tuple(jnp.asarray(t) for t in tables)


def _retention(proj3, ret_norm_g, cast_weights, cast_tiles):
    dk, dv = RET_KEY_DIM, RET_VAL_DIM
    cos, sin, dmask, qdec, kdec, chunk_decay = _retention_tables()
    head = lambda off, w: (lambda b, h: (b, 0, off // w + h))
    per_head = lambda b, h: (h, 0, 0)
    cast_in, cast_out, cast_shapes = _cast_rider(
        cast_weights, cast_tiles, BATCH * RET_HEADS, lambda b, h: b * RET_HEADS + h)
    cast_bytes = sum(2 * (4 + 2) * w.size // (BATCH * RET_HEADS) for w in cast_weights)
    outs = pl.pallas_call(
        functools.partial(_retention_kernel, len(cast_weights)),
        grid=(BATCH, RET_HEADS),
        in_specs=[pl.BlockSpec(memory_space=pltpu.SMEM),
                  pl.BlockSpec((1, SEQ, dk), head(OFF_QR, dk)),
                  pl.BlockSpec((1, SEQ, dk), head(OFF_KR, dk)),
                  pl.BlockSpec((1, SEQ, dv), head(OFF_VR, dv)),
                  pl.BlockSpec((1, SEQ, dv), head(OFF_GR, dv)),
                  pl.BlockSpec((SEQ, dk), lambda b, h: (0, 0)),
                  pl.BlockSpec((SEQ, dk), lambda b, h: (0, 0)),
                  pl.BlockSpec((1, RET_CHUNK, RET_CHUNK), per_head),
                  pl.BlockSpec((1, RET_CHUNK, dk), per_head),
                  pl.BlockSpec((1, RET_CHUNK, dk), per_head),
                  pl.BlockSpec((1, dv), lambda b, h: (0, h))] + cast_in,
        out_specs=[pl.BlockSpec((1, SEQ, dv), lambda b, h: (b, 0, h))] + cast_out,
        out_shape=[jax.ShapeDtypeStruct((BATCH, SEQ, RET_V_WIDTH), BF16)] + cast_shapes,
        scratch_shapes=[pltpu.VMEM((SEQ, dk), F32), pltpu.VMEM((SEQ, dk), F32),
                        pltpu.VMEM((SEQ, dv), F32), pltpu.VMEM((N_CHUNKS, dk, dv), F32)],
        compiler_params=_params(("arbitrary", "arbitrary"), 24 * MIB + cast_bytes),
        name="retention",
    )(chunk_decay, proj3, proj3, proj3, proj3, cos, sin, dmask, qdec, kdec, ret_norm_g,
      *cast_weights)
    return outs[0], outs[1:]


def _mixer_kernel(ya_ref, yr_ref, ga_ref, gb_ref, wa_ref, wr_ref, wo_ref, x_ref, mod_ref,
                  o_ref):
    j = pl.program_id(1)

    @pl.when(j == 0)
    def _():
        o_ref[...] = x_ref[...]

    a = jnp.dot(ya_ref[...], wa_ref[j], preferred_element_type=F32)
    r = jnp.dot(yr_ref[...], wr_ref[j], preferred_element_type=F32)
    merged = (_sigmoid(ga_ref[...].astype(F32)) * a
              + _sigmoid(gb_ref[...].astype(F32)) * r)
    wo_rows = pl.ds(pl.multiple_of(j * MIX_TN, MIX_TN), MIX_TN)
    o_ref[...] += mod_ref[0, GATE1:GATE1 + 1, :] * jnp.dot(
        merged.astype(BF16), wo_ref[wo_rows, :], preferred_element_type=F32)


def _mixer(ya2d, yr2d, proj2d, w_attn_br, w_ret_br, w_o, x2d, mod3):
    tm, tn = MIX_TM, MIX_TN
    nj = D_MODEL // tn
    assert w_attn_br.shape == (nj, ATTN_WIDTH, tn) and w_ret_br.shape == (nj, RET_V_WIDTH, tn)
    assert OFF_GA % tn == 0 and OFF_GB % tn == 0
    resident = lambda shape: pl.BlockSpec(shape, lambda i, j: (0,) * len(shape),
                                          pipeline_mode=pl.Buffered(1))
    vmem = (2 * tm * (ATTN_WIDTH + RET_V_WIDTH + 2 * tn) * 2
            + (ATTN_WIDTH + RET_V_WIDTH + D_MODEL) * D_MODEL * 2
            + 4 * tm * D_MODEL * 4 + 3 * tm * tn * 4 + 4 * MIB)
    return pl.pallas_call(
        _mixer_kernel,
        grid=(TOKENS // tm, D_MODEL // tn),
        in_specs=[pl.BlockSpec((tm, ATTN_WIDTH), lambda i, j: (i, 0)),
                  pl.BlockSpec((tm, RET_V_WIDTH), lambda i, j: (i, 0)),
                  pl.BlockSpec((tm, tn), lambda i, j: (i, OFF_GA // tn + j)),
                  pl.BlockSpec((tm, tn), lambda i, j: (i, OFF_GB // tn + j)),
                  resident((nj, ATTN_WIDTH, tn)),
                  resident((nj, RET_V_WIDTH, tn)),
                  resident((D_MODEL, D_MODEL)),
                  pl.BlockSpec((tm, D_MODEL), lambda i, j: (i, 0)),
                  pl.BlockSpec((1, N_MOD, D_MODEL), lambda i, j: (i // (SEQ // tm), 0, 0))],
        out_specs=pl.BlockSpec((tm, D_MODEL), lambda i, j: (i, 0)),
        out_shape=jax.ShapeDtypeStruct((TOKENS, D_MODEL), F32),
        compiler_params=_params(("arbitrary", "arbitrary"), vmem),
        name="mixer",
    )(ya2d, yr2d, proj2d, proj2d, w_attn_br, w_ret_br, w_o, x2d, mod3)


def _ffn_kernel(x_ref, halo_ref, mod_ref, g_ref, wv_ref, wg_ref, cwv_ref, cwg_ref,
                cbv_ref, cbg_ref, wd_ref, o_ref, h_ref, u_ref):
    i = pl.program_id(0)
    j = pl.program_id(1)
    tm, halo = FFN_TM, FFN_HALO

    @pl.when(j == 0)
    def _():
        o_ref[...] = jnp.zeros_like(o_ref)
        shift = mod_ref[0, SHIFT2:SHIFT2 + 1, :]
        gain = g_ref[...] * (1.0 + mod_ref[0, SCALE2:SCALE2 + 1, :])
        seq_start = (i % (SEQ // tm)) == 0
        h_halo = _rms_mod(halo_ref[...], gain, shift)
        h_ref[0:halo, :] = jnp.where(seq_start, 0.0, h_halo).astype(BF16)

        def body(r, carry):
            src = pl.ds(pl.multiple_of(r * NORM_ROWS, NORM_ROWS), NORM_ROWS)
            dst = pl.ds(pl.multiple_of(halo + r * NORM_ROWS, halo), NORM_ROWS)
            h_ref[dst, :] = _rms_mod(x_ref[src, :], gain, shift).astype(BF16)
            return carry

        lax.fori_loop(0, tm // NORM_ROWS, body, 0, unroll=NORM_UNROLL)

    def conv(half, w_ref, cw_ref, cb_ref):
        u_ref[half] = jnp.dot(h_ref[...], w_ref[...], preferred_element_type=F32)
        y = cb_ref[...]
        for t in range(CONV_WIDTH):
            lag = CONV_WIDTH - 1 - t
            y = y + cw_ref[0, t:t + 1, :] * u_ref[half, halo - lag:halo - lag + tm, :]
        return y

    val = conv(0, wv_ref, cwv_ref, cbv_ref)
    gt = conv(1, wg_ref, cwg_ref, cbg_ref)
    kw = FFN_TN // FFN_DOWN_SPLITS
    for kh in range(FFN_DOWN_SPLITS):
        cols = slice(kh * kw, (kh + 1) * kw)
        act = (_silu(gt[:, cols]) * val[:, cols]).astype(BF16)
        o_ref[...] += jnp.dot(act, wd_ref[cols, :], preferred_element_type=F32)

    @pl.when(j == pl.num_programs(1) - 1)
    def _():
        o_ref[...] = x_ref[...] + mod_ref[0, GATE2:GATE2 + 1, :] * o_ref[...]


def _ffn(x1, mod3, norm_g, w_up, conv_w, conv_b, w_down):
    tm, tn, halo, nj = FFN_TM, FFN_TN, FFN_HALO, FFN_NJ
    vmem = (4 * tm * D_MODEL * 4 + 2 * halo * D_MODEL * 4
            + 2 * 3 * D_MODEL * tn * 2
            + (tm + halo) * D_MODEL * 2 + 2 * (tm + halo) * tn * 4
            + tm * tn * 4 + 4 * MIB)
    return pl.pallas_call(
        _ffn_kernel,
        grid=(TOKENS // tm, nj),
        in_specs=[pl.BlockSpec((tm, D_MODEL), lambda i, j: (i, 0)),
                  pl.BlockSpec((halo, D_MODEL),
                               lambda i, j: (jnp.maximum(i * (tm // halo) - 1, 0), 0)),
                  pl.BlockSpec((1, N_MOD, D_MODEL), lambda i, j: (i // (SEQ // tm), 0, 0)),
                  pl.BlockSpec((1, D_MODEL), lambda i, j: (0, 0)),
                  pl.BlockSpec((D_MODEL, tn), lambda i, j: (0, j)),
                  pl.BlockSpec((D_MODEL, tn), lambda i, j: (0, nj + j)),
                  pl.BlockSpec((1, CONV_WIDTH, tn), lambda i, j: (0, 0, j)),
                  pl.BlockSpec((1, CONV_WIDTH, tn), lambda i, j: (0, 0, nj + j)),
                  pl.BlockSpec((1, tn), lambda i, j: (0, j)),
                  pl.BlockSpec((1, tn), lambda i, j: (0, nj + j)),
                  pl.BlockSpec((tn, D_MODEL), lambda i, j: (j, 0))],
        out_specs=pl.BlockSpec((tm, D_MODEL), lambda i, j: (i, 0)),
        out_shape=jax.ShapeDtypeStruct((TOKENS, D_MODEL), F32),
        scratch_shapes=[pltpu.VMEM((tm + halo, D_MODEL), BF16),
                        pltpu.VMEM((2, tm + halo, tn), F32)],
        compiler_params=_params(("arbitrary", "arbitrary"), vmem),
        name="ffn",
    )(x1, x1, mod3, norm_g, w_up, w_up, conv_w, conv_w, conv_b, conv_b, w_down)


def kernel(x, c, w_ada, b_ada, norm1_g, w_in, q_norm_g, k_norm_g, rel_bias, ret_norm_g,
           w_attn_br, w_ret_br, w_o, norm2_g, w_up, conv_w, conv_b, w_down):
    assert x.shape == (BATCH, SEQ, D_MODEL) and w_ada.shape[0] == 1
    assert _far_bucket_is_last()
    layer = 0
    x2d = x.reshape(TOKENS, D_MODEL)

    c_pad = jnp.pad(c, ((0, 8 - BATCH), (0, 0)))
    mod = _modulation(c_pad, w_ada[layer], b_ada)[:BATCH]
    mod3 = mod.reshape(BATCH, N_MOD, D_MODEL)

    proj = _input_projection(_prenorm(x2d, mod3, norm1_g), w_in[layer])
    proj3 = proj.reshape(BATCH, SEQ, IN_WIDTH)

    bias_tiles = _relbias_tiles(rel_bias)
    yr, (w_attn_b, w_ret_b, w_o_b) = _retention(
        proj3, ret_norm_g, (w_attn_br[layer], w_ret_br[layer], w_o[layer]),
        (MIX_TN, MIX_TN, None))
    ya, (w_up_b, w_down_b) = _moba(proj3, rel_bias, q_norm_g, k_norm_g, bias_tiles,
                                   (w_up[layer], w_down[layer]), (None, None))

    x1 = _mixer(ya.reshape(TOKENS, ATTN_WIDTH), yr.reshape(TOKENS, RET_V_WIDTH), proj,
                w_attn_b, w_ret_b, w_o_b, x2d, mod3)

    out = _ffn(x1, mod3, norm2_g, w_up_b, conv_w, conv_b, w_down_b)
    return out.reshape(BATCH, SEQ, D_MODEL)
```

```python
import functools
import math

import numpy as np
import jax
import jax.numpy as jnp
from jax import lax
from jax.experimental import pallas as pl
from jax.experimental.pallas import tpu as pltpu

F32 = jnp.float32
BF16 = jnp.bfloat16

D_MODEL = 2048
BATCH = 4
SEQ = 2048
ATTN_HEADS = 8
ATTN_HEAD_DIM = 128
MOBA_BLOCK = 256
MOBA_TOPK = 3
REL_BUCKETS = 32
REL_MAX_DIST = 128
RET_HEADS = 8
RET_KEY_DIM = 128
RET_VAL_DIM = 256
RET_CHUNK = 128
ROPE_BASE = 10000.0
FFN_DIM = 5632
CONV_WIDTH = 3
EPS = 1e-6
N_MOD = 6

ATTN_WIDTH = ATTN_HEADS * ATTN_HEAD_DIM
RET_QK_WIDTH = RET_HEADS * RET_KEY_DIM
RET_V_WIDTH = RET_HEADS * RET_VAL_DIM
OFF_QA = 0
OFF_KA = OFF_QA + ATTN_WIDTH
OFF_VA = OFF_KA + ATTN_WIDTH
OFF_QR = OFF_VA + ATTN_WIDTH
OFF_KR = OFF_QR + RET_QK_WIDTH
OFF_VR = OFF_KR + RET_QK_WIDTH
OFF_GR = OFF_VR + RET_V_WIDTH
OFF_GA = OFF_GR + RET_V_WIDTH
OFF_GB = OFF_GA + D_MODEL
IN_WIDTH = OFF_GB + D_MODEL

TOKENS = BATCH * SEQ
N_BLOCKS = SEQ // MOBA_BLOCK
N_CHUNKS = SEQ // RET_CHUNK
MASK_VALUE = -1e30
LOG2E = math.log2(math.e)
MOBA_ONES_ROWS = 16
MOBA_TILE_ORDER = (0, 7, 3, 4, 1, 6, 2, 5)
MIB = 1024 * 1024

SHIFT1, SCALE1, GATE1, SHIFT2, SCALE2, GATE2 = range(N_MOD)

MOD_TN = 1024
PRENORM_TM = 1024
INPROJ_TM, INPROJ_TN = 2048, 1024
MIX_TM, MIX_TN = 512, 1024
FFN_TM, FFN_TN = 1024, 512
FFN_HALO = 16
FFN_NJ = FFN_DIM // FFN_TN
FFN_DOWN_SPLITS = 2
NORM_ROWS = 16
NORM_UNROLL = 16


def _params(semantics, vmem_bytes):
    return pltpu.CompilerParams(dimension_semantics=semantics,
                                vmem_limit_bytes=int(vmem_bytes))


def _sigmoid(v):
    return 0.5 + 0.5 * jnp.tanh(0.5 * v)


def _silu(v):
    hv = 0.5 * v
    return hv + hv * jnp.tanh(hv)


def _cast_rider(weights, col_tiles, steps, step_of):
    in_specs, out_specs, out_shapes = [], [], []
    for w, tile in zip(weights, col_tiles):
        rows, cols = w.shape
        assert rows % (steps * 16) == 0, (rows, steps)
        slab = rows // steps
        in_specs.append(pl.BlockSpec((slab, cols), lambda *g: (step_of(*g), 0)))
        if tile is None:
            out_specs.append(pl.BlockSpec((slab, cols), lambda *g: (step_of(*g), 0)))
            out_shapes.append(pltpu.HBM((rows, cols), BF16))
        else:
            width = tile if isinstance(tile, int) else tile[0] * tile[1]
            out_specs.append(pl.BlockSpec((cols // width, slab, width),
                                          lambda *g: (0, step_of(*g), 0)))
            out_shapes.append(pltpu.HBM((cols // width, rows, width), BF16))
    return in_specs, out_specs, out_shapes


def _run_cast_rider(col_tiles, src_refs, dst_refs):
    for tile, src, dst in zip(col_tiles, src_refs, dst_refs):
        if tile is None:
            dst[...] = src[...].astype(BF16)
        elif isinstance(tile, int):
            for t in range(dst.shape[0]):
                dst[t] = src[:, t * tile:(t + 1) * tile].astype(BF16)
        else:
            t_w, parts = tile
            part_w = src.shape[1] // parts
            for t in range(dst.shape[0]):
                for p in range(parts):
                    lo = p * part_w + t * t_w
                    dst[t, :, p * t_w:(p + 1) * t_w] = src[:, lo:lo + t_w].astype(BF16)


def _rms_mod(x, gain, shift):
    ms = jnp.mean(x * x, axis=-1, keepdims=True)
    return (x * lax.rsqrt(ms + EPS)) * gain + shift


def _split_bf16(v):
    hi = v.astype(BF16)
    lo = (v - hi.astype(F32)).astype(BF16)
    return hi, lo


def _mod_kernel(c_ref, w_ref, b_ref, o_ref):
    rows = c_ref.shape[0]
    s_hi, s_lo = _split_bf16(_silu(c_ref[...]))
    w_hi, w_lo = _split_bf16(w_ref[...])
    both = jnp.dot(jnp.concatenate([s_hi, s_lo], axis=0), w_hi, preferred_element_type=F32)
    cross = jnp.dot(s_hi, w_lo, preferred_element_type=F32)
    o_ref[...] = (both[0:rows] + both[rows:]) + cross + b_ref[...]


def _modulation(c_pad, w_ada, b_ada):
    rows = c_pad.shape[0]
    n = w_ada.shape[1]
    return pl.pallas_call(
        _mod_kernel,
        grid=(n // MOD_TN,),
        in_specs=[pl.BlockSpec((rows, D_MODEL), lambda j: (0, 0)),
                  pl.BlockSpec((D_MODEL, MOD_TN), lambda j: (0, j)),
                  pl.BlockSpec((1, MOD_TN), lambda j: (0, j))],
        out_specs=pl.BlockSpec((rows, MOD_TN), lambda j: (0, j)),
        out_shape=jax.ShapeDtypeStruct((rows, n), F32),
        compiler_params=_params(("arbitrary",), 2 * D_MODEL * MOD_TN * 4 + 8 * MIB),
        name="mod",
    )(c_pad, w_ada, b_ada)


def _prenorm_kernel(x_ref, mod_ref, g_ref, h_ref):
    shift = mod_ref[0, SHIFT1:SHIFT1 + 1, :]
    gain = g_ref[...] * (1.0 + mod_ref[0, SCALE1:SCALE1 + 1, :])

    def body(r, carry):
        rows = pl.ds(pl.multiple_of(r * NORM_ROWS, NORM_ROWS), NORM_ROWS)
        h_ref[rows, :] = _rms_mod(x_ref[rows, :], gain, shift).astype(BF16)
        return carry

    lax.fori_loop(0, PRENORM_TM // NORM_ROWS, body, 0, unroll=NORM_UNROLL)


def _prenorm(x2d, mod3, norm_g):
    tm = PRENORM_TM
    return pl.pallas_call(
        _prenorm_kernel,
        grid=(TOKENS // tm,),
        in_specs=[pl.BlockSpec((tm, D_MODEL), lambda i: (i, 0)),
                  pl.BlockSpec((1, N_MOD, D_MODEL), lambda i: (i // (SEQ // tm), 0, 0)),
                  pl.BlockSpec((1, D_MODEL), lambda i: (0, 0))],
        out_specs=pl.BlockSpec((tm, D_MODEL), lambda i: (i, 0)),
        out_shape=pltpu.HBM((TOKENS, D_MODEL), BF16),
        compiler_params=_params(("arbitrary",), 2 * tm * D_MODEL * (4 + 2) + 4 * MIB),
        name="prenorm",
    )(x2d, mod3, norm_g)


def _inproj_kernel(h_ref, w_ref, o_ref):
    o_ref[...] = jnp.dot(h_ref[...], w_ref[...].astype(BF16),
                         preferred_element_type=F32).astype(o_ref.dtype)


def _input_projection(h2d, w_in):
    tm, tn = INPROJ_TM, INPROJ_TN
    vmem = (2 * tm * D_MODEL * 2 + 2 * D_MODEL * tn * 4 + 2 * tm * tn * 2
            + D_MODEL * tn * 2 + tm * tn * 4 + 4 * MIB)
    return pl.pallas_call(
        _inproj_kernel,
        grid=(IN_WIDTH // tn, TOKENS // tm),
        in_specs=[pl.BlockSpec((tm, D_MODEL), lambda j, i: (i, 0)),
                  pl.BlockSpec((D_MODEL, tn), lambda j, i: (0, j))],
        out_specs=pl.BlockSpec((tm, tn), lambda j, i: (i, j)),
        out_shape=jax.ShapeDtypeStruct((TOKENS, IN_WIDTH), BF16),
        compiler_params=_params(("arbitrary", "arbitrary"), vmem),
        name="inproj",
    )(h2d, w_in)


def _relbias_kernel(rb_ref, o_ref):
    h = pl.program_id(0)
    shape = (2 * MOBA_BLOCK, MOBA_BLOCK)
    key = lax.broadcasted_iota(jnp.int32, shape, 0)
    qry = lax.broadcasted_iota(jnp.int32, shape, 1)
    dist = qry - key + MOBA_BLOCK
    n = jnp.maximum(dist, 0)
    max_exact = REL_BUCKETS // 2
    nf = jnp.maximum(n, 1).astype(F32)
    large = max_exact + (jnp.log(nf / max_exact) / math.log(REL_MAX_DIST / max_exact)
                         * (REL_BUCKETS - max_exact)).astype(jnp.int32)
    large = jnp.minimum(large, REL_BUCKETS - 1)
    bucket = jnp.where(n < max_exact, n, large)
    bias = jnp.zeros(shape, F32)
    for b in range(REL_BUCKETS):
        bias = jnp.where(bucket == b, rb_ref[b, h], bias)
    o_ref[0] = jnp.where(dist >= 0, bias * LOG2E, MASK_VALUE)


def _relbias_tiles(rel_bias):
    return pl.pallas_call(
        _relbias_kernel,
        grid=(ATTN_HEADS,),
        in_specs=[pl.BlockSpec(memory_space=pltpu.SMEM)],
        out_specs=pl.BlockSpec((1, 2 * MOBA_BLOCK, MOBA_BLOCK), lambda h: (h, 0, 0)),
        out_shape=jax.ShapeDtypeStruct((ATTN_HEADS, 2 * MOBA_BLOCK, MOBA_BLOCK), F32),
        compiler_params=_params(("arbitrary",), 16 * MIB),
        name="relbias",
    )(rel_bias)


def _far_bucket_is_last():
    d = np.arange(MOBA_BLOCK + 1, SEQ, dtype=np.float32)
    max_exact = REL_BUCKETS // 2
    large = max_exact + (np.log(d / max_exact) / math.log(REL_MAX_DIST / max_exact)
                         * (REL_BUCKETS - max_exact)).astype(np.int32)
    return bool(np.all(np.minimum(large, REL_BUCKETS - 1) == REL_BUCKETS - 1))


def _moba_kernel(cast_tiles, rb_ref, q_ref, k_ref, v_ref, gq_ref, gk_ref, bt_ref, *refs):
    n_cast = len(cast_tiles)
    cast_src, (o_ref, *cast_dst) = refs[:n_cast], refs[n_cast:2 * n_cast + 1]
    qb_ref, kb_ref, vt_ref, *s_refs = refs[2 * n_cast + 1:]
    _run_cast_rider(cast_tiles, cast_src, cast_dst)

    h = pl.program_id(1)
    far_bias = rb_ref[REL_BUCKETS - 1, h] * LOG2E
    blk, hd = MOBA_BLOCK, ATTN_HEAD_DIM
    nt = (((1,), (1,)), ((), ()))

    q = q_ref[0].astype(F32)
    k = k_ref[0].astype(F32)
    qn = (q * lax.rsqrt(jnp.mean(q * q, axis=-1, keepdims=True) + EPS)) * gq_ref[...]
    kn = (k * lax.rsqrt(jnp.mean(k * k, axis=-1, keepdims=True) + EPS)) * gk_ref[...]
    qb_ref[...] = (qn * (hd ** -0.5 * LOG2E)).astype(BF16)
    kb_ref[...] = kn.astype(BF16)
    vt_ref[0:hd, :] = v_ref[0].astype(F32).T.astype(BF16)
    vt_ref[hd:, :] = jnp.ones((MOBA_ONES_ROWS, SEQ), BF16)

    k_mean = jnp.concatenate(
        [jnp.sum(kn[n * blk:(n + 1) * blk], axis=0, keepdims=True) for n in range(N_BLOCKS)],
        axis=0) * (1.0 / blk)
    gate = lax.dot_general(k_mean, qn, nt, preferred_element_type=F32,
                           precision=lax.Precision.HIGHEST)
    row = lax.broadcasted_iota(jnp.int32, (N_BLOCKS, blk), 0)

    def scores(qi, s_ref):
        cols = slice(qi * blk, (qi + 1) * blk)
        nk = (qi + 1) * blk

        mask_add = None
        if qi > MOBA_TOPK:
            g = gate[:, cols]
            rank = jnp.zeros((N_BLOCKS, blk), F32)
            for m in range(qi):
                gm = g[m:m + 1, :]
                beats = (gm > g) | ((gm == g) & (row > m))
                rank = rank + jnp.where(beats, 1.0, 0.0)
            mask_add = jnp.where(rank < MOBA_TOPK, 0.0, MASK_VALUE)

        s_all = lax.dot_general(kb_ref[0:nk, :], qb_ref[cols, :], nt,
                                preferred_element_type=F32)
        m8 = None
        for n in range(qi + 1):
            s_blk = s_all[n * blk:(n + 1) * blk]
            if n == qi:
                s_blk = s_blk + bt_ref[0, blk:2 * blk, :]
            else:
                if n == qi - 1:
                    s_blk = s_blk + bt_ref[0, 0:blk, :]
                    if mask_add is not None:
                        s_blk = s_blk + mask_add[n:n + 1, :]
                elif mask_add is not None:
                    s_blk = s_blk + (mask_add[n:n + 1, :] + far_bias)
                else:
                    s_blk = s_blk + far_bias
            s_ref[n * blk:(n + 1) * blk, :] = s_blk
            b8 = jnp.max(s_blk.reshape(blk // 8, 8, blk), axis=0)
            m8 = b8 if m8 is None else jnp.maximum(m8, b8)
        return jnp.max(m8, axis=0, keepdims=True)

    def attend(qi, s_ref, m):
        cols = slice(qi * blk, (qi + 1) * blk)
        nk = (qi + 1) * blk
        p = jnp.exp2(s_ref[0:nk, :] - m).astype(BF16)
        o_aug = jnp.dot(vt_ref[:, 0:nk], p, preferred_element_type=F32)
        o_t = o_aug[0:hd] * (1.0 / o_aug[hd:hd + 1])
        o_ref[0, cols, :] = o_t.T.astype(o_ref.dtype)

    tiles = list(zip(MOBA_TILE_ORDER, s_refs))
    maxes = [scores(qi, s_ref) for qi, s_ref in tiles]
    for (qi, s_ref), m in zip(tiles, maxes):
        attend(qi, s_ref, m)


def _moba(proj3, rel_bias, q_norm_g, k_norm_g, bias_tiles, cast_weights, cast_tiles):
    hd = ATTN_HEAD_DIM
    head = lambda off: (lambda b, h: (b, 0, off // hd + h))
    cast_in, cast_out, cast_shapes = _cast_rider(
        cast_weights, cast_tiles, BATCH * ATTN_HEADS, lambda b, h: b * ATTN_HEADS + h)
    cast_bytes = sum(2 * (4 + 2) * w.size // (BATCH * ATTN_HEADS) for w in cast_weights)
    outs = pl.pallas_call(
        functools.partial(_moba_kernel, tuple(cast_tiles)),
        grid=(BATCH, ATTN_HEADS),
        in_specs=[pl.BlockSpec(memory_space=pltpu.SMEM),
                  pl.BlockSpec((1, SEQ, hd), head(OFF_QA)),
                  pl.BlockSpec((1, SEQ, hd), head(OFF_KA)),
                  pl.BlockSpec((1, SEQ, hd), head(OFF_VA)),
                  pl.BlockSpec((1, hd), lambda b, h: (0, 0)),
                  pl.BlockSpec((1, hd), lambda b, h: (0, 0)),
                  pl.BlockSpec((1, 2 * MOBA_BLOCK, MOBA_BLOCK), lambda b, h: (h, 0, 0))]
                 + cast_in,
        out_specs=[pl.BlockSpec((1, SEQ, hd), lambda b, h: (b, 0, h))] + cast_out,
        out_shape=[jax.ShapeDtypeStruct((BATCH, SEQ, ATTN_WIDTH), BF16)] + cast_shapes,
        scratch_shapes=[pltpu.VMEM((SEQ, hd), BF16),
                        pltpu.VMEM((SEQ, hd), BF16),
                        pltpu.VMEM((hd + MOBA_ONES_ROWS, SEQ), BF16)]
                       + [pltpu.VMEM((SEQ, MOBA_BLOCK), F32)] * len(MOBA_TILE_ORDER),
        compiler_params=_params(("arbitrary", "arbitrary"), 24 * MIB + cast_bytes),
        name="moba",
    )(rel_bias, proj3, proj3, proj3, q_norm_g, k_norm_g, bias_tiles, *cast_weights)
    return outs[0], outs[1:]


def _retention_kernel(cast_tiles, cd_ref, q_ref, k_ref, v_ref, gr_ref, cos_ref, sin_ref,
                      dmask_ref, qdec_ref, kdec_ref, gn_ref, *refs):
    n_cast = len(cast_tiles)
    cast_src, (o_ref, *cast_dst) = refs[:n_cast], refs[n_cast:2 * n_cast + 1]
    qf_ref, kf_ref, y_ref, kv_ref = refs[2 * n_cast + 1:]
    _run_cast_rider(cast_tiles, cast_src, cast_dst)
    h = pl.program_id(1)
    chunk_decay = cd_ref[h]
    half = RET_KEY_DIM // 2
    cos = cos_ref[...]
    sin = sin_ref[...]
    q = q_ref[0].astype(F32)
    k = k_ref[0].astype(F32)
    qf_ref[...] = q * cos + pltpu.roll(q, half, 1) * sin
    kf_ref[...] = (k * cos + pltpu.roll(k, half, 1) * sin) * (RET_KEY_DIM ** -0.5)

    nt = (((1,), (1,)), ((), ()))
    chunks = [slice(c * RET_CHUNK, (c + 1) * RET_CHUNK) for c in range(N_CHUNKS)]

    scores = [(lax.dot_general(qf_ref[rows, :].astype(BF16), kf_ref[rows, :].astype(BF16), nt,
                               preferred_element_type=F32) * dmask_ref[0]).astype(BF16)
              for rows in chunks]
    for c, rows in enumerate(chunks):
        vc = v_ref[0, rows, :]
        y_ref[rows, :] = jnp.dot(scores[c], vc, preferred_element_type=F32)
        kd_t = (kf_ref[rows, :] * kdec_ref[0]).T.astype(BF16)
        kv_ref[c] = jnp.dot(kd_t, vc, preferred_element_type=F32)

    state = jnp.zeros((RET_KEY_DIM, RET_VAL_DIM), F32)
    for c, rows in enumerate(chunks):
        y_ref[rows, :] += jnp.dot((qf_ref[rows, :] * qdec_ref[0]).astype(BF16),
                                  state.astype(BF16), preferred_element_type=F32)
        state = chunk_decay * state + kv_ref[c]

    for rows in chunks:
        y = y_ref[rows, :]
        mu = jnp.mean(y, axis=-1, keepdims=True)
        yc = y - mu
        var = jnp.mean(yc * yc, axis=-1, keepdims=True)
        yn = (yc * lax.rsqrt(var + EPS)) * gn_ref[...]
        o_ref[0, rows, :] = (yn * _silu(gr_ref[0, rows, :]).astype(F32)).astype(o_ref.dtype)


def _retention_tables():
    f32 = np.float32
    half = RET_KEY_DIM // 2
    freqs = np.power(f32(ROPE_BASE), -np.arange(half, dtype=f32) / f32(half))
    ang = np.arange(SEQ, dtype=f32)[:, None] * freqs[None, :]
    cos, sin = np.cos(ang), np.sin(ang)
    cos_full = np.concatenate([cos, cos], axis=-1)
    sin_signed = np.concatenate([-sin, sin], axis=-1)

    log_decay = np.log(f32(1.0) - np.power(f32(2.0), f32(-5.0) - np.arange(RET_HEADS, dtype=f32)))
    i = np.arange(RET_CHUNK, dtype=f32)
    diff = i[:, None] - i[None, :]
    ld = log_decay[:, None, None]
    inner_decay = np.where(diff >= 0, np.exp(ld * np.maximum(diff, f32(0.0))), f32(0.0))
    q_decay = np.exp(log_decay[:, None] * (i + f32(1.0)))
    k_decay = np.exp(log_decay[:, None] * (f32(RET_CHUNK - 1.0) - i))
    chunk_decay = np.exp(log_decay * f32(RET_CHUNK))
    bcast = lambda t: np.ascontiguousarray(
        np.broadcast_to(t[:, :, None], (RET_HEADS, RET_CHUNK, RET_KEY_DIM)))
    tables = (cos_full, sin_signed, inner_decay, bcast(q_decay), bcast(k_decay), chunk_decay)
    assert all(t.dtype == f32 for t in tables)
    return tuple(jnp.asarray(t) for t in tables)


def _retention(proj3, ret_norm_g, cast_weights, cast_tiles):
    dk, dv = RET_KEY_DIM, RET_VAL_DIM
    cos, sin, dmask, qdec, kdec, chunk_decay = _retention_tables()
    head = lambda off, w: (lambda b, h: (b, 0, off // w + h))
    per_head = lambda b, h: (h, 0, 0)
    cast_in, cast_out, cast_shapes = _cast_rider(
        cast_weights, cast_tiles, BATCH * RET_HEADS, lambda b, h: b * RET_HEADS + h)
    cast_bytes = sum(2 * (4 + 2) * w.size // (BATCH * RET_HEADS) for w in cast_weights)
    outs = pl.pallas_call(
        functools.partial(_retention_kernel, tuple(cast_tiles)),
        grid=(BATCH, RET_HEADS),
        in_specs=[pl.BlockSpec(memory_space=pltpu.SMEM),
                  pl.BlockSpec((1, SEQ, dk), head(OFF_QR, dk)),
                  pl.BlockSpec((1, SEQ, dk), head(OFF_KR, dk)),
                  pl.BlockSpec((1, SEQ, dv), head(OFF_VR, dv)),
                  pl.BlockSpec((1, SEQ, dv), head(OFF_GR, dv)),
                  pl.BlockSpec((SEQ, dk), lambda b, h: (0, 0)),
                  pl.BlockSpec((SEQ, dk), lambda b, h: (0, 0)),
                  pl.BlockSpec((1, RET_CHUNK, RET_CHUNK), per_head),
                  pl.BlockSpec((1, RET_CHUNK, dk), per_head),
                  pl.BlockSpec((1, RET_CHUNK, dk), per_head),
                  pl.BlockSpec((1, dv), lambda b, h: (0, h))] + cast_in,
        out_specs=[pl.BlockSpec((1, SEQ, dv), lambda b, h: (b, 0, h))] + cast_out,
        out_shape=[jax.ShapeDtypeStruct((BATCH, SEQ, RET_V_WIDTH), BF16)] + cast_shapes,
        scratch_shapes=[pltpu.VMEM((SEQ, dk), F32), pltpu.VMEM((SEQ, dk), F32),
                        pltpu.VMEM((SEQ, dv), F32), pltpu.VMEM((N_CHUNKS, dk, dv), F32)],
        compiler_params=_params(("arbitrary", "arbitrary"), 24 * MIB + cast_bytes),
        name="retention",
    )(chunk_decay, proj3, proj3, proj3, proj3, cos, sin, dmask, qdec, kdec, ret_norm_g,
      *cast_weights)
    return outs[0], outs[1:]


def _mixer_kernel(ya_ref, yr_ref, ga_ref, gb_ref, wa_ref, wr_ref, wo_ref, x_ref, mod_ref,
                  o_ref):
    j = pl.program_id(1)

    @pl.when(j == 0)
    def _():
        o_ref[...] = x_ref[...]

    a = jnp.dot(ya_ref[...], wa_ref[j], preferred_element_type=F32)
    r = jnp.dot(yr_ref[...], wr_ref[j], preferred_element_type=F32)
    merged = (_sigmoid(ga_ref[...].astype(F32)) * a
              + _sigmoid(gb_ref[...].astype(F32)) * r)
    wo_rows = pl.ds(pl.multiple_of(j * MIX_TN, MIX_TN), MIX_TN)
    o_ref[...] += mod_ref[0, GATE1:GATE1 + 1, :] * jnp.dot(
        merged.astype(BF16), wo_ref[wo_rows, :], preferred_element_type=F32)


def _mixer(ya2d, yr2d, proj2d, w_attn_br, w_ret_br, w_o, x2d, mod3):
    tm, tn = MIX_TM, MIX_TN
    nj = D_MODEL // tn
    assert w_attn_br.shape == (nj, ATTN_WIDTH, tn) and w_ret_br.shape == (nj, RET_V_WIDTH, tn)
    assert OFF_GA % tn == 0 and OFF_GB % tn == 0
    resident = lambda shape: pl.BlockSpec(shape, lambda i, j: (0,) * len(shape),
                                          pipeline_mode=pl.Buffered(1))
    vmem = (2 * tm * (ATTN_WIDTH + RET_V_WIDTH + 2 * tn) * 2
            + (ATTN_WIDTH + RET_V_WIDTH + D_MODEL) * D_MODEL * 2
            + 4 * tm * D_MODEL * 4 + 3 * tm * tn * 4 + 4 * MIB)
    return pl.pallas_call(
        _mixer_kernel,
        grid=(TOKENS // tm, D_MODEL // tn),
        in_specs=[pl.BlockSpec((tm, ATTN_WIDTH), lambda i, j: (i, 0)),
                  pl.BlockSpec((tm, RET_V_WIDTH), lambda i, j: (i, 0)),
                  pl.BlockSpec((tm, tn), lambda i, j: (i, OFF_GA // tn + j)),
                  pl.BlockSpec((tm, tn), lambda i, j: (i, OFF_GB // tn + j)),
                  resident((nj, ATTN_WIDTH, tn)),
                  resident((nj, RET_V_WIDTH, tn)),
                  resident((D_MODEL, D_MODEL)),
                  pl.BlockSpec((tm, D_MODEL), lambda i, j: (i, 0)),
                  pl.BlockSpec((1, N_MOD, D_MODEL), lambda i, j: (i // (SEQ // tm), 0, 0))],
        out_specs=pl.BlockSpec((tm, D_MODEL), lambda i, j: (i, 0)),
        out_shape=jax.ShapeDtypeStruct((TOKENS, D_MODEL), F32),
        compiler_params=_params(("arbitrary", "arbitrary"), vmem),
        name="mixer",
    )(ya2d, yr2d, proj2d, proj2d, w_attn_br, w_ret_br, w_o, x2d, mod3)


def _ffn_kernel(x_ref, halo_ref, mod_ref, g_ref, wu_ref, cp_ref, wd_ref,
                o_ref, h_ref, u_ref):
    i = pl.program_id(0)
    j = pl.program_id(1)
    tm, halo = FFN_TM, FFN_HALO

    @pl.when(j == 0)
    def _():
        o_ref[...] = jnp.zeros_like(o_ref)
        shift = mod_ref[0, SHIFT2:SHIFT2 + 1, :]
        gain = g_ref[...] * (1.0 + mod_ref[0, SCALE2:SCALE2 + 1, :])
        seq_start = (i % (SEQ // tm)) == 0
        h_halo = _rms_mod(halo_ref[...], gain, shift)
        h_ref[0:halo, :] = jnp.where(seq_start, 0.0, h_halo).astype(BF16)

        def body(r, carry):
            src = pl.ds(pl.multiple_of(r * NORM_ROWS, NORM_ROWS), NORM_ROWS)
            dst = pl.ds(pl.multiple_of(halo + r * NORM_ROWS, halo), NORM_ROWS)
            h_ref[dst, :] = _rms_mod(x_ref[src, :], gain, shift).astype(BF16)
            return carry

        lax.fori_loop(0, tm // NORM_ROWS, body, 0, unroll=NORM_UNROLL)

    def conv(half):
        w = wu_ref[0, :, half * FFN_TN:(half + 1) * FFN_TN]
        u_ref[half] = jnp.dot(h_ref[...], w, preferred_element_type=F32)
        y = cp_ref[half, CONV_WIDTH:CONV_WIDTH + 1, :]
        for t in range(CONV_WIDTH):
            lag = CONV_WIDTH - 1 - t
            y = y + cp_ref[half, t:t + 1, :] * u_ref[half, halo - lag:halo - lag + tm, :]
        return y

    val = conv(0)
    gt = conv(1)
    kw = FFN_TN // FFN_DOWN_SPLITS
    for kh in range(FFN_DOWN_SPLITS):
        cols = slice(kh * kw, (kh + 1) * kw)
        act = (_silu(gt[:, cols]) * val[:, cols]).astype(BF16)
        o_ref[...] += jnp.dot(act, wd_ref[cols, :], preferred_element_type=F32)

    @pl.when(j == pl.num_programs(1) - 1)
    def _():
        o_ref[...] = x_ref[...] + mod_ref[0, GATE2:GATE2 + 1, :] * o_ref[...]


def _ffn(x1, mod3, norm_g, w_up, conv_params, w_down):
    tm, tn, halo, nj = FFN_TM, FFN_TN, FFN_HALO, FFN_NJ
    assert w_up.shape == (nj, D_MODEL, 2 * tn)
    vmem = (4 * tm * D_MODEL * 4 + 2 * halo * D_MODEL * 4
            + 2 * 3 * D_MODEL * tn * 2
            + (tm + halo) * D_MODEL * 2 + 2 * (tm + halo) * tn * 4
            + tm * tn * 4 + 4 * MIB)
    return pl.pallas_call(
        _ffn_kernel,
        grid=(TOKENS // tm, nj),
        in_specs=[pl.BlockSpec((tm, D_MODEL), lambda i, j: (i, 0)),
                  pl.BlockSpec((halo, D_MODEL),
                               lambda i, j: (jnp.maximum(i * (tm // halo) - 1, 0), 0)),
                  pl.BlockSpec((1, N_MOD, D_MODEL), lambda i, j: (i // (SEQ // tm), 0, 0)),
                  pl.BlockSpec((1, D_MODEL), lambda i, j: (0, 0)),
                  pl.BlockSpec((1, D_MODEL, 2 * tn), lambda i, j: (j, 0, 0)),
                  pl.BlockSpec((2, CONV_WIDTH + 1, tn), lambda i, j: (0, 0, j)),
                  pl.BlockSpec((tn, D_MODEL), lambda i, j: (j, 0))],
        out_specs=pl.BlockSpec((tm, D_MODEL), lambda i, j: (i, 0)),
        out_shape=jax.ShapeDtypeStruct((TOKENS, D_MODEL), F32),
        scratch_shapes=[pltpu.VMEM((tm + halo, D_MODEL), BF16),
                        pltpu.VMEM((2, tm + halo, tn), F32)],
        compiler_params=_params(("arbitrary", "arbitrary"), vmem),
        name="ffn",
    )(x1, x1, mod3, norm_g, w_up, conv_params, w_down)


def kernel(x, c, w_ada, b_ada, norm1_g, w_in, q_norm_g, k_norm_g, rel_bias, ret_norm_g,
           w_attn_br, w_ret_br, w_o, norm2_g, w_up, conv_w, conv_b, w_down):
    assert x.shape == (BATCH, SEQ, D_MODEL) and w_ada.shape[0] == 1
    assert _far_bucket_is_last()
    layer = 0
    x2d = x.reshape(TOKENS, D_MODEL)

    c_pad = jnp.pad(c, ((0, 8 - BATCH), (0, 0)))
    mod = _modulation(c_pad, w_ada[layer], b_ada)[:BATCH]
    mod3 = mod.reshape(BATCH, N_MOD, D_MODEL)

    proj = _input_projection(_prenorm(x2d, mod3, norm1_g), w_in[layer])
    proj3 = proj.reshape(BATCH, SEQ, IN_WIDTH)

    bias_tiles = _relbias_tiles(rel_bias)
    yr, (w_attn_b, w_ret_b, w_o_b) = _retention(
        proj3, ret_norm_g, (w_attn_br[layer], w_ret_br[layer], w_o[layer]),
        (MIX_TN, MIX_TN, None))
    ya, (w_up_b, w_down_b) = _moba(proj3, rel_bias, q_norm_g, k_norm_g, bias_tiles,
                                   (w_up[layer], w_down[layer]), ((FFN_TN, 2), None))

    x1 = _mixer(ya.reshape(TOKENS, ATTN_WIDTH), yr.reshape(TOKENS, RET_V_WIDTH), proj,
                w_attn_b, w_ret_b, w_o_b, x2d, mod3)

    conv_params = jnp.concatenate([conv_w[layer], conv_b], axis=0).reshape(
        CONV_WIDTH + 1, 2, FFN_DIM).transpose(1, 0, 2)
    out = _ffn(x1, mod3, norm2_g, w_up_b, conv_params, w_down_b)
    return out.reshape(BATCH, SEQ, D_MODEL)
```

```python
import functools
import math

import numpy as np
import jax
import jax.numpy as jnp
from jax import lax
from jax.experimental import pallas as pl
from jax.experimental.pallas import tpu as pltpu

F32 = jnp.float32
BF16 = jnp.bfloat16

D_MODEL = 2048
BATCH = 4
SEQ = 2048
ATTN_HEADS = 8
ATTN_HEAD_DIM = 128
MOBA_BLOCK = 256
MOBA_TOPK = 3
REL_BUCKETS = 32
REL_MAX_DIST = 128
RET_HEADS = 8
RET_KEY_DIM = 128
RET_VAL_DIM = 256
RET_CHUNK = 128
ROPE_BASE = 10000.0
FFN_DIM = 5632
CONV_WIDTH = 3
EPS = 1e-6
N_MOD = 6

ATTN_WIDTH = ATTN_HEADS * ATTN_HEAD_DIM
RET_QK_WIDTH = RET_HEADS * RET_KEY_DIM
RET_V_WIDTH = RET_HEADS * RET_VAL_DIM
OFF_QA = 0
OFF_KA = OFF_QA + ATTN_WIDTH
OFF_VA = OFF_KA + ATTN_WIDTH
OFF_QR = OFF_VA + ATTN_WIDTH
OFF_KR = OFF_QR + RET_QK_WIDTH
OFF_VR = OFF_KR + RET_QK_WIDTH
OFF_GR = OFF_VR + RET_V_WIDTH
OFF_GA = OFF_GR + RET_V_WIDTH
OFF_GB = OFF_GA + D_MODEL
IN_WIDTH = OFF_GB + D_MODEL

TOKENS = BATCH * SEQ
N_BLOCKS = SEQ // MOBA_BLOCK
N_CHUNKS = SEQ // RET_CHUNK
MASK_VALUE = -1e30
LOG2E = math.log2(math.e)
MOBA_ONES_ROWS = 16
MOBA_TILE_ORDER = (0, 7, 3, 4, 1, 6, 2, 5)
MIB = 1024 * 1024

SHIFT1, SCALE1, GATE1, SHIFT2, SCALE2, GATE2 = range(N_MOD)

MOD_TN = 1024
PRENORM_TM = 1024
INPROJ_TM, INPROJ_TN = 2048, 1024
MIX_TM, MIX_TN = 512, 1024
FFN_TM, FFN_TN = 1024, 512
FFN_HALO = 16
FFN_NJ = FFN_DIM // FFN_TN
FFN_DOWN_SPLITS = 2
NORM_ROWS = 16
NORM_UNROLL = 16


def _params(semantics, vmem_bytes):
    return pltpu.CompilerParams(dimension_semantics=semantics,
                                vmem_limit_bytes=int(vmem_bytes))


def _sigmoid(v):
    return 0.5 + 0.5 * jnp.tanh(0.5 * v)


def _silu(v):
    hv = 0.5 * v
    return hv + hv * jnp.tanh(hv)


def _cast_rider(weights, col_tiles, steps, step_of):
    in_specs, out_specs, out_shapes = [], [], []
    for w, tile in zip(weights, col_tiles):
        rows, cols = w.shape
        assert rows % (steps * 16) == 0, (rows, steps)
        slab = rows // steps
        in_specs.append(pl.BlockSpec((slab, cols), lambda *g: (step_of(*g), 0)))
        if tile is None:
            out_specs.append(pl.BlockSpec((slab, cols), lambda *g: (step_of(*g), 0)))
            out_shapes.append(pltpu.HBM((rows, cols), BF16))
        else:
            width = tile if isinstance(tile, int) else tile[0] * tile[1]
            out_specs.append(pl.BlockSpec((cols // width, slab, width),
                                          lambda *g: (0, step_of(*g), 0)))
            out_shapes.append(pltpu.HBM((cols // width, rows, width), BF16))
    return in_specs, out_specs, out_shapes


def _run_cast_rider(col_tiles, src_refs, dst_refs):
    for tile, src, dst in zip(col_tiles, src_refs, dst_refs):
        if tile is None:
            dst[...] = src[...].astype(BF16)
        elif isinstance(tile, int):
            for t in range(dst.shape[0]):
                dst[t] = src[:, t * tile:(t + 1) * tile].astype(BF16)
        else:
            t_w, parts = tile
            part_w = src.shape[1] // parts
            for t in range(dst.shape[0]):
                for p in range(parts):
                    lo = p * part_w + t * t_w
                    dst[t, :, p * t_w:(p + 1) * t_w] = src[:, lo:lo + t_w].astype(BF16)


def _rms_mod(x, gain, shift):
    ms = jnp.mean(x * x, axis=-1, keepdims=True)
    return (x * lax.rsqrt(ms + EPS)) * gain + shift


def _split_bf16(v):
    hi = v.astype(BF16)
    lo = (v - hi.astype(F32)).astype(BF16)
    return hi, lo


def _mod_kernel(c_ref, w_ref, b_ref, o_ref):
    rows = c_ref.shape[0]
    s_hi, s_lo = _split_bf16(_silu(c_ref[...]))
    w_hi, w_lo = _split_bf16(w_ref[...])
    both = jnp.dot(jnp.concatenate([s_hi, s_lo], axis=0), w_hi, preferred_element_type=F32)
    cross = jnp.dot(s_hi, w_lo, preferred_element_type=F32)
    o_ref[...] = (both[0:rows] + both[rows:]) + cross + b_ref[...]


def _modulation(c_pad, w_ada, b_ada):
    rows = c_pad.shape[0]
    n = w_ada.shape[1]
    return pl.pallas_call(
        _mod_kernel,
        grid=(n // MOD_TN,),
        in_specs=[pl.BlockSpec((rows, D_MODEL), lambda j: (0, 0)),
                  pl.BlockSpec((D_MODEL, MOD_TN), lambda j: (0, j)),
                  pl.BlockSpec((1, MOD_TN), lambda j: (0, j))],
        out_specs=pl.BlockSpec((rows, MOD_TN), lambda j: (0, j)),
        out_shape=jax.ShapeDtypeStruct((rows, n), F32),
        compiler_params=_params(("arbitrary",), 2 * D_MODEL * MOD_TN * 4 + 8 * MIB),
        name="mod",
    )(c_pad, w_ada, b_ada)


def _prenorm_kernel(x_ref, mod_ref, g_ref, h_ref):
    shift = mod_ref[0, SHIFT1:SHIFT1 + 1, :]
    gain = g_ref[...] * (1.0 + mod_ref[0, SCALE1:SCALE1 + 1, :])

    def body(r, carry):
        rows = pl.ds(pl.multiple_of(r * NORM_ROWS, NORM_ROWS), NORM_ROWS)
        h_ref[rows, :] = _rms_mod(x_ref[rows, :], gain, shift).astype(BF16)
        return carry

    lax.fori_loop(0, PRENORM_TM // NORM_ROWS, body, 0, unroll=NORM_UNROLL)


def _prenorm(x2d, mod3, norm_g):
    tm = PRENORM_TM
    return pl.pallas_call(
        _prenorm_kernel,
        grid=(TOKENS // tm,),
        in_specs=[pl.BlockSpec((tm, D_MODEL), lambda i: (i, 0)),
                  pl.BlockSpec((1, N_MOD, D_MODEL), lambda i: (i // (SEQ // tm), 0, 0)),
                  pl.BlockSpec((1, D_MODEL), lambda i: (0, 0))],
        out_specs=pl.BlockSpec((tm, D_MODEL), lambda i: (i, 0)),
        out_shape=pltpu.HBM((TOKENS, D_MODEL), BF16),
        compiler_params=_params(("arbitrary",), 2 * tm * D_MODEL * (4 + 2) + 4 * MIB),
        name="prenorm",
    )(x2d, mod3, norm_g)


def _inproj_kernel(h_ref, w_ref, o_ref):
    o_ref[...] = jnp.dot(h_ref[...], w_ref[...].astype(BF16),
                         preferred_element_type=F32).astype(o_ref.dtype)


def _input_projection(h2d, w_in):
    tm, tn = INPROJ_TM, INPROJ_TN
    vmem = (2 * tm * D_MODEL * 2 + 2 * D_MODEL * tn * 4 + 2 * tm * tn * 2
            + D_MODEL * tn * 2 + tm * tn * 4 + 4 * MIB)
    return pl.pallas_call(
        _inproj_kernel,
        grid=(IN_WIDTH // tn, TOKENS // tm),
        in_specs=[pl.BlockSpec((tm, D_MODEL), lambda j, i: (i, 0)),
                  pl.BlockSpec((D_MODEL, tn), lambda j, i: (0, j))],
        out_specs=pl.BlockSpec((tm, tn), lambda j, i: (i, j)),
        out_shape=jax.ShapeDtypeStruct((TOKENS, IN_WIDTH), BF16),
        compiler_params=_params(("arbitrary", "arbitrary"), vmem),
        name="inproj",
    )(h2d, w_in)


def _relbias_kernel(rb_ref, o_ref):
    h = pl.program_id(0)
    shape = (2 * MOBA_BLOCK, MOBA_BLOCK)
    key = lax.broadcasted_iota(jnp.int32, shape, 0)
    qry = lax.broadcasted_iota(jnp.int32, shape, 1)
    dist = qry - key + MOBA_BLOCK
    n = jnp.maximum(dist, 0)
    max_exact = REL_BUCKETS // 2
    nf = jnp.maximum(n, 1).astype(F32)
    large = max_exact + (jnp.log(nf / max_exact) / math.log(REL_MAX_DIST / max_exact)
                         * (REL_BUCKETS - max_exact)).astype(jnp.int32)
    large = jnp.minimum(large, REL_BUCKETS - 1)
    bucket = jnp.where(n < max_exact, n, large)
    bias = jnp.zeros(shape, F32)
    for b in range(REL_BUCKETS):
        bias = jnp.where(bucket == b, rb_ref[b, h], bias)
    o_ref[0] = jnp.where(dist >= 0, bias * LOG2E, MASK_VALUE)


def _relbias_tiles(rel_bias):
    return pl.pallas_call(
        _relbias_kernel,
        grid=(ATTN_HEADS,),
        in_specs=[pl.BlockSpec(memory_space=pltpu.SMEM)],
        out_specs=pl.BlockSpec((1, 2 * MOBA_BLOCK, MOBA_BLOCK), lambda h: (h, 0, 0)),
        out_shape=jax.ShapeDtypeStruct((ATTN_HEADS, 2 * MOBA_BLOCK, MOBA_BLOCK), F32),
        compiler_params=_params(("arbitrary",), 16 * MIB),
        name="relbias",
    )(rel_bias)


def _far_bucket_is_last():
    d = np.arange(MOBA_BLOCK + 1, SEQ, dtype=np.float32)
    max_exact = REL_BUCKETS // 2
    large = max_exact + (np.log(d / max_exact) / math.log(REL_MAX_DIST / max_exact)
                         * (REL_BUCKETS - max_exact)).astype(np.int32)
    return bool(np.all(np.minimum(large, REL_BUCKETS - 1) == REL_BUCKETS - 1))


def _moba_kernel(cast_tiles, rb_ref, q_ref, k_ref, v_ref, gq_ref, gk_ref, bt_ref, *refs):
    n_cast = len(cast_tiles)
    cast_src, (o_ref, *cast_dst) = refs[:n_cast], refs[n_cast:2 * n_cast + 1]
    qb_ref, kb_ref, vt_ref, *s_refs = refs[2 * n_cast + 1:]
    _run_cast_rider(cast_tiles, cast_src, cast_dst)

    h = pl.program_id(1)
    far_bias = rb_ref[REL_BUCKETS - 1, h] * LOG2E
    blk, hd = MOBA_BLOCK, ATTN_HEAD_DIM
    nt = (((1,), (1,)), ((), ()))

    q = q_ref[0].astype(F32)
    k = k_ref[0].astype(F32)
    qn = (q * lax.rsqrt(jnp.mean(q * q, axis=-1, keepdims=True) + EPS)) * gq_ref[...]
    kn = (k * lax.rsqrt(jnp.mean(k * k, axis=-1, keepdims=True) + EPS)) * gk_ref[...]
    qb_ref[...] = (qn * (hd ** -0.5 * LOG2E)).astype(BF16)
    kb_ref[...] = kn.astype(BF16)
    vt_ref[0:hd, :] = v_ref[0].astype(F32).T.astype(BF16)
    vt_ref[hd:, :] = jnp.ones((MOBA_ONES_ROWS, SEQ), BF16)

    k_mean = jnp.concatenate(
        [jnp.sum(kn[n * blk:(n + 1) * blk], axis=0, keepdims=True) for n in range(N_BLOCKS)],
        axis=0) * (1.0 / blk)
    gate = lax.dot_general(k_mean, qn, nt, preferred_element_type=F32,
                           precision=lax.Precision.HIGHEST)
    row = lax.broadcasted_iota(jnp.int32, (N_BLOCKS, blk), 0)

    def scores(qi, s_ref):
        cols = slice(qi * blk, (qi + 1) * blk)
        nk = (qi + 1) * blk

        mask_add = None
        if qi > MOBA_TOPK:
            g = gate[:, cols]
            rank = jnp.zeros((N_BLOCKS, blk), F32)
            for m in range(qi):
                gm = g[m:m + 1, :]
                beats = (gm > g) | ((gm == g) & (row > m))
                rank = rank + jnp.where(beats, 1.0, 0.0)
            mask_add = jnp.where(rank < MOBA_TOPK, 0.0, MASK_VALUE)

        s_all = lax.dot_general(kb_ref[0:nk, :], qb_ref[cols, :], nt,
                                preferred_element_type=F32)
        m8 = None
        for n in range(qi + 1):
            s_blk = s_all[n * blk:(n + 1) * blk]
            if n == qi:
                s_blk = s_blk + bt_ref[0, blk:2 * blk, :]
            else:
                if n == qi - 1:
                    s_blk = s_blk + bt_ref[0, 0:blk, :]
                    if mask_add is not None:
                        s_blk = s_blk + mask_add[n:n + 1, :]
                elif mask_add is not None:
                    s_blk = s_blk + (mask_add[n:n + 1, :] + far_bias)
                else:
                    s_blk = s_blk + far_bias
            s_ref[n * blk:(n + 1) * blk, :] = s_blk
            b8 = jnp.max(s_blk.reshape(blk // 8, 8, blk), axis=0)
            m8 = b8 if m8 is None else jnp.maximum(m8, b8)
        return jnp.max(m8, axis=0, keepdims=True)

    def attend(qi, s_ref, m):
        cols = slice(qi * blk, (qi + 1) * blk)
        nk = (qi + 1) * blk
        p = jnp.exp2(s_ref[0:nk, :] - m).astype(BF16)
        o_aug = jnp.dot(vt_ref[:, 0:nk], p, preferred_element_type=F32)
        o_t = o_aug[0:hd] * (1.0 / o_aug[hd:hd + 1])
        o_ref[0, cols, :] = o_t.T.astype(o_ref.dtype)

    tiles = list(zip(MOBA_TILE_ORDER, s_refs))
    maxes = [scores(qi, s_ref) for qi, s_ref in tiles]
    for (qi, s_ref), m in zip(tiles, maxes):
        attend(qi, s_ref, m)


def _moba(proj3, rel_bias, q_norm_g, k_norm_g, bias_tiles, cast_weights, cast_tiles):
    hd = ATTN_HEAD_DIM
    head = lambda off: (lambda b, h: (b, 0, off // hd + h))
    cast_in, cast_out, cast_shapes = _cast_rider(
        cast_weights, cast_tiles, BATCH * ATTN_HEADS, lambda b, h: b * ATTN_HEADS + h)
    cast_bytes = sum(2 * (4 + 2) * w.size // (BATCH * ATTN_HEADS) for w in cast_weights)
    outs = pl.pallas_call(
        functools.partial(_moba_kernel, tuple(cast_tiles)),
        grid=(BATCH, ATTN_HEADS),
        in_specs=[pl.BlockSpec(memory_space=pltpu.SMEM),
                  pl.BlockSpec((1, SEQ, hd), head(OFF_QA)),
                  pl.BlockSpec((1, SEQ, hd), head(OFF_KA)),
                  pl.BlockSpec((1, SEQ, hd), head(OFF_VA)),
                  pl.BlockSpec((1, hd), lambda b, h: (0, 0)),
                  pl.BlockSpec((1, hd), lambda b, h: (0, 0)),
                  pl.BlockSpec((1, 2 * MOBA_BLOCK, MOBA_BLOCK), lambda b, h: (h, 0, 0))]
                 + cast_in,
        out_specs=[pl.BlockSpec((1, SEQ, hd), lambda b, h: (b, 0, h))] + cast_out,
        out_shape=[jax.ShapeDtypeStruct((BATCH, SEQ, ATTN_WIDTH), BF16)] + cast_shapes,
        scratch_shapes=[pltpu.VMEM((SEQ, hd), BF16),
                        pltpu.VMEM((SEQ, hd), BF16),
                        pltpu.VMEM((hd + MOBA_ONES_ROWS, SEQ), BF16)]
                       + [pltpu.VMEM((SEQ, MOBA_BLOCK), F32)] * len(MOBA_TILE_ORDER),
        compiler_params=_params(("arbitrary", "arbitrary"), 24 * MIB + cast_bytes),
        name="moba",
    )(rel_bias, proj3, proj3, proj3, q_norm_g, k_norm_g, bias_tiles, *cast_weights)
    return outs[0], outs[1:]


def _retention_kernel(cast_tiles, cd_ref, q_ref, k_ref, v_ref, gr_ref, cos_ref, sin_ref,
                      dmask_ref, qdec_ref, kdec_ref, gn_ref, *refs):
    n_cast = len(cast_tiles)
    cast_src, (o_ref, *cast_dst) = refs[:n_cast], refs[n_cast:2 * n_cast + 1]
    qf_ref, kf_ref, y_ref, kv_ref = refs[2 * n_cast + 1:]
    _run_cast_rider(cast_tiles, cast_src, cast_dst)
    h = pl.program_id(1)
    chunk_decay = cd_ref[h]
    half = RET_KEY_DIM // 2
    cos = cos_ref[...]
    sin = sin_ref[...]
    q = q_ref[0].astype(F32)
    k = k_ref[0].astype(F32)
    qf_ref[...] = q * cos + pltpu.roll(q, half, 1) * sin
    kf_ref[...] = (k * cos + pltpu.roll(k, half, 1) * sin) * (RET_KEY_DIM ** -0.5)

    nt = (((1,), (1,)), ((), ()))
    chunks = [slice(c * RET_CHUNK, (c + 1) * RET_CHUNK) for c in range(N_CHUNKS)]

    scores = [(lax.dot_general(qf_ref[rows, :].astype(BF16), kf_ref[rows, :].astype(BF16), nt,
                               preferred_element_type=F32) * dmask_ref[0]).astype(BF16)
              for rows in chunks]
    for c, rows in enumerate(chunks):
        vc = v_ref[0, rows, :]
        y_ref[rows, :] = jnp.dot(scores[c], vc, preferred_element_type=F32)
        kd_t = (kf_ref[rows, :] * kdec_ref[0]).T.astype(BF16)
        kv_ref[c] = jnp.dot(kd_t, vc, preferred_element_type=F32)

    state = jnp.zeros((RET_KEY_DIM, RET_VAL_DIM), F32)
    for c, rows in enumerate(chunks):
        y_ref[rows, :] += jnp.dot((qf_ref[rows, :] * qdec_ref[0]).astype(BF16),
                                  state.astype(BF16), preferred_element_type=F32)
        state = chunk_decay * state + kv_ref[c]

    for rows in chunks:
        y = y_ref[rows, :]
        mu = jnp.mean(y, axis=-1, keepdims=True)
        yc = y - mu
        var = jnp.mean(yc * yc, axis=-1, keepdims=True)
        yn = (yc * lax.rsqrt(var + EPS)) * gn_ref[...]
        o_ref[0, rows, :] = (yn * _silu(gr_ref[0, rows, :]).astype(F32)).astype(o_ref.dtype)


def _retention_tables():
    f32 = np.float32
    half = RET_KEY_DIM // 2
    freqs = np.power(f32(ROPE_BASE), -np.arange(half, dtype=f32) / f32(half))
    ang = np.arange(SEQ, dtype=f32)[:, None] * freqs[None, :]
    cos, sin = np.cos(ang), np.sin(ang)
    cos_full = np.concatenate([cos, cos], axis=-1)
    sin_signed = np.concatenate([-sin, sin], axis=-1)

    log_decay = np.log(f32(1.0) - np.power(f32(2.0), f32(-5.0) - np.arange(RET_HEADS, dtype=f32)))
    i = np.arange(RET_CHUNK, dtype=f32)
    diff = i[:, None] - i[None, :]
    ld = log_decay[:, None, None]
    inner_decay = np.where(diff >= 0, np.exp(ld * np.maximum(diff, f32(0.0))), f32(0.0))
    q_decay = np.exp(log_decay[:, None] * (i + f32(1.0)))
    k_decay = np.exp(log_decay[:, None] * (f32(RET_CHUNK - 1.0) - i))
    chunk_decay = np.exp(log_decay * f32(RET_CHUNK))
    bcast = lambda t: np.ascontiguousarray(
        np.broadcast_to(t[:, :, None], (RET_HEADS, RET_CHUNK, RET_KEY_DIM)))
    tables = (cos_full, sin_signed, inner_decay, bcast(q_decay), bcast(k_decay), chunk_decay)
    assert all(t.dtype == f32 for t in tables)
    return tuple(jnp.asarray(t) for t in tables)


def _retention(proj3, ret_norm_g, cast_weights, cast_tiles):
    dk, dv = RET_KEY_DIM, RET_VAL_DIM
    cos, sin, dmask, qdec, kdec, chunk_decay = _retention_tables()
    head = lambda off, w: (lambda b, h: (b, 0, off // w + h))
    per_head = lambda b, h: (h, 0, 0)
    cast_in, cast_out, cast_shapes = _cast_rider(
        cast_weights, cast_tiles, BATCH * RET_HEADS, lambda b, h: b * RET_HEADS + h)
    cast_bytes = sum(2 * (4 + 2) * w.size // (BATCH * RET_HEADS) for w in cast_weights)
    outs = pl.pallas_call(
        functools.partial(_retention_kernel, tuple(cast_tiles)),
        grid=(BATCH, RET_HEADS),
        in_specs=[pl.BlockSpec(memory_space=pltpu.SMEM),
                  pl.BlockSpec((1, SEQ, dk), head(OFF_QR, dk)),
                  pl.BlockSpec((1, SEQ, dk), head(OFF_KR, dk)),
                  pl.BlockSpec((1, SEQ, dv), head(OFF_VR, dv)),
                  pl.BlockSpec((1, SEQ, dv), head(OFF_GR, dv)),
                  pl.BlockSpec((SEQ, dk), lambda b, h: (0, 0)),
                  pl.BlockSpec((SEQ, dk), lambda b, h: (0, 0)),
                  pl.BlockSpec((1, RET_CHUNK, RET_CHUNK), per_head),
                  pl.BlockSpec((1, RET_CHUNK, dk), per_head),
                  pl.BlockSpec((1, RET_CHUNK, dk), per_head),
                  pl.BlockSpec((1, dv), lambda b, h: (0, h))] + cast_in,
        out_specs=[pl.BlockSpec((1, SEQ, dv), lambda b, h: (b, 0, h))] + cast_out,
        out_shape=[jax.ShapeDtypeStruct((BATCH, SEQ, RET_V_WIDTH), BF16)] + cast_shapes,
        scratch_shapes=[pltpu.VMEM((SEQ, dk), F32), pltpu.VMEM((SEQ, dk), F32),
                        pltpu.VMEM((SEQ, dv), F32), pltpu.VMEM((N_CHUNKS, dk, dv), F32)],
        compiler_params=_params(("arbitrary", "arbitrary"), 24 * MIB + cast_bytes),
        name="retention",
    )(chunk_decay, proj3, proj3, proj3, proj3, cos, sin, dmask, qdec, kdec, ret_norm_g,
      *cast_weights)
    return outs[0], outs[1:]


def _mixer_kernel(ya_ref, yr_ref, ga_ref, gb_ref, wa_ref, wr_ref, wo_ref, x_ref, mod_ref,
                  o_ref):
    j = pl.program_id(1)

    @pl.when(j == 0)
    def _():
        o_ref[...] = x_ref[...]

    a = jnp.dot(ya_ref[...], wa_ref[j], preferred_element_type=F32)
    r = jnp.dot(yr_ref[...], wr_ref[j], preferred_element_type=F32)
    merged = (_sigmoid(ga_ref[...].astype(F32)) * a
              + _sigmoid(gb_ref[...].astype(F32)) * r)
    wo_rows = pl.ds(pl.multiple_of(j * MIX_TN, MIX_TN), MIX_TN)
    o_ref[...] += mod_ref[0, GATE1:GATE1 + 1, :] * jnp.dot(
        merged.astype(BF16), wo_ref[wo_rows, :], preferred_element_type=F32)


def _mixer(ya2d, yr2d, proj2d, w_attn_br, w_ret_br, w_o, x2d, mod3):
    tm, tn = MIX_TM, MIX_TN
    nj = D_MODEL // tn
    assert w_attn_br.shape == (nj, ATTN_WIDTH, tn) and w_ret_br.shape == (nj, RET_V_WIDTH, tn)
    assert OFF_GA % tn == 0 and OFF_GB % tn == 0
    resident = lambda shape: pl.BlockSpec(shape, lambda i, j: (0,) * len(shape),
                                          pipeline_mode=pl.Buffered(1))
    vmem = (2 * tm * (ATTN_WIDTH + RET_V_WIDTH + 2 * tn) * 2
            + (ATTN_WIDTH + RET_V_WIDTH + D_MODEL) * D_MODEL * 2
            + 4 * tm * D_MODEL * 4 + 3 * tm * tn * 4 + 4 * MIB)
    return pl.pallas_call(
        _mixer_kernel,
        grid=(TOKENS // tm, D_MODEL // tn),
        in_specs=[pl.BlockSpec((tm, ATTN_WIDTH), lambda i, j: (i, 0)),
                  pl.BlockSpec((tm, RET_V_WIDTH), lambda i, j: (i, 0)),
                  pl.BlockSpec((tm, tn), lambda i, j: (i, OFF_GA // tn + j)),
                  pl.BlockSpec((tm, tn), lambda i, j: (i, OFF_GB // tn + j)),
                  resident((nj, ATTN_WIDTH, tn)),
                  resident((nj, RET_V_WIDTH, tn)),
                  resident((D_MODEL, D_MODEL)),
                  pl.BlockSpec((tm, D_MODEL), lambda i, j: (i, 0)),
                  pl.BlockSpec((1, N_MOD, D_MODEL), lambda i, j: (i // (SEQ // tm), 0, 0))],
        out_specs=pl.BlockSpec((tm, D_MODEL), lambda i, j: (i, 0)),
        out_shape=jax.ShapeDtypeStruct((TOKENS, D_MODEL), F32),
        compiler_params=_params(("arbitrary", "arbitrary"), vmem),
        name="mixer",
    )(ya2d, yr2d, proj2d, proj2d, w_attn_br, w_ret_br, w_o, x2d, mod3)


def _ffn_kernel(x_ref, halo_ref, mod_ref, g_ref, wu_ref, cp_ref, wd_ref,
                o_ref, h_ref, u_ref):
    i = pl.program_id(0)
    j = pl.program_id(1)
    tm, halo = FFN_TM, FFN_HALO

    @pl.when(j == 0)
    def _():
        o_ref[...] = jnp.zeros_like(o_ref)
        shift = mod_ref[0, SHIFT2:SHIFT2 + 1, :]
        gain = g_ref[...] * (1.0 + mod_ref[0, SCALE2:SCALE2 + 1, :])
        seq_start = (i % (SEQ // tm)) == 0
        h_halo = _rms_mod(halo_ref[...], gain, shift)
        h_ref[0:halo, :] = jnp.where(seq_start, 0.0, h_halo).astype(BF16)

        def body(r, carry):
            src = pl.ds(pl.multiple_of(r * NORM_ROWS, NORM_ROWS), NORM_ROWS)
            dst = pl.ds(pl.multiple_of(halo + r * NORM_ROWS, halo), NORM_ROWS)
            h_ref[dst, :] = _rms_mod(x_ref[src, :], gain, shift).astype(BF16)
            return carry

        lax.fori_loop(0, tm // NORM_ROWS, body, 0, unroll=NORM_UNROLL)

    def conv(half):
        w = wu_ref[0, :, half * FFN_TN:(half + 1) * FFN_TN]
        u_ref[half] = jnp.dot(h_ref[...], w, preferred_element_type=F32)
        y = cp_ref[half, CONV_WIDTH:CONV_WIDTH + 1, :]
        for t in range(CONV_WIDTH):
            lag = CONV_WIDTH - 1 - t
            y = y + cp_ref[half, t:t + 1, :] * u_ref[half, halo - lag:halo - lag + tm, :]
        return y

    gt = conv(1)
    val = conv(0)
    kw = FFN_TN // FFN_DOWN_SPLITS
    for kh in range(FFN_DOWN_SPLITS):
        cols = slice(kh * kw, (kh + 1) * kw)
        act = (_silu(gt[:, cols]) * val[:, cols]).astype(BF16)
        o_ref[...] += jnp.dot(act, wd_ref[cols, :], preferred_element_type=F32)

    @pl.when(j == pl.num_programs(1) - 1)
    def _():
        o_ref[...] = x_ref[...] + mod_ref[0, GATE2:GATE2 + 1, :] * o_ref[...]


def _ffn(x1, mod3, norm_g, w_up, conv_params, w_down):
    tm, tn, halo, nj = FFN_TM, FFN_TN, FFN_HALO, FFN_NJ
    assert w_up.shape == (nj, D_MODEL, 2 * tn)
    vmem = (4 * tm * D_MODEL * 4 + 2 * halo * D_MODEL * 4
            + 2 * 3 * D_MODEL * tn * 2
            + (tm + halo) * D_MODEL * 2 + 2 * (tm + halo) * tn * 4
            + tm * tn * 4 + 4 * MIB)
    return pl.pallas_call(
        _ffn_kernel,
        grid=(TOKENS // tm, nj),
        in_specs=[pl.BlockSpec((tm, D_MODEL), lambda i, j: (i, 0)),
                  pl.BlockSpec((halo, D_MODEL),
                               lambda i, j: (jnp.maximum(i * (tm // halo) - 1, 0), 0)),
                  pl.BlockSpec((1, N_MOD, D_MODEL), lambda i, j: (i // (SEQ // tm), 0, 0)),
                  pl.BlockSpec((1, D_MODEL), lambda i, j: (0, 0)),
                  pl.BlockSpec((1, D_MODEL, 2 * tn), lambda i, j: (j, 0, 0)),
                  pl.BlockSpec((2, CONV_WIDTH + 1, tn), lambda i, j: (0, 0, j)),
                  pl.BlockSpec((tn, D_MODEL), lambda i, j: (j, 0))],
        out_specs=pl.BlockSpec((tm, D_MODEL), lambda i, j: (i, 0)),
        out_shape=jax.ShapeDtypeStruct((TOKENS, D_MODEL), F32),
        scratch_shapes=[pltpu.VMEM((tm + halo, D_MODEL), BF16),
                        pltpu.VMEM((2, tm + halo, tn), F32)],
        compiler_params=_params(("arbitrary", "arbitrary"), vmem),
        name="ffn",
    )(x1, x1, mod3, norm_g, w_up, conv_params, w_down)


def kernel(x, c, w_ada, b_ada, norm1_g, w_in, q_norm_g, k_norm_g, rel_bias, ret_norm_g,
           w_attn_br, w_ret_br, w_o, norm2_g, w_up, conv_w, conv_b, w_down):
    assert x.shape == (BATCH, SEQ, D_MODEL) and w_ada.shape[0] == 1
    assert _far_bucket_is_last()
    layer = 0
    x2d = x.reshape(TOKENS, D_MODEL)

    c_pad = jnp.pad(c, ((0, 8 - BATCH), (0, 0)))
    mod = _modulation(c_pad, w_ada[layer], b_ada)[:BATCH]
    mod3 = mod.reshape(BATCH, N_MOD, D_MODEL)

    proj = _input_projection(_prenorm(x2d, mod3, norm1_g), w_in[layer])
    proj3 = proj.reshape(BATCH, SEQ, IN_WIDTH)

    bias_tiles = _relbias_tiles(rel_bias)
    yr, (w_attn_b, w_ret_b, w_o_b) = _retention(
        proj3, ret_norm_g, (w_attn_br[layer], w_ret_br[layer], w_o[layer]),
        (MIX_TN, MIX_TN, None))
    ya, (w_up_b, w_down_b) = _moba(proj3, rel_bias, q_norm_g, k_norm_g, bias_tiles,
                                   (w_up[layer], w_down[layer]), ((FFN_TN, 2), None))

    x1 = _mixer(ya.reshape(TOKENS, ATTN_WIDTH), yr.reshape(TOKENS, RET_V_WIDTH), proj,
                w_attn_b, w_ret_b, w_o_b, x2d, mod3)

    conv_params = jnp.concatenate([conv_w[layer], conv_b], axis=0).reshape(
        CONV_WIDTH + 1, 2, FFN_DIM).transpose(1, 0, 2)
    out = _ffn(x1, mod3, norm2_g, w_up_b, conv_params, w_down_b)
    return out.reshape(BATCH, SEQ, D_MODEL)
```

```python
import functools
import math

import numpy as np
import jax
import jax.numpy as jnp
from jax import lax
from jax.experimental import pallas as pl
from jax.experimental.pallas import tpu as pltpu

F32 = jnp.float32
BF16 = jnp.bfloat16

D_MODEL = 2048
BATCH = 4
SEQ = 2048
ATTN_HEADS = 8
ATTN_HEAD_DIM = 128
MOBA_BLOCK = 256
MOBA_TOPK = 3
REL_BUCKETS = 32
REL_MAX_DIST = 128
RET_HEADS = 8
RET_KEY_DIM = 128
RET_VAL_DIM = 256
RET_CHUNK = 128
ROPE_BASE = 10000.0
FFN_DIM = 5632
CONV_WIDTH = 3
EPS = 1e-6
N_MOD = 6

ATTN_WIDTH = ATTN_HEADS * ATTN_HEAD_DIM
RET_QK_WIDTH = RET_HEADS * RET_KEY_DIM
RET_V_WIDTH = RET_HEADS * RET_VAL_DIM
OFF_QA = 0
OFF_KA = OFF_QA + ATTN_WIDTH
OFF_VA = OFF_KA + ATTN_WIDTH
OFF_QR = OFF_VA + ATTN_WIDTH
OFF_KR = OFF_QR + RET_QK_WIDTH
OFF_VR = OFF_KR + RET_QK_WIDTH
OFF_GR = OFF_VR + RET_V_WIDTH
OFF_GA = OFF_GR + RET_V_WIDTH
OFF_GB = OFF_GA + D_MODEL
IN_WIDTH = OFF_GB + D_MODEL

TOKENS = BATCH * SEQ
N_BLOCKS = SEQ // MOBA_BLOCK
N_CHUNKS = SEQ // RET_CHUNK
MASK_VALUE = -1e30
LOG2E = math.log2(math.e)
MOBA_ONES_ROWS = 16
MOBA_TILE_ORDER = tuple(range(N_BLOCKS))
MIB = 1024 * 1024

SHIFT1, SCALE1, GATE1, SHIFT2, SCALE2, GATE2 = range(N_MOD)

MOD_TN = 1024
PRENORM_TM = 1024
INPROJ_TM, INPROJ_TN = 2048, 1024
MIX_TM, MIX_TN = 512, 1024
FFN_TM, FFN_TN = 1024, 512
FFN_HALO = 16
FFN_NJ = FFN_DIM // FFN_TN
FFN_DOWN_SPLITS = 2
NORM_ROWS = 16
NORM_UNROLL = 16


def _params(semantics, vmem_bytes):
    return pltpu.CompilerParams(dimension_semantics=semantics,
                                vmem_limit_bytes=int(vmem_bytes))


def _sigmoid(v):
    return 0.5 + 0.5 * jnp.tanh(0.5 * v)


def _silu(v):
    hv = 0.5 * v
    return hv + hv * jnp.tanh(hv)


def _cast_rider(weights, col_tiles, steps, step_of):
    in_specs, out_specs, out_shapes = [], [], []
    for w, tile in zip(weights, col_tiles):
        rows, cols = w.shape
        assert rows % (steps * 16) == 0, (rows, steps)
        slab = rows // steps
        in_specs.append(pl.BlockSpec((slab, cols), lambda *g: (step_of(*g), 0)))
        if tile is None:
            out_specs.append(pl.BlockSpec((slab, cols), lambda *g: (step_of(*g), 0)))
            out_shapes.append(pltpu.HBM((rows, cols), BF16))
        else:
            width = tile if isinstance(tile, int) else tile[0] * tile[1]
            out_specs.append(pl.BlockSpec((cols // width, slab, width),
                                          lambda *g: (0, step_of(*g), 0)))
            out_shapes.append(pltpu.HBM((cols // width, rows, width), BF16))
    return in_specs, out_specs, out_shapes


def _run_cast_rider(col_tiles, src_refs, dst_refs):
    for tile, src, dst in zip(col_tiles, src_refs, dst_refs):
        if tile is None:
            dst[...] = src[...].astype(BF16)
        elif isinstance(tile, int):
            for t in range(dst.shape[0]):
                dst[t] = src[:, t * tile:(t + 1) * tile].astype(BF16)
        else:
            t_w, parts = tile
            part_w = src.shape[1] // parts
            for t in range(dst.shape[0]):
                for p in range(parts):
                    lo = p * part_w + t * t_w
                    dst[t, :, p * t_w:(p + 1) * t_w] = src[:, lo:lo + t_w].astype(BF16)


def _rms_mod(x, gain, shift):
    ms = jnp.mean(x * x, axis=-1, keepdims=True)
    return (x * lax.rsqrt(ms + EPS)) * gain + shift


def _split_bf16(v):
    hi = v.astype(BF16)
    lo = (v - hi.astype(F32)).astype(BF16)
    return hi, lo


def _mod_kernel(c_ref, w_ref, b_ref, o_ref):
    rows = c_ref.shape[0]
    s_hi, s_lo = _split_bf16(_silu(c_ref[...]))
    w_hi, w_lo = _split_bf16(w_ref[...])
    both = jnp.dot(jnp.concatenate([s_hi, s_lo], axis=0), w_hi, preferred_element_type=F32)
    cross = jnp.dot(s_hi, w_lo, preferred_element_type=F32)
    o_ref[...] = (both[0:rows] + both[rows:]) + cross + b_ref[...]


def _modulation(c_pad, w_ada, b_ada):
    rows = c_pad.shape[0]
    n = w_ada.shape[1]
    return pl.pallas_call(
        _mod_kernel,
        grid=(n // MOD_TN,),
        in_specs=[pl.BlockSpec((rows, D_MODEL), lambda j: (0, 0)),
                  pl.BlockSpec((D_MODEL, MOD_TN), lambda j: (0, j)),
                  pl.BlockSpec((1, MOD_TN), lambda j: (0, j))],
        out_specs=pl.BlockSpec((rows, MOD_TN), lambda j: (0, j)),
        out_shape=jax.ShapeDtypeStruct((rows, n), F32),
        compiler_params=_params(("arbitrary",), 2 * D_MODEL * MOD_TN * 4 + 8 * MIB),
        name="mod",
    )(c_pad, w_ada, b_ada)


def _prenorm_kernel(x_ref, mod_ref, g_ref, h_ref):
    shift = mod_ref[0, SHIFT1:SHIFT1 + 1, :]
    gain = g_ref[...] * (1.0 + mod_ref[0, SCALE1:SCALE1 + 1, :])

    def body(r, carry):
        rows = pl.ds(pl.multiple_of(r * NORM_ROWS, NORM_ROWS), NORM_ROWS)
        h_ref[rows, :] = _rms_mod(x_ref[rows, :], gain, shift).astype(BF16)
        return carry

    lax.fori_loop(0, PRENORM_TM // NORM_ROWS, body, 0, unroll=NORM_UNROLL)


def _prenorm(x2d, mod3, norm_g):
    tm = PRENORM_TM
    return pl.pallas_call(
        _prenorm_kernel,
        grid=(TOKENS // tm,),
        in_specs=[pl.BlockSpec((tm, D_MODEL), lambda i: (i, 0)),
                  pl.BlockSpec((1, N_MOD, D_MODEL), lambda i: (i // (SEQ // tm), 0, 0)),
                  pl.BlockSpec((1, D_MODEL), lambda i: (0, 0))],
        out_specs=pl.BlockSpec((tm, D_MODEL), lambda i: (i, 0)),
        out_shape=pltpu.HBM((TOKENS, D_MODEL), BF16),
        compiler_params=_params(("arbitrary",), 2 * tm * D_MODEL * (4 + 2) + 4 * MIB),
        name="prenorm",
    )(x2d, mod3, norm_g)


def _inproj_kernel(h_ref, w_ref, o_ref):
    o_ref[...] = jnp.dot(h_ref[...], w_ref[...].astype(BF16),
                         preferred_element_type=F32).astype(o_ref.dtype)


def _input_projection(h2d, w_in):
    tm, tn = INPROJ_TM, INPROJ_TN
    vmem = (2 * tm * D_MODEL * 2 + 2 * D_MODEL * tn * 4 + 2 * tm * tn * 2
            + D_MODEL * tn * 2 + tm * tn * 4 + 4 * MIB)
    return pl.pallas_call(
        _inproj_kernel,
        grid=(IN_WIDTH // tn, TOKENS // tm),
        in_specs=[pl.BlockSpec((tm, D_MODEL), lambda j, i: (i, 0)),
                  pl.BlockSpec((D_MODEL, tn), lambda j, i: (0, j))],
        out_specs=pl.BlockSpec((tm, tn), lambda j, i: (i, j)),
        out_shape=jax.ShapeDtypeStruct((TOKENS, IN_WIDTH), BF16),
        compiler_params=_params(("arbitrary", "arbitrary"), vmem),
        name="inproj",
    )(h2d, w_in)


def _relbias_kernel(rb_ref, o_ref):
    h = pl.program_id(0)
    shape = (2 * MOBA_BLOCK, MOBA_BLOCK)
    key = lax.broadcasted_iota(jnp.int32, shape, 0)
    qry = lax.broadcasted_iota(jnp.int32, shape, 1)
    dist = qry - key + MOBA_BLOCK
    n = jnp.maximum(dist, 0)
    max_exact = REL_BUCKETS // 2
    nf = jnp.maximum(n, 1).astype(F32)
    large = max_exact + (jnp.log(nf / max_exact) / math.log(REL_MAX_DIST / max_exact)
                         * (REL_BUCKETS - max_exact)).astype(jnp.int32)
    large = jnp.minimum(large, REL_BUCKETS - 1)
    bucket = jnp.where(n < max_exact, n, large)
    bias = jnp.zeros(shape, F32)
    for b in range(REL_BUCKETS):
        bias = jnp.where(bucket == b, rb_ref[b, h], bias)
    o_ref[0] = jnp.where(dist >= 0, bias * LOG2E, MASK_VALUE)


def _relbias_tiles(rel_bias):
    return pl.pallas_call(
        _relbias_kernel,
        grid=(ATTN_HEADS,),
        in_specs=[pl.BlockSpec(memory_space=pltpu.SMEM)],
        out_specs=pl.BlockSpec((1, 2 * MOBA_BLOCK, MOBA_BLOCK), lambda h: (h, 0, 0)),
        out_shape=jax.ShapeDtypeStruct((ATTN_HEADS, 2 * MOBA_BLOCK, MOBA_BLOCK), F32),
        compiler_params=_params(("arbitrary",), 16 * MIB),
        name="relbias",
    )(rel_bias)


def _far_bucket_is_last():
    d = np.arange(MOBA_BLOCK + 1, SEQ, dtype=np.float32)
    max_exact = REL_BUCKETS // 2
    large = max_exact + (np.log(d / max_exact) / math.log(REL_MAX_DIST / max_exact)
                         * (REL_BUCKETS - max_exact)).astype(np.int32)
    return bool(np.all(np.minimum(large, REL_BUCKETS - 1) == REL_BUCKETS - 1))


def _moba_kernel(cast_tiles, rb_ref, q_ref, k_ref, v_ref, gq_ref, gk_ref, bt_ref, *refs):
    n_cast = len(cast_tiles)
    cast_src, (o_ref, *cast_dst) = refs[:n_cast], refs[n_cast:2 * n_cast + 1]
    qb_ref, kb_ref, vt_ref, *s_refs = refs[2 * n_cast + 1:]
    _run_cast_rider(cast_tiles, cast_src, cast_dst)

    h = pl.program_id(1)
    far_bias = rb_ref[REL_BUCKETS - 1, h] * LOG2E
    blk, hd = MOBA_BLOCK, ATTN_HEAD_DIM
    nt = (((1,), (1,)), ((), ()))

    q = q_ref[0].astype(F32)
    k = k_ref[0].astype(F32)
    qn = (q * lax.rsqrt(jnp.mean(q * q, axis=-1, keepdims=True) + EPS)) * gq_ref[...]
    kn = (k * lax.rsqrt(jnp.mean(k * k, axis=-1, keepdims=True) + EPS)) * gk_ref[...]
    qb_ref[...] = (qn * (hd ** -0.5 * LOG2E)).astype(BF16)
    kb_ref[...] = kn.astype(BF16)
    vt_ref[0:hd, :] = v_ref[0].astype(F32).T.astype(BF16)
    vt_ref[hd:, :] = jnp.ones((MOBA_ONES_ROWS, SEQ), BF16)

    k_mean = jnp.concatenate(
        [jnp.sum(kn[n * blk:(n + 1) * blk], axis=0, keepdims=True) for n in range(N_BLOCKS)],
        axis=0) * (1.0 / blk)
    gate = lax.dot_general(k_mean, qn, nt, preferred_element_type=F32,
                           precision=lax.Precision.HIGHEST)
    row = lax.broadcasted_iota(jnp.int32, (N_BLOCKS, blk), 0)

    def scores(qi, s_ref):
        cols = slice(qi * blk, (qi + 1) * blk)
        nk = (qi + 1) * blk

        mask_add = None
        if qi > MOBA_TOPK:
            g = gate[:, cols]
            rank = jnp.zeros((N_BLOCKS, blk), F32)
            for m in range(qi):
                gm = g[m:m + 1, :]
                beats = (gm > g) | ((gm == g) & (row > m))
                rank = rank + jnp.where(beats, 1.0, 0.0)
            mask_add = jnp.where(rank < MOBA_TOPK, 0.0, MASK_VALUE)

        s_all = lax.dot_general(kb_ref[0:nk, :], qb_ref[cols, :], nt,
                                preferred_element_type=F32)
        m8 = None
        for n in range(qi + 1):
            s_blk = s_all[n * blk:(n + 1) * blk]
            if n == qi:
                s_blk = s_blk + bt_ref[0, blk:2 * blk, :]
            else:
                if n == qi - 1:
                    s_blk = s_blk + bt_ref[0, 0:blk, :]
                    if mask_add is not None:
                        s_blk = s_blk + mask_add[n:n + 1, :]
                elif mask_add is not None:
                    s_blk = s_blk + (mask_add[n:n + 1, :] + far_bias)
                else:
                    s_blk = s_blk + far_bias
            s_ref[n * blk:(n + 1) * blk, :] = s_blk
            b8 = jnp.max(s_blk.reshape(blk // 8, 8, blk), axis=0)
            m8 = b8 if m8 is None else jnp.maximum(m8, b8)
        return jnp.max(m8, axis=0, keepdims=True)

    def attend(qi, s_ref, m):
        cols = slice(qi * blk, (qi + 1) * blk)
        nk = (qi + 1) * blk
        p = jnp.exp2(s_ref[0:nk, :] - m).astype(BF16)
        o_aug = jnp.dot(vt_ref[:, 0:nk], p, preferred_element_type=F32)
        o_t = o_aug[0:hd] * (1.0 / o_aug[hd:hd + 1])
        o_ref[0, cols, :] = o_t.T.astype(o_ref.dtype)

    tiles = list(zip(MOBA_TILE_ORDER, s_refs))
    maxes = [scores(qi, s_ref) for qi, s_ref in tiles]
    for (qi, s_ref), m in zip(tiles, maxes):
        attend(qi, s_ref, m)


def _moba(proj3, rel_bias, q_norm_g, k_norm_g, bias_tiles, cast_weights, cast_tiles):
    hd = ATTN_HEAD_DIM
    head = lambda off: (lambda b, h: (b, 0, off // hd + h))
    cast_in, cast_out, cast_shapes = _cast_rider(
        cast_weights, cast_tiles, BATCH * ATTN_HEADS, lambda b, h: b * ATTN_HEADS + h)
    cast_bytes = sum(2 * (4 + 2) * w.size // (BATCH * ATTN_HEADS) for w in cast_weights)
    outs = pl.pallas_call(
        functools.partial(_moba_kernel, tuple(cast_tiles)),
        grid=(BATCH, ATTN_HEADS),
        in_specs=[pl.BlockSpec(memory_space=pltpu.SMEM),
                  pl.BlockSpec((1, SEQ, hd), head(OFF_QA)),
                  pl.BlockSpec((1, SEQ, hd), head(OFF_KA)),
                  pl.BlockSpec((1, SEQ, hd), head(OFF_VA)),
                  pl.BlockSpec((1, hd), lambda b, h: (0, 0)),
                  pl.BlockSpec((1, hd), lambda b, h: (0, 0)),
                  pl.BlockSpec((1, 2 * MOBA_BLOCK, MOBA_BLOCK), lambda b, h: (h, 0, 0))]
                 + cast_in,
        out_specs=[pl.BlockSpec((1, SEQ, hd), lambda b, h: (b, 0, h))] + cast_out,
        out_shape=[jax.ShapeDtypeStruct((BATCH, SEQ, ATTN_WIDTH), BF16)] + cast_shapes,
        scratch_shapes=[pltpu.VMEM((SEQ, hd), BF16),
                        pltpu.VMEM((SEQ, hd), BF16),
                        pltpu.VMEM((hd + MOBA_ONES_ROWS, SEQ), BF16)]
                       + [pltpu.VMEM((SEQ, MOBA_BLOCK), F32)] * len(MOBA_TILE_ORDER),
        compiler_params=_params(("arbitrary", "arbitrary"), 24 * MIB + cast_bytes),
        name="moba",
    )(rel_bias, proj3, proj3, proj3, q_norm_g, k_norm_g, bias_tiles, *cast_weights)
    return outs[0], outs[1:]


def _retention_kernel(cast_tiles, cd_ref, q_ref, k_ref, v_ref, gr_ref, cos_ref, sin_ref,
                      dmask_ref, qdec_ref, kdec_ref, gn_ref, *refs):
    n_cast = len(cast_tiles)
    cast_src, (o_ref, *cast_dst) = refs[:n_cast], refs[n_cast:2 * n_cast + 1]
    qf_ref, kf_ref, y_ref, kv_ref = refs[2 * n_cast + 1:]
    _run_cast_rider(cast_tiles, cast_src, cast_dst)
    h = pl.program_id(1)
    chunk_decay = cd_ref[h]
    half = RET_KEY_DIM // 2
    cos = cos_ref[...]
    sin = sin_ref[...]
    q = q_ref[0].astype(F32)
    k = k_ref[0].astype(F32)
    qf_ref[...] = q * cos + pltpu.roll(q, half, 1) * sin
    kf_ref[...] = (k * cos + pltpu.roll(k, half, 1) * sin) * (RET_KEY_DIM ** -0.5)

    nt = (((1,), (1,)), ((), ()))
    chunks = [slice(c * RET_CHUNK, (c + 1) * RET_CHUNK) for c in range(N_CHUNKS)]

    scores = [(lax.dot_general(qf_ref[rows, :].astype(BF16), kf_ref[rows, :].astype(BF16), nt,
                               preferred_element_type=F32) * dmask_ref[0]).astype(BF16)
              for rows in chunks]
    for c, rows in enumerate(chunks):
        vc = v_ref[0, rows, :]
        y_ref[rows, :] = jnp.dot(scores[c], vc, preferred_element_type=F32)
        kd_t = (kf_ref[rows, :] * kdec_ref[0]).T.astype(BF16)
        kv_ref[c] = jnp.dot(kd_t, vc, preferred_element_type=F32)

    state = jnp.zeros((RET_KEY_DIM, RET_VAL_DIM), F32)
    for c, rows in enumerate(chunks):
        y_ref[rows, :] += jnp.dot((qf_ref[rows, :] * qdec_ref[0]).astype(BF16),
                                  state.astype(BF16), preferred_element_type=F32)
        state = chunk_decay * state + kv_ref[c]

    for rows in chunks:
        y = y_ref[rows, :]
        mu = jnp.mean(y, axis=-1, keepdims=True)
        yc = y - mu
        var = jnp.mean(yc * yc, axis=-1, keepdims=True)
        yn = (yc * lax.rsqrt(var + EPS)) * gn_ref[...]
        o_ref[0, rows, :] = (yn * _silu(gr_ref[0, rows, :]).astype(F32)).astype(o_ref.dtype)


def _retention_tables():
    f32 = np.float32
    half = RET_KEY_DIM // 2
    freqs = np.power(f32(ROPE_BASE), -np.arange(half, dtype=f32) / f32(half))
    ang = np.arange(SEQ, dtype=f32)[:, None] * freqs[None, :]
    cos, sin = np.cos(ang), np.sin(ang)
    cos_full = np.concatenate([cos, cos], axis=-1)
    sin_signed = np.concatenate([-sin, sin], axis=-1)

    log_decay = np.log(f32(1.0) - np.power(f32(2.0), f32(-5.0) - np.arange(RET_HEADS, dtype=f32)))
    i = np.arange(RET_CHUNK, dtype=f32)
    diff = i[:, None] - i[None, :]
    ld = log_decay[:, None, None]
    inner_decay = np.where(diff >= 0, np.exp(ld * np.maximum(diff, f32(0.0))), f32(0.0))
    q_decay = np.exp(log_decay[:, None] * (i + f32(1.0)))
    k_decay = np.exp(log_decay[:, None] * (f32(RET_CHUNK - 1.0) - i))
    chunk_decay = np.exp(log_decay * f32(RET_CHUNK))
    bcast = lambda t: np.ascontiguousarray(
        np.broadcast_to(t[:, :, None], (RET_HEADS, RET_CHUNK, RET_KEY_DIM)))
    tables = (cos_full, sin_signed, inner_decay, bcast(q_decay), bcast(k_decay), chunk_decay)
    assert all(t.dtype == f32 for t in tables)
    return tuple(jnp.asarray(t) for t in tables)


def _retention(proj3, ret_norm_g, cast_weights, cast_tiles):
    dk, dv = RET_KEY_DIM, RET_VAL_DIM
    cos, sin, dmask, qdec, kdec, chunk_decay = _retention_tables()
    head = lambda off, w: (lambda b, h: (b, 0, off // w + h))
    per_head = lambda b, h: (h, 0, 0)
    cast_in, cast_out, cast_shapes = _cast_rider(
        cast_weights, cast_tiles, BATCH * RET_HEADS, lambda b, h: b * RET_HEADS + h)
    cast_bytes = sum(2 * (4 + 2) * w.size // (BATCH * RET_HEADS) for w in cast_weights)
    outs = pl.pallas_call(
        functools.partial(_retention_kernel, tuple(cast_tiles)),
        grid=(BATCH, RET_HEADS),
        in_specs=[pl.BlockSpec(memory_space=pltpu.SMEM),
                  pl.BlockSpec((1, SEQ, dk), head(OFF_QR, dk)),
                  pl.BlockSpec((1, SEQ, dk), head(OFF_KR, dk)),
                  pl.BlockSpec((1, SEQ, dv), head(OFF_VR, dv)),
                  pl.BlockSpec((1, SEQ, dv), head(OFF_GR, dv)),
                  pl.BlockSpec((SEQ, dk), lambda b, h: (0, 0)),
                  pl.BlockSpec((SEQ, dk), lambda b, h: (0, 0)),
                  pl.BlockSpec((1, RET_CHUNK, RET_CHUNK), per_head),
                  pl.BlockSpec((1, RET_CHUNK, dk), per_head),
                  pl.BlockSpec((1, RET_CHUNK, dk), per_head),
                  pl.BlockSpec((1, dv), lambda b, h: (0, h))] + cast_in,
        out_specs=[pl.BlockSpec((1, SEQ, dv), lambda b, h: (b, 0, h))] + cast_out,
        out_shape=[jax.ShapeDtypeStruct((BATCH, SEQ, RET_V_WIDTH), BF16)] + cast_shapes,
        scratch_shapes=[pltpu.VMEM((SEQ, dk), F32), pltpu.VMEM((SEQ, dk), F32),
                        pltpu.VMEM((SEQ, dv), F32), pltpu.VMEM((N_CHUNKS, dk, dv), F32)],
        compiler_params=_params(("arbitrary", "arbitrary"), 24 * MIB + cast_bytes),
        name="retention",
    )(chunk_decay, proj3, proj3, proj3, proj3, cos, sin, dmask, qdec, kdec, ret_norm_g,
      *cast_weights)
    return outs[0], outs[1:]


def _mixer_kernel(ya_ref, yr_ref, ga_ref, gb_ref, wa_ref, wr_ref, wo_ref, x_ref, mod_ref,
                  o_ref):
    j = pl.program_id(1)

    @pl.when(j == 0)
    def _():
        o_ref[...] = x_ref[...]

    a = jnp.dot(ya_ref[...], wa_ref[j], preferred_element_type=F32)
    r = jnp.dot(yr_ref[...], wr_ref[j], preferred_element_type=F32)
    merged = (_sigmoid(ga_ref[...].astype(F32)) * a
              + _sigmoid(gb_ref[...].astype(F32)) * r)
    wo_rows = pl.ds(pl.multiple_of(j * MIX_TN, MIX_TN), MIX_TN)
    o_ref[...] += mod_ref[0, GATE1:GATE1 + 1, :] * jnp.dot(
        merged.astype(BF16), wo_ref[wo_rows, :], preferred_element_type=F32)


def _mixer(ya2d, yr2d, proj2d, w_attn_br, w_ret_br, w_o, x2d, mod3):
    tm, tn = MIX_TM, MIX_TN
    nj = D_MODEL // tn
    assert w_attn_br.shape == (nj, ATTN_WIDTH, tn) and w_ret_br.shape == (nj, RET_V_WIDTH, tn)
    assert OFF_GA % tn == 0 and OFF_GB % tn == 0
    resident = lambda shape: pl.BlockSpec(shape, lambda i, j: (0,) * len(shape),
                                          pipeline_mode=pl.Buffered(1))
    vmem = (2 * tm * (ATTN_WIDTH + RET_V_WIDTH + 2 * tn) * 2
            + (ATTN_WIDTH + RET_V_WIDTH + D_MODEL) * D_MODEL * 2
            + 4 * tm * D_MODEL * 4 + 3 * tm * tn * 4 + 4 * MIB)
    return pl.pallas_call(
        _mixer_kernel,
        grid=(TOKENS // tm, D_MODEL // tn),
        in_specs=[pl.BlockSpec((tm, ATTN_WIDTH), lambda i, j: (i, 0)),
                  pl.BlockSpec((tm, RET_V_WIDTH), lambda i, j: (i, 0)),
                  pl.BlockSpec((tm, tn), lambda i, j: (i, OFF_GA // tn + j)),
                  pl.BlockSpec((tm, tn), lambda i, j: (i, OFF_GB // tn + j)),
                  resident((nj, ATTN_WIDTH, tn)),
                  resident((nj, RET_V_WIDTH, tn)),
                  resident((D_MODEL, D_MODEL)),
                  pl.BlockSpec((tm, D_MODEL), lambda i, j: (i, 0)),
                  pl.BlockSpec((1, N_MOD, D_MODEL), lambda i, j: (i // (SEQ // tm), 0, 0))],
        out_specs=pl.BlockSpec((tm, D_MODEL), lambda i, j: (i, 0)),
        out_shape=jax.ShapeDtypeStruct((TOKENS, D_MODEL), F32),
        compiler_params=_params(("arbitrary", "arbitrary"), vmem),
        name="mixer",
    )(ya2d, yr2d, proj2d, proj2d, w_attn_br, w_ret_br, w_o, x2d, mod3)


def _ffn_kernel(x_ref, halo_ref, mod_ref, g_ref, wu_ref, cp_ref, wd_ref,
                o_ref, h_ref, u_ref):
    i = pl.program_id(0)
    j = pl.program_id(1)
    tm, halo = FFN_TM, FFN_HALO

    @pl.when(j == 0)
    def _():
        o_ref[...] = jnp.zeros_like(o_ref)
        shift = mod_ref[0, SHIFT2:SHIFT2 + 1, :]
        gain = g_ref[...] * (1.0 + mod_ref[0, SCALE2:SCALE2 + 1, :])
        seq_start = (i % (SEQ // tm)) == 0
        h_halo = _rms_mod(halo_ref[...], gain, shift)
        h_ref[0:halo, :] = jnp.where(seq_start, 0.0, h_halo).astype(BF16)

        def body(r, carry):
            src = pl.ds(pl.multiple_of(r * NORM_ROWS, NORM_ROWS), NORM_ROWS)
            dst = pl.ds(pl.multiple_of(halo + r * NORM_ROWS, halo), NORM_ROWS)
            h_ref[dst, :] = _rms_mod(x_ref[src, :], gain, shift).astype(BF16)
            return carry

        lax.fori_loop(0, tm // NORM_ROWS, body, 0, unroll=NORM_UNROLL)

    def conv(half):
        w = wu_ref[0, :, half * FFN_TN:(half + 1) * FFN_TN]
        u_ref[half] = jnp.dot(h_ref[...], w, preferred_element_type=F32)
        y = cp_ref[half, CONV_WIDTH:CONV_WIDTH + 1, :]
        for t in range(CONV_WIDTH):
            lag = CONV_WIDTH - 1 - t
            y = y + cp_ref[half, t:t + 1, :] * u_ref[half, halo - lag:halo - lag + tm, :]
        return y

    gt = conv(1)
    val = conv(0)
    kw = FFN_TN // FFN_DOWN_SPLITS
    for kh in range(FFN_DOWN_SPLITS):
        cols = slice(kh * kw, (kh + 1) * kw)
        act = (_silu(gt[:, cols]) * val[:, cols]).astype(BF16)
        o_ref[...] += jnp.dot(act, wd_ref[cols, :], preferred_element_type=F32)

    @pl.when(j == pl.num_programs(1) - 1)
    def _():
        o_ref[...] = x_ref[...] + mod_ref[0, GATE2:GATE2 + 1, :] * o_ref[...]


def _ffn(x1, mod3, norm_g, w_up, conv_params, w_down):
    tm, tn, halo, nj = FFN_TM, FFN_TN, FFN_HALO, FFN_NJ
    assert w_up.shape == (nj, D_MODEL, 2 * tn)
    vmem = (4 * tm * D_MODEL * 4 + 2 * halo * D_MODEL * 4
            + 2 * 3 * D_MODEL * tn * 2
            + (tm + halo) * D_MODEL * 2 + 2 * (tm + halo) * tn * 4
            + tm * tn * 4 + 4 * MIB)
    return pl.pallas_call(
        _ffn_kernel,
        grid=(TOKENS // tm, nj),
        in_specs=[pl.BlockSpec((tm, D_MODEL), lambda i, j: (i, 0)),
                  pl.BlockSpec((halo, D_MODEL),
                               lambda i, j: (jnp.maximum(i * (tm // halo) - 1, 0), 0)),
                  pl.BlockSpec((1, N_MOD, D_MODEL), lambda i, j: (i // (SEQ // tm), 0, 0)),
                  pl.BlockSpec((1, D_MODEL), lambda i, j: (0, 0)),
                  pl.BlockSpec((1, D_MODEL, 2 * tn), lambda i, j: (j, 0, 0)),
                  pl.BlockSpec((2, CONV_WIDTH + 1, tn), lambda i, j: (0, 0, j)),
                  pl.BlockSpec((tn, D_MODEL), lambda i, j: (j, 0))],
        out_specs=pl.BlockSpec((tm, D_MODEL), lambda i, j: (i, 0)),
        out_shape=jax.ShapeDtypeStruct((TOKENS, D_MODEL), F32),
        scratch_shapes=[pltpu.VMEM((tm + halo, D_MODEL), BF16),
                        pltpu.VMEM((2, tm + halo, tn), F32)],
        compiler_params=_params(("arbitrary", "arbitrary"), vmem),
        name="ffn",
    )(x1, x1, mod3, norm_g, w_up, conv_params, w_down)


def kernel(x, c, w_ada, b_ada, norm1_g, w_in, q_norm_g, k_norm_g, rel_bias, ret_norm_g,
           w_attn_br, w_ret_br, w_o, norm2_g, w_up, conv_w, conv_b, w_down):
    assert x.shape == (BATCH, SEQ, D_MODEL) and w_ada.shape[0] == 1
    assert _far_bucket_is_last()
    layer = 0
    x2d = x.reshape(TOKENS, D_MODEL)

    c_pad = jnp.pad(c, ((0, 8 - BATCH), (0, 0)))
    mod = _modulation(c_pad, w_ada[layer], b_ada)[:BATCH]
    mod3 = mod.reshape(BATCH, N_MOD, D_MODEL)

    proj = _input_projection(_prenorm(x2d, mod3, norm1_g), w_in[layer])
    proj3 = proj.reshape(BATCH, SEQ, IN_WIDTH)

    bias_tiles = _relbias_tiles(rel_bias)
    yr, (w_attn_b, w_ret_b, w_o_b) = _retention(
        proj3, ret_norm_g, (w_attn_br[layer], w_ret_br[layer], w_o[layer]),
        (MIX_TN, MIX_TN, None))
    ya, (w_up_b, w_down_b) = _moba(proj3, rel_bias, q_norm_g, k_norm_g, bias_tiles,
                                   (w_up[layer], w_down[layer]), ((FFN_TN, 2), None))

    x1 = _mixer(ya.reshape(TOKENS, ATTN_WIDTH), yr.reshape(TOKENS, RET_V_WIDTH), proj,
                w_attn_b, w_ret_b, w_o_b, x2d, mod3)

    conv_params = jnp.concatenate([conv_w[layer], conv_b], axis=0).reshape(
        CONV_WIDTH + 1, 2, FFN_DIM).transpose(1, 0, 2)
    out = _ffn(x1, mod3, norm2_g, w_up_b, conv_params, w_down_b)
    return out.reshape(BATCH, SEQ, D_MODEL)
```

```python
import functools
import math

import numpy as np
import jax
import jax.numpy as jnp
from jax import lax
from jax.experimental import pallas as pl
from jax.experimental.pallas import tpu as pltpu

F32 = jnp.float32
BF16 = jnp.bfloat16

D_MODEL = 2048
BATCH = 4
SEQ = 2048
ATTN_HEADS = 8
ATTN_HEAD_DIM = 128
MOBA_BLOCK = 256
MOBA_TOPK = 3
REL_BUCKETS = 32
REL_MAX_DIST = 128
RET_HEADS = 8
RET_KEY_DIM = 128
RET_VAL_DIM = 256
RET_CHUNK = 128
ROPE_BASE = 10000.0
FFN_DIM = 5632
CONV_WIDTH = 3
EPS = 1e-6
N_MOD = 6

ATTN_WIDTH = ATTN_HEADS * ATTN_HEAD_DIM
RET_QK_WIDTH = RET_HEADS * RET_KEY_DIM
RET_V_WIDTH = RET_HEADS * RET_VAL_DIM
OFF_QA = 0
OFF_KA = OFF_QA + ATTN_WIDTH
OFF_VA = OFF_KA + ATTN_WIDTH
OFF_QR = OFF_VA + ATTN_WIDTH
OFF_KR = OFF_QR + RET_QK_WIDTH
OFF_VR = OFF_KR + RET_QK_WIDTH
OFF_GR = OFF_VR + RET_V_WIDTH
OFF_GA = OFF_GR + RET_V_WIDTH
OFF_GB = OFF_GA + D_MODEL
IN_WIDTH = OFF_GB + D_MODEL

TOKENS = BATCH * SEQ
N_BLOCKS = SEQ // MOBA_BLOCK
N_CHUNKS = SEQ // RET_CHUNK
MASK_VALUE = -1e30
LOG2E = math.log2(math.e)
F32_ROWS = 8
BF16_ROWS = 16
MOBA_ONES_ROWS = BF16_ROWS
MOBA_TILE_ORDER = tuple(range(N_BLOCKS))
MIB = 1024 * 1024

SHIFT1, SCALE1, GATE1, SHIFT2, SCALE2, GATE2 = range(N_MOD)

MOD_TN = 1024
PRENORM_TM = 1024
INPROJ_TM, INPROJ_TN = 2048, 1024
MIX_TM, MIX_TN = 512, 1024
FFN_TM, FFN_TN = 1024, 512
FFN_HALO = BF16_ROWS
FFN_NJ = FFN_DIM // FFN_TN
FFN_DOWN_SPLITS = 2
NORM_ROWS = 16
NORM_UNROLL = 16


def _params(semantics, vmem_bytes):
    return pltpu.CompilerParams(dimension_semantics=semantics,
                                vmem_limit_bytes=int(vmem_bytes))


def _sigmoid(v):
    return 0.5 + 0.5 * jnp.tanh(0.5 * v)


def _silu(v):
    hv = 0.5 * v
    return hv + hv * jnp.tanh(hv)


def _cast_rider(weights, col_tiles, steps, step_of):
    in_specs, out_specs, out_shapes = [], [], []
    for w, tile in zip(weights, col_tiles):
        rows, cols = w.shape
        assert rows % (steps * BF16_ROWS) == 0, (rows, steps)
        slab = rows // steps
        in_specs.append(pl.BlockSpec((slab, cols), lambda *g: (step_of(*g), 0)))
        if tile is None:
            out_specs.append(pl.BlockSpec((slab, cols), lambda *g: (step_of(*g), 0)))
            out_shapes.append(pltpu.HBM((rows, cols), BF16))
        else:
            width = tile if isinstance(tile, int) else tile[0] * tile[1]
            out_specs.append(pl.BlockSpec((cols // width, slab, width),
                                          lambda *g: (0, step_of(*g), 0)))
            out_shapes.append(pltpu.HBM((cols // width, rows, width), BF16))
    return in_specs, out_specs, out_shapes


def _run_cast_rider(col_tiles, src_refs, dst_refs):
    for tile, src, dst in zip(col_tiles, src_refs, dst_refs):
        if tile is None:
            dst[...] = src[...].astype(BF16)
        elif isinstance(tile, int):
            for t in range(dst.shape[0]):
                dst[t] = src[:, t * tile:(t + 1) * tile].astype(BF16)
        else:
            t_w, parts = tile
            part_w = src.shape[1] // parts
            for t in range(dst.shape[0]):
                for p in range(parts):
                    lo = p * part_w + t * t_w
                    dst[t, :, p * t_w:(p + 1) * t_w] = src[:, lo:lo + t_w].astype(BF16)


def _rms_mod(x, gain, shift):
    ms = jnp.mean(x * x, axis=-1, keepdims=True)
    return (x * lax.rsqrt(ms + EPS)) * gain + shift


def _split_bf16(v):
    hi = v.astype(BF16)
    lo = (v - hi.astype(F32)).astype(BF16)
    return hi, lo


def _mod_kernel(c_ref, w_ref, b_ref, o_ref):
    rows = c_ref.shape[0]
    s_hi, s_lo = _split_bf16(_silu(c_ref[...]))
    w_hi, w_lo = _split_bf16(w_ref[...])
    both = jnp.dot(jnp.concatenate([s_hi, s_lo], axis=0), w_hi, preferred_element_type=F32)
    cross = jnp.dot(s_hi, w_lo, preferred_element_type=F32)
    o_ref[...] = (both[0:rows] + both[rows:]) + cross + b_ref[...]


def _modulation(c_pad, w_ada, b_ada):
    rows = c_pad.shape[0]
    n = w_ada.shape[1]
    return pl.pallas_call(
        _mod_kernel,
        grid=(n // MOD_TN,),
        in_specs=[pl.BlockSpec((rows, D_MODEL), lambda j: (0, 0)),
                  pl.BlockSpec((D_MODEL, MOD_TN), lambda j: (0, j)),
                  pl.BlockSpec((1, MOD_TN), lambda j: (0, j))],
        out_specs=pl.BlockSpec((rows, MOD_TN), lambda j: (0, j)),
        out_shape=jax.ShapeDtypeStruct((rows, n), F32),
        compiler_params=_params(("arbitrary",), 2 * D_MODEL * MOD_TN * 4 + 8 * MIB),
        name="mod",
    )(c_pad, w_ada, b_ada)


def _prenorm_kernel(x_ref, mod_ref, g_ref, h_ref):
    shift = mod_ref[0, SHIFT1:SHIFT1 + 1, :]
    gain = g_ref[...] * (1.0 + mod_ref[0, SCALE1:SCALE1 + 1, :])

    def body(r, carry):
        rows = pl.ds(pl.multiple_of(r * NORM_ROWS, NORM_ROWS), NORM_ROWS)
        h_ref[rows, :] = _rms_mod(x_ref[rows, :], gain, shift).astype(BF16)
        return carry

    lax.fori_loop(0, PRENORM_TM // NORM_ROWS, body, 0, unroll=NORM_UNROLL)


def _prenorm(x2d, mod3, norm_g):
    tm = PRENORM_TM
    return pl.pallas_call(
        _prenorm_kernel,
        grid=(TOKENS // tm,),
        in_specs=[pl.BlockSpec((tm, D_MODEL), lambda i: (i, 0)),
                  pl.BlockSpec((1, N_MOD, D_MODEL), lambda i: (i // (SEQ // tm), 0, 0)),
                  pl.BlockSpec((1, D_MODEL), lambda i: (0, 0))],
        out_specs=pl.BlockSpec((tm, D_MODEL), lambda i: (i, 0)),
        out_shape=pltpu.HBM((TOKENS, D_MODEL), BF16),
        compiler_params=_params(("arbitrary",), 2 * tm * D_MODEL * (4 + 2) + 4 * MIB),
        name="prenorm",
    )(x2d, mod3, norm_g)


def _inproj_kernel(h_ref, w_ref, o_ref):
    o_ref[...] = jnp.dot(h_ref[...], w_ref[...].astype(BF16),
                         preferred_element_type=F32).astype(o_ref.dtype)


def _input_projection(h2d, w_in):
    tm, tn = INPROJ_TM, INPROJ_TN
    vmem = (2 * tm * D_MODEL * 2 + 2 * D_MODEL * tn * 4 + 2 * tm * tn * 2
            + D_MODEL * tn * 2 + tm * tn * 4 + 4 * MIB)
    return pl.pallas_call(
        _inproj_kernel,
        grid=(IN_WIDTH // tn, TOKENS // tm),
        in_specs=[pl.BlockSpec((tm, D_MODEL), lambda j, i: (i, 0)),
                  pl.BlockSpec((D_MODEL, tn), lambda j, i: (0, j))],
        out_specs=pl.BlockSpec((tm, tn), lambda j, i: (i, j)),
        out_shape=jax.ShapeDtypeStruct((TOKENS, IN_WIDTH), BF16),
        compiler_params=_params(("arbitrary", "arbitrary"), vmem),
        name="inproj",
    )(h2d, w_in)


def _relbias_kernel(rb_ref, o_ref):
    h = pl.program_id(0)
    shape = (2 * MOBA_BLOCK, MOBA_BLOCK)
    key = lax.broadcasted_iota(jnp.int32, shape, 0)
    qry = lax.broadcasted_iota(jnp.int32, shape, 1)
    dist = qry - key + MOBA_BLOCK
    n = jnp.maximum(dist, 0)
    max_exact = REL_BUCKETS // 2
    nf = jnp.maximum(n, 1).astype(F32)
    large = max_exact + (jnp.log(nf / max_exact) / math.log(REL_MAX_DIST / max_exact)
                         * (REL_BUCKETS - max_exact)).astype(jnp.int32)
    large = jnp.minimum(large, REL_BUCKETS - 1)
    bucket = jnp.where(n < max_exact, n, large)
    bias = jnp.zeros(shape, F32)
    for b in range(REL_BUCKETS):
        bias = jnp.where(bucket == b, rb_ref[b, h], bias)
    o_ref[0] = jnp.where(dist >= 0, bias * LOG2E, MASK_VALUE)


def _relbias_tiles(rel_bias):
    return pl.pallas_call(
        _relbias_kernel,
        grid=(ATTN_HEADS,),
        in_specs=[pl.BlockSpec(memory_space=pltpu.SMEM)],
        out_specs=pl.BlockSpec((1, 2 * MOBA_BLOCK, MOBA_BLOCK), lambda h: (h, 0, 0)),
        out_shape=jax.ShapeDtypeStruct((ATTN_HEADS, 2 * MOBA_BLOCK, MOBA_BLOCK), F32),
        compiler_params=_params(("arbitrary",), 10 * 2 * MOBA_BLOCK * MOBA_BLOCK * 4),
        name="relbias",
    )(rel_bias)


def _far_bucket_is_last():
    d = np.arange(MOBA_BLOCK + 1, SEQ, dtype=np.float32)
    max_exact = REL_BUCKETS // 2
    large = max_exact + (np.log(d / max_exact) / math.log(REL_MAX_DIST / max_exact)
                         * (REL_BUCKETS - max_exact)).astype(np.int32)
    return bool(np.all(np.minimum(large, REL_BUCKETS - 1) == REL_BUCKETS - 1))


def _moba_kernel(cast_tiles, rb_ref, q_ref, k_ref, v_ref, gq_ref, gk_ref, bt_ref, *refs):
    n_cast = len(cast_tiles)
    cast_src, (o_ref, *cast_dst) = refs[:n_cast], refs[n_cast:2 * n_cast + 1]
    qb_ref, kb_ref, vt_ref, *s_refs = refs[2 * n_cast + 1:]
    _run_cast_rider(cast_tiles, cast_src, cast_dst)

    h = pl.program_id(1)
    far_bias = rb_ref[REL_BUCKETS - 1, h] * LOG2E
    blk, hd = MOBA_BLOCK, ATTN_HEAD_DIM
    nt = (((1,), (1,)), ((), ()))

    q = q_ref[0].astype(F32)
    k = k_ref[0].astype(F32)
    qn = (q * lax.rsqrt(jnp.mean(q * q, axis=-1, keepdims=True) + EPS)) * gq_ref[...]
    kn = (k * lax.rsqrt(jnp.mean(k * k, axis=-1, keepdims=True) + EPS)) * gk_ref[...]
    qb_ref[...] = (qn * (hd ** -0.5 * LOG2E)).astype(BF16)
    kb_ref[...] = kn.astype(BF16)
    vt_ref[0:hd, :] = v_ref[0].astype(F32).T.astype(BF16)
    vt_ref[hd:, :] = jnp.ones((MOBA_ONES_ROWS, SEQ), BF16)

    k_mean = jnp.concatenate(
        [jnp.sum(kn[n * blk:(n + 1) * blk], axis=0, keepdims=True) for n in range(N_BLOCKS)],
        axis=0) * (1.0 / blk)
    gate = lax.dot_general(k_mean, qn, nt, preferred_element_type=F32,
                           precision=lax.Precision.HIGHEST)
    row = lax.broadcasted_iota(jnp.int32, (N_BLOCKS, blk), 0)

    def scores(qi, s_ref):
        cols = slice(qi * blk, (qi + 1) * blk)
        nk = (qi + 1) * blk

        mask_add = None
        if qi > MOBA_TOPK:
            g = gate[:, cols]
            rank = jnp.zeros((N_BLOCKS, blk), F32)
            for m in range(qi):
                gm = g[m:m + 1, :]
                beats = (gm > g) | ((gm == g) & (row > m))
                rank = rank + jnp.where(beats, 1.0, 0.0)
            mask_add = jnp.where(rank < MOBA_TOPK, 0.0, MASK_VALUE)

        s_all = lax.dot_general(kb_ref[0:nk, :], qb_ref[cols, :], nt,
                                preferred_element_type=F32)
        m8 = None
        for n in range(qi + 1):
            s_blk = s_all[n * blk:(n + 1) * blk]
            if n == qi:
                s_blk = s_blk + bt_ref[0, blk:2 * blk, :]
            else:
                if n == qi - 1:
                    s_blk = s_blk + bt_ref[0, 0:blk, :]
                    if mask_add is not None:
                        s_blk = s_blk + mask_add[n:n + 1, :]
                elif mask_add is not None:
                    s_blk = s_blk + (mask_add[n:n + 1, :] + far_bias)
                else:
                    s_blk = s_blk + far_bias
            s_ref[n * blk:(n + 1) * blk, :] = s_blk
            b8 = jnp.max(s_blk.reshape(blk // 8, 8, blk), axis=0)
            m8 = b8 if m8 is None else jnp.maximum(m8, b8)
        return jnp.max(m8, axis=0, keepdims=True)

    def attend(qi, s_ref, m):
        cols = slice(qi * blk, (qi + 1) * blk)
        nk = (qi + 1) * blk
        p = jnp.exp2(s_ref[0:nk, :] - m).astype(BF16)
        o_aug = jnp.dot(vt_ref[:, 0:nk], p, preferred_element_type=F32)
        o_t = o_aug[0:hd] * (1.0 / o_aug[hd:hd + 1])
        o_ref[0, cols, :] = o_t.T.astype(o_ref.dtype)

    tiles = list(zip(MOBA_TILE_ORDER, s_refs))
    maxes = [scores(qi, s_ref) for qi, s_ref in tiles]
    for (qi, s_ref), m in zip(tiles, maxes):
        attend(qi, s_ref, m)


def _moba(proj3, rel_bias, q_norm_g, k_norm_g, bias_tiles, cast_weights, cast_tiles):
    hd = ATTN_HEAD_DIM
    head = lambda off: (lambda b, h: (b, 0, off // hd + h))
    cast_in, cast_out, cast_shapes = _cast_rider(
        cast_weights, cast_tiles, BATCH * ATTN_HEADS, lambda b, h: b * ATTN_HEADS + h)
    cast_bytes = sum(2 * (4 + 2) * w.size // (BATCH * ATTN_HEADS) for w in cast_weights)
    vmem = (2 * (4 * SEQ * hd * 2 + 2 * MOBA_BLOCK * MOBA_BLOCK * 4)
            + 2 * SEQ * hd * 2 + (hd + MOBA_ONES_ROWS) * SEQ * 2
            + len(MOBA_TILE_ORDER) * SEQ * MOBA_BLOCK * 4 + cast_bytes + 4 * MIB)
    outs = pl.pallas_call(
        functools.partial(_moba_kernel, tuple(cast_tiles)),
        grid=(BATCH, ATTN_HEADS),
        in_specs=[pl.BlockSpec(memory_space=pltpu.SMEM),
                  pl.BlockSpec((1, SEQ, hd), head(OFF_QA)),
                  pl.BlockSpec((1, SEQ, hd), head(OFF_KA)),
                  pl.BlockSpec((1, SEQ, hd), head(OFF_VA)),
                  pl.BlockSpec((1, hd), lambda b, h: (0, 0)),
                  pl.BlockSpec((1, hd), lambda b, h: (0, 0)),
                  pl.BlockSpec((1, 2 * MOBA_BLOCK, MOBA_BLOCK), lambda b, h: (h, 0, 0))]
                 + cast_in,
        out_specs=[pl.BlockSpec((1, SEQ, hd), lambda b, h: (b, 0, h))] + cast_out,
        out_shape=[jax.ShapeDtypeStruct((BATCH, SEQ, ATTN_WIDTH), BF16)] + cast_shapes,
        scratch_shapes=[pltpu.VMEM((SEQ, hd), BF16),
                        pltpu.VMEM((SEQ, hd), BF16),
                        pltpu.VMEM((hd + MOBA_ONES_ROWS, SEQ), BF16)]
                       + [pltpu.VMEM((SEQ, MOBA_BLOCK), F32)] * len(MOBA_TILE_ORDER),
        compiler_params=_params(("arbitrary", "arbitrary"), vmem),
        name="moba",
    )(rel_bias, proj3, proj3, proj3, q_norm_g, k_norm_g, bias_tiles, *cast_weights)
    return outs[0], outs[1:]


def _retention_kernel(cast_tiles, cd_ref, q_ref, k_ref, v_ref, gr_ref, cos_ref, sin_ref,
                      dmask_ref, qdec_ref, kdec_ref, gn_ref, *refs):
    n_cast = len(cast_tiles)
    cast_src, (o_ref, *cast_dst) = refs[:n_cast], refs[n_cast:2 * n_cast + 1]
    qf_ref, kf_ref, y_ref, kv_ref = refs[2 * n_cast + 1:]
    _run_cast_rider(cast_tiles, cast_src, cast_dst)
    h = pl.program_id(1)
    chunk_decay = cd_ref[h]
    half = RET_KEY_DIM // 2
    cos = cos_ref[...]
    sin = sin_ref[...]
    q = q_ref[0].astype(F32)
    k = k_ref[0].astype(F32)
    qf_ref[...] = q * cos + pltpu.roll(q, half, 1) * sin
    kf_ref[...] = (k * cos + pltpu.roll(k, half, 1) * sin) * (RET_KEY_DIM ** -0.5)

    nt = (((1,), (1,)), ((), ()))
    chunks = [slice(c * RET_CHUNK, (c + 1) * RET_CHUNK) for c in range(N_CHUNKS)]

    scores = [(lax.dot_general(qf_ref[rows, :].astype(BF16), kf_ref[rows, :].astype(BF16), nt,
                               preferred_element_type=F32) * dmask_ref[0]).astype(BF16)
              for rows in chunks]
    for c, rows in enumerate(chunks):
        vc = v_ref[0, rows, :]
        y_ref[rows, :] = jnp.dot(scores[c], vc, preferred_element_type=F32)
        kd_t = (kf_ref[rows, :] * kdec_ref[0]).T.astype(BF16)
        kv_ref[c] = jnp.dot(kd_t, vc, preferred_element_type=F32)

    state = jnp.zeros((RET_KEY_DIM, RET_VAL_DIM), F32)
    for c, rows in enumerate(chunks):
        y_ref[rows, :] += jnp.dot((qf_ref[rows, :] * qdec_ref[0]).astype(BF16),
                                  state.astype(BF16), preferred_element_type=F32)
        state = chunk_decay * state + kv_ref[c]

    for rows in chunks:
        y = y_ref[rows, :]
        mu = jnp.mean(y, axis=-1, keepdims=True)
        yc = y - mu
        var = jnp.mean(yc * yc, axis=-1, keepdims=True)
        yn = (yc * lax.rsqrt(var + EPS)) * gn_ref[...]
        o_ref[0, rows, :] = (yn * _silu(gr_ref[0, rows, :]).astype(F32)).astype(o_ref.dtype)


def _retention_tables():
    f32 = np.float32
    half = RET_KEY_DIM // 2
    freqs = np.power(f32(ROPE_BASE), -np.arange(half, dtype=f32) / f32(half))
    ang = np.arange(SEQ, dtype=f32)[:, None] * freqs[None, :]
    cos, sin = np.cos(ang), np.sin(ang)
    cos_full = np.concatenate([cos, cos], axis=-1)
    sin_signed = np.concatenate([-sin, sin], axis=-1)

    log_decay = np.log(f32(1.0) - np.power(f32(2.0), f32(-5.0) - np.arange(RET_HEADS, dtype=f32)))
    i = np.arange(RET_CHUNK, dtype=f32)
    diff = i[:, None] - i[None, :]
    ld = log_decay[:, None, None]
    inner_decay = np.where(diff >= 0, np.exp(ld * np.maximum(diff, f32(0.0))), f32(0.0))
    q_decay = np.exp(log_decay[:, None] * (i + f32(1.0)))
    k_decay = np.exp(log_decay[:, None] * (f32(RET_CHUNK - 1.0) - i))
    chunk_decay = np.exp(log_decay * f32(RET_CHUNK))
    bcast = lambda t: np.ascontiguousarray(
        np.broadcast_to(t[:, :, None], (RET_HEADS, RET_CHUNK, RET_KEY_DIM)))
    tables = (cos_full, sin_signed, inner_decay, bcast(q_decay), bcast(k_decay), chunk_decay)
    assert all(t.dtype == f32 for t in tables)
    return tuple(jnp.asarray(t) for t in tables)


def _retention(proj3, ret_norm_g, cast_weights, cast_tiles):
    dk, dv = RET_KEY_DIM, RET_VAL_DIM
    cos, sin, dmask, qdec, kdec, chunk_decay = _retention_tables()
    head = lambda off, w: (lambda b, h: (b, 0, off // w + h))
    per_head = lambda b, h: (h, 0, 0)
    cast_in, cast_out, cast_shapes = _cast_rider(
        cast_weights, cast_tiles, BATCH * RET_HEADS, lambda b, h: b * RET_HEADS + h)
    cast_bytes = sum(2 * (4 + 2) * w.size // (BATCH * RET_HEADS) for w in cast_weights)
    vmem = (2 * (2 * SEQ * dk * 2 + 3 * SEQ * dv * 2 + 2 * SEQ * dk * 4
                 + RET_CHUNK * (RET_CHUNK + 2 * dk) * 4)
            + 2 * SEQ * dk * 4 + SEQ * dv * 4 + N_CHUNKS * dk * dv * 4 + cast_bytes + 4 * MIB)
    outs = pl.pallas_call(
        functools.partial(_retention_kernel, tuple(cast_tiles)),
        grid=(BATCH, RET_HEADS),
        in_specs=[pl.BlockSpec(memory_space=pltpu.SMEM),
                  pl.BlockSpec((1, SEQ, dk), head(OFF_QR, dk)),
                  pl.BlockSpec((1, SEQ, dk), head(OFF_KR, dk)),
                  pl.BlockSpec((1, SEQ, dv), head(OFF_VR, dv)),
                  pl.BlockSpec((1, SEQ, dv), head(OFF_GR, dv)),
                  pl.BlockSpec((SEQ, dk), lambda b, h: (0, 0)),
                  pl.BlockSpec((SEQ, dk), lambda b, h: (0, 0)),
                  pl.BlockSpec((1, RET_CHUNK, RET_CHUNK), per_head),
                  pl.BlockSpec((1, RET_CHUNK, dk), per_head),
                  pl.BlockSpec((1, RET_CHUNK, dk), per_head),
                  pl.BlockSpec((1, dv), lambda b, h: (0, h))] + cast_in,
        out_specs=[pl.BlockSpec((1, SEQ, dv), lambda b, h: (b, 0, h))] + cast_out,
        out_shape=[jax.ShapeDtypeStruct((BATCH, SEQ, RET_V_WIDTH), BF16)] + cast_shapes,
        scratch_shapes=[pltpu.VMEM((SEQ, dk), F32), pltpu.VMEM((SEQ, dk), F32),
                        pltpu.VMEM((SEQ, dv), F32), pltpu.VMEM((N_CHUNKS, dk, dv), F32)],
        compiler_params=_params(("arbitrary", "arbitrary"), vmem),
        name="retention",
    )(chunk_decay, proj3, proj3, proj3, proj3, cos, sin, dmask, qdec, kdec, ret_norm_g,
      *cast_weights)
    return outs[0], outs[1:]


def _mixer_kernel(ya_ref, yr_ref, ga_ref, gb_ref, wa_ref, wr_ref, wo_ref, x_ref, mod_ref,
                  o_ref):
    j = pl.program_id(1)

    @pl.when(j == 0)
    def _():
        o_ref[...] = x_ref[...]

    a = jnp.dot(ya_ref[...], wa_ref[j], preferred_element_type=F32)
    r = jnp.dot(yr_ref[...], wr_ref[j], preferred_element_type=F32)
    merged = (_sigmoid(ga_ref[...].astype(F32)) * a
              + _sigmoid(gb_ref[...].astype(F32)) * r)
    wo_rows = pl.ds(pl.multiple_of(j * MIX_TN, MIX_TN), MIX_TN)
    o_ref[...] += mod_ref[0, GATE1:GATE1 + 1, :] * jnp.dot(
        merged.astype(BF16), wo_ref[wo_rows, :], preferred_element_type=F32)


def _mixer(ya2d, yr2d, proj2d, w_attn_br, w_ret_br, w_o, x2d, mod3):
    tm, tn = MIX_TM, MIX_TN
    nj = D_MODEL // tn
    assert w_attn_br.shape == (nj, ATTN_WIDTH, tn) and w_ret_br.shape == (nj, RET_V_WIDTH, tn)
    assert OFF_GA % tn == 0 and OFF_GB % tn == 0
    resident = lambda shape: pl.BlockSpec(shape, lambda i, j: (0,) * len(shape),
                                          pipeline_mode=pl.Buffered(1))
    vmem = (2 * tm * (ATTN_WIDTH + RET_V_WIDTH + 2 * tn) * 2
            + (ATTN_WIDTH + RET_V_WIDTH + D_MODEL) * D_MODEL * 2
            + 4 * tm * D_MODEL * 4 + 3 * tm * tn * 4 + 4 * MIB)
    return pl.pallas_call(
        _mixer_kernel,
        grid=(TOKENS // tm, D_MODEL // tn),
        in_specs=[pl.BlockSpec((tm, ATTN_WIDTH), lambda i, j: (i, 0)),
                  pl.BlockSpec((tm, RET_V_WIDTH), lambda i, j: (i, 0)),
                  pl.BlockSpec((tm, tn), lambda i, j: (i, OFF_GA // tn + j)),
                  pl.BlockSpec((tm, tn), lambda i, j: (i, OFF_GB // tn + j)),
                  resident((nj, ATTN_WIDTH, tn)),
                  resident((nj, RET_V_WIDTH, tn)),
                  resident((D_MODEL, D_MODEL)),
                  pl.BlockSpec((tm, D_MODEL), lambda i, j: (i, 0)),
                  pl.BlockSpec((1, N_MOD, D_MODEL), lambda i, j: (i // (SEQ // tm), 0, 0))],
        out_specs=pl.BlockSpec((tm, D_MODEL), lambda i, j: (i, 0)),
        out_shape=jax.ShapeDtypeStruct((TOKENS, D_MODEL), F32),
        compiler_params=_params(("arbitrary", "arbitrary"), vmem),
        name="mixer",
    )(ya2d, yr2d, proj2d, proj2d, w_attn_br, w_ret_br, w_o, x2d, mod3)


def _ffn_kernel(x_ref, halo_ref, mod_ref, g_ref, wu_ref, cp_ref, wd_ref,
                o_ref, h_ref, u_ref):
    i = pl.program_id(0)
    j = pl.program_id(1)
    tm, halo = FFN_TM, FFN_HALO

    @pl.when(j == 0)
    def _():
        o_ref[...] = jnp.zeros_like(o_ref)
        shift = mod_ref[0, SHIFT2:SHIFT2 + 1, :]
        gain = g_ref[...] * (1.0 + mod_ref[0, SCALE2:SCALE2 + 1, :])
        seq_start = (i % (SEQ // tm)) == 0
        h_halo = _rms_mod(halo_ref[...], gain, shift)
        h_ref[0:halo, :] = jnp.where(seq_start, 0.0, h_halo).astype(BF16)

        def body(r, carry):
            src = pl.ds(pl.multiple_of(r * NORM_ROWS, NORM_ROWS), NORM_ROWS)
            dst = pl.ds(pl.multiple_of(halo + r * NORM_ROWS, halo), NORM_ROWS)
            h_ref[dst, :] = _rms_mod(x_ref[src, :], gain, shift).astype(BF16)
            return carry

        lax.fori_loop(0, tm // NORM_ROWS, body, 0, unroll=NORM_UNROLL)

    def conv(half):
        w = wu_ref[0, :, half * FFN_TN:(half + 1) * FFN_TN]
        u_ref[half] = jnp.dot(h_ref[...], w, preferred_element_type=F32)
        y = cp_ref[half, CONV_WIDTH:CONV_WIDTH + 1, :]
        for t in range(CONV_WIDTH):
            lag = CONV_WIDTH - 1 - t
            y = y + cp_ref[half, t:t + 1, :] * u_ref[half, halo - lag:halo - lag + tm, :]
        return y

    gt = conv(1)
    val = conv(0)
    kw = FFN_TN // FFN_DOWN_SPLITS
    for kh in range(FFN_DOWN_SPLITS):
        cols = slice(kh * kw, (kh + 1) * kw)
        act = (_silu(gt[:, cols]) * val[:, cols]).astype(BF16)
        o_ref[...] += jnp.dot(act, wd_ref[cols, :], preferred_element_type=F32)

    @pl.when(j == pl.num_programs(1) - 1)
    def _():
        o_ref[...] = x_ref[...] + mod_ref[0, GATE2:GATE2 + 1, :] * o_ref[...]


def _ffn(x1, mod3, norm_g, w_up, conv_params, w_down):
    tm, tn, halo, nj = FFN_TM, FFN_TN, FFN_HALO, FFN_NJ
    assert w_up.shape == (nj, D_MODEL, 2 * tn)
    vmem = (4 * tm * D_MODEL * 4 + 2 * halo * D_MODEL * 4
            + 2 * 3 * D_MODEL * tn * 2
            + (tm + halo) * D_MODEL * 2 + 2 * (tm + halo) * tn * 4
            + tm * tn * 4 + 4 * MIB)
    return pl.pallas_call(
        _ffn_kernel,
        grid=(TOKENS // tm, nj),
        in_specs=[pl.BlockSpec((tm, D_MODEL), lambda i, j: (i, 0)),
                  pl.BlockSpec((halo, D_MODEL),
                               lambda i, j: (jnp.maximum(i * (tm // halo) - 1, 0), 0)),
                  pl.BlockSpec((1, N_MOD, D_MODEL), lambda i, j: (i // (SEQ // tm), 0, 0)),
                  pl.BlockSpec((1, D_MODEL), lambda i, j: (0, 0)),
                  pl.BlockSpec((1, D_MODEL, 2 * tn), lambda i, j: (j, 0, 0)),
                  pl.BlockSpec((2, CONV_WIDTH + 1, tn), lambda i, j: (0, 0, j)),
                  pl.BlockSpec((tn, D_MODEL), lambda i, j: (j, 0))],
        out_specs=pl.BlockSpec((tm, D_MODEL), lambda i, j: (i, 0)),
        out_shape=jax.ShapeDtypeStruct((TOKENS, D_MODEL), F32),
        scratch_shapes=[pltpu.VMEM((tm + halo, D_MODEL), BF16),
                        pltpu.VMEM((2, tm + halo, tn), F32)],
        compiler_params=_params(("arbitrary", "arbitrary"), vmem),
        name="ffn",
    )(x1, x1, mod3, norm_g, w_up, conv_params, w_down)


def kernel(x, c, w_ada, b_ada, norm1_g, w_in, q_norm_g, k_norm_g, rel_bias, ret_norm_g,
           w_attn_br, w_ret_br, w_o, norm2_g, w_up, conv_w, conv_b, w_down):
    assert x.shape == (BATCH, SEQ, D_MODEL) and w_ada.shape[0] == 1
    assert _far_bucket_is_last()
    layer = 0
    x2d = x.reshape(TOKENS, D_MODEL)

    c_pad = jnp.pad(c, ((0, F32_ROWS - BATCH), (0, 0)))
    mod = _modulation(c_pad, w_ada[layer], b_ada)[:BATCH]
    mod3 = mod.reshape(BATCH, N_MOD, D_MODEL)

    proj = _input_projection(_prenorm(x2d, mod3, norm1_g), w_in[layer])
    proj3 = proj.reshape(BATCH, SEQ, IN_WIDTH)

    bias_tiles = _relbias_tiles(rel_bias)
    yr, (w_attn_b, w_ret_b, w_o_b) = _retention(
        proj3, ret_norm_g, (w_attn_br[layer], w_ret_br[layer], w_o[layer]),
        (MIX_TN, MIX_TN, None))
    ya, (w_up_b, w_down_b) = _moba(proj3, rel_bias, q_norm_g, k_norm_g, bias_tiles,
                                   (w_up[layer], w_down[layer]), ((FFN_TN, 2), None))

    x1 = _mixer(ya.reshape(TOKENS, ATTN_WIDTH), yr.reshape(TOKENS, RET_V_WIDTH), proj,
                w_attn_b, w_ret_b, w_o_b, x2d, mod3)

    conv_params = jnp.concatenate([conv_w[layer], conv_b], axis=0).reshape(
        CONV_WIDTH + 1, 2, FFN_DIM).transpose(1, 0, 2)
    out = _ffn(x1, mod3, norm2_g, w_up_b, conv_params, w_down_b)
    return out.reshape(BATCH, SEQ, D_MODEL)
```

```python
import functools
import math

import numpy as np
import jax
import jax.numpy as jnp
from jax import lax
from jax.experimental import pallas as pl
from jax.experimental.pallas import tpu as pltpu

F32 = jnp.float32
BF16 = jnp.bfloat16

D_MODEL = 2048
BATCH = 4
SEQ = 2048
ATTN_HEADS = 8
ATTN_HEAD_DIM = 128
MOBA_BLOCK = 256
MOBA_TOPK = 3
REL_BUCKETS = 32
REL_MAX_DIST = 128
RET_HEADS = 8
RET_KEY_DIM = 128
RET_VAL_DIM = 256
RET_CHUNK = 128
ROPE_BASE = 10000.0
FFN_DIM = 5632
CONV_WIDTH = 3
EPS = 1e-6
N_MOD = 6

ATTN_WIDTH = ATTN_HEADS * ATTN_HEAD_DIM
RET_QK_WIDTH = RET_HEADS * RET_KEY_DIM
RET_V_WIDTH = RET_HEADS * RET_VAL_DIM
OFF_QA = 0
OFF_KA = OFF_QA + ATTN_WIDTH
OFF_VA = OFF_KA + ATTN_WIDTH
OFF_QR = OFF_VA + ATTN_WIDTH
OFF_KR = OFF_QR + RET_QK_WIDTH
OFF_VR = OFF_KR + RET_QK_WIDTH
OFF_GR = OFF_VR + RET_V_WIDTH
OFF_GA = OFF_GR + RET_V_WIDTH
OFF_GB = OFF_GA + D_MODEL
IN_WIDTH = OFF_GB + D_MODEL

TOKENS = BATCH * SEQ
N_BLOCKS = SEQ // MOBA_BLOCK
N_CHUNKS = SEQ // RET_CHUNK
MASK_VALUE = -1e30
LOG2E = math.log2(math.e)
F32_ROWS = 8
BF16_ROWS = 16
MOBA_ONES_ROWS = BF16_ROWS
MOBA_TILE_ORDER = tuple(range(N_BLOCKS))
MIB = 1024 * 1024

SHIFT1, SCALE1, GATE1, SHIFT2, SCALE2, GATE2 = range(N_MOD)

MOD_TN = 1024
MOD_STREAMS = 2
PRENORM_TM = 1024
PRENORM_STREAMS = 2
INPROJ_TM, INPROJ_TN = 2048, 1024
MIX_TM = 512
MIX_GATE_TILE = 1024
FFN_TM, FFN_TN = 1024, 512
FFN_HALO = BF16_ROWS
FFN_NJ = FFN_DIM // FFN_TN
FFN_DOWN_SPLITS = 2
NORM_ROWS = 16
NORM_UNROLL = 16


def _params(semantics, vmem_bytes):
    return pltpu.CompilerParams(dimension_semantics=semantics,
                                vmem_limit_bytes=int(vmem_bytes))


def _sigmoid(v):
    return 0.5 + 0.5 * jnp.tanh(0.5 * v)


def _silu(v):
    hv = 0.5 * v
    return hv + hv * jnp.tanh(hv)


def _cast_rider(weights, col_tiles, steps, step_of):
    in_specs, out_specs, out_shapes = [], [], []
    for w, tile in zip(weights, col_tiles):
        rows, cols = w.shape
        assert rows % (steps * BF16_ROWS) == 0, (rows, steps)
        slab = rows // steps
        in_specs.append(pl.BlockSpec((slab, cols), lambda *g: (step_of(*g), 0)))
        if tile is None:
            out_specs.append(pl.BlockSpec((slab, cols), lambda *g: (step_of(*g), 0)))
            out_shapes.append(pltpu.HBM((rows, cols), BF16))
        else:
            width = tile if isinstance(tile, int) else tile[0] * tile[1]
            out_specs.append(pl.BlockSpec((cols // width, slab, width),
                                          lambda *g: (0, step_of(*g), 0)))
            out_shapes.append(pltpu.HBM((cols // width, rows, width), BF16))
    return in_specs, out_specs, out_shapes


def _run_cast_rider(col_tiles, src_refs, dst_refs):
    for tile, src, dst in zip(col_tiles, src_refs, dst_refs):
        if tile is None:
            dst[...] = src[...].astype(BF16)
        elif isinstance(tile, int):
            for t in range(dst.shape[0]):
                dst[t] = src[:, t * tile:(t + 1) * tile].astype(BF16)
        else:
            t_w, parts = tile
            part_w = src.shape[1] // parts
            for t in range(dst.shape[0]):
                for p in range(parts):
                    lo = p * part_w + t * t_w
                    dst[t, :, p * t_w:(p + 1) * t_w] = src[:, lo:lo + t_w].astype(BF16)


def _rms_mod(x, gain, shift):
    ms = jnp.mean(x * x, axis=-1, keepdims=True)
    return (x * lax.rsqrt(ms + EPS)) * gain + shift


def _split_bf16(v):
    hi = v.astype(BF16)
    lo = (v - hi.astype(F32)).astype(BF16)
    return hi, lo


def _mod_kernel(c_ref, b_ref, *refs):
    *w_refs, o_ref = refs
    rows = c_ref.shape[0]
    s_hi, s_lo = _split_bf16(_silu(c_ref[...]))
    s_both = jnp.concatenate([s_hi, s_lo], axis=0)
    width = MOD_TN // MOD_STREAMS
    for st, w_ref in enumerate(w_refs):
        cols = slice(st * width, (st + 1) * width)
        w_hi, w_lo = _split_bf16(w_ref[...])
        both = jnp.dot(s_both, w_hi, preferred_element_type=F32)
        cross = jnp.dot(s_hi, w_lo, preferred_element_type=F32)
        o_ref[:, cols] = (both[0:rows] + both[rows:]) + cross + b_ref[:, cols]


def _modulation(c_pad, w_ada, b_ada):
    rows = c_pad.shape[0]
    n = w_ada.shape[1]
    width = MOD_TN // MOD_STREAMS
    stream = lambda st: pl.BlockSpec((D_MODEL, width), lambda j: (0, MOD_STREAMS * j + st))
    return pl.pallas_call(
        _mod_kernel,
        grid=(n // MOD_TN,),
        in_specs=[pl.BlockSpec((rows, D_MODEL), lambda j: (0, 0)),
                  pl.BlockSpec((1, MOD_TN), lambda j: (0, j))]
                 + [stream(st) for st in range(MOD_STREAMS)],
        out_specs=pl.BlockSpec((rows, MOD_TN), lambda j: (0, j)),
        out_shape=jax.ShapeDtypeStruct((rows, n), F32),
        compiler_params=_params(("arbitrary",), 2 * D_MODEL * MOD_TN * 4 + 8 * MIB),
        name="mod",
    )(c_pad, b_ada, *([w_ada] * MOD_STREAMS))


def _prenorm_kernel(mod_ref, g_ref, *refs):
    *x_refs, h_ref = refs
    shift = mod_ref[0, SHIFT1:SHIFT1 + 1, :]
    gain = g_ref[...] * (1.0 + mod_ref[0, SCALE1:SCALE1 + 1, :])
    part = PRENORM_TM // PRENORM_STREAMS
    for st, x_ref in enumerate(x_refs):
        def body(r, carry, x_ref=x_ref, base=st * part):
            src = pl.ds(pl.multiple_of(r * NORM_ROWS, NORM_ROWS), NORM_ROWS)
            dst = pl.ds(pl.multiple_of(base + r * NORM_ROWS, NORM_ROWS), NORM_ROWS)
            h_ref[dst, :] = _rms_mod(x_ref[src, :], gain, shift).astype(BF16)
            return carry

        lax.fori_loop(0, part // NORM_ROWS, body, 0, unroll=NORM_UNROLL)


def _prenorm(x2d, mod3, norm_g):
    tm = PRENORM_TM
    part = tm // PRENORM_STREAMS
    stream = lambda st: pl.BlockSpec((part, D_MODEL), lambda i: (PRENORM_STREAMS * i + st, 0))
    return pl.pallas_call(
        _prenorm_kernel,
        grid=(TOKENS // tm,),
        in_specs=[pl.BlockSpec((1, N_MOD, D_MODEL), lambda i: (i // (SEQ // tm), 0, 0)),
                  pl.BlockSpec((1, D_MODEL), lambda i: (0, 0))]
                 + [stream(st) for st in range(PRENORM_STREAMS)],
        out_specs=pl.BlockSpec((tm, D_MODEL), lambda i: (i, 0)),
        out_shape=pltpu.HBM((TOKENS, D_MODEL), BF16),
        compiler_params=_params(("arbitrary",), 2 * tm * D_MODEL * (4 + 2) + 4 * MIB),
        name="prenorm",
    )(mod3, norm_g, *([x2d] * PRENORM_STREAMS))


def _inproj_kernel(h_ref, w_ref, o_ref):
    o_ref[...] = jnp.dot(h_ref[...], w_ref[...].astype(BF16),
                         preferred_element_type=F32).astype(o_ref.dtype)


def _input_projection(h2d, w_in):
    tm, tn = INPROJ_TM, INPROJ_TN
    vmem = (2 * tm * D_MODEL * 2 + 2 * D_MODEL * tn * 4 + 2 * tm * tn * 2
            + D_MODEL * tn * 2 + tm * tn * 4 + 4 * MIB)
    return pl.pallas_call(
        _inproj_kernel,
        grid=(IN_WIDTH // tn, TOKENS // tm),
        in_specs=[pl.BlockSpec((tm, D_MODEL), lambda j, i: (i, 0)),
                  pl.BlockSpec((D_MODEL, tn), lambda j, i: (0, j))],
        out_specs=pl.BlockSpec((tm, tn), lambda j, i: (i, j)),
        out_shape=jax.ShapeDtypeStruct((TOKENS, IN_WIDTH), BF16),
        compiler_params=_params(("arbitrary", "arbitrary"), vmem),
        name="inproj",
    )(h2d, w_in)


def _relbias_kernel(rb_ref, o_ref):
    shape = (2 * MOBA_BLOCK, MOBA_BLOCK)
    key = lax.broadcasted_iota(jnp.int32, shape, 0)
    qry = lax.broadcasted_iota(jnp.int32, shape, 1)
    dist = qry - key + MOBA_BLOCK
    n = jnp.maximum(dist, 0)
    max_exact = REL_BUCKETS // 2
    nf = jnp.maximum(n, 1).astype(F32)
    large = max_exact + (jnp.log(nf / max_exact) / math.log(REL_MAX_DIST / max_exact)
                         * (REL_BUCKETS - max_exact)).astype(jnp.int32)
    large = jnp.minimum(large, REL_BUCKETS - 1)
    bucket = jnp.where(n < max_exact, n, large)
    for h in range(ATTN_HEADS):
        bias = jnp.zeros(shape, F32)
        for b in range(REL_BUCKETS):
            bias = jnp.where(bucket == b, rb_ref[b, h], bias)
        o_ref[h] = jnp.where(dist >= 0, bias * LOG2E, MASK_VALUE)


def _relbias_tiles(rel_bias):
    tile_bytes = 2 * MOBA_BLOCK * MOBA_BLOCK * 4
    return pl.pallas_call(
        _relbias_kernel,
        grid=(1,),
        in_specs=[pl.BlockSpec(memory_space=pltpu.SMEM)],
        out_specs=pl.BlockSpec((ATTN_HEADS, 2 * MOBA_BLOCK, MOBA_BLOCK), lambda i: (0, 0, 0)),
        out_shape=jax.ShapeDtypeStruct((ATTN_HEADS, 2 * MOBA_BLOCK, MOBA_BLOCK), F32),
        compiler_params=_params(("arbitrary",), (2 * ATTN_HEADS + 10) * tile_bytes),
        name="relbias",
    )(rel_bias)


def _far_bucket_is_last():
    d = np.arange(MOBA_BLOCK + 1, SEQ, dtype=np.float32)
    max_exact = REL_BUCKETS // 2
    large = max_exact + (np.log(d / max_exact) / math.log(REL_MAX_DIST / max_exact)
                         * (REL_BUCKETS - max_exact)).astype(np.int32)
    return bool(np.all(np.minimum(large, REL_BUCKETS - 1) == REL_BUCKETS - 1))


def _moba_kernel(cast_tiles, rb_ref, q_ref, k_ref, v_ref, gq_ref, gk_ref, bt_ref, *refs):
    n_cast = len(cast_tiles)
    cast_src, (o_ref, *cast_dst) = refs[:n_cast], refs[n_cast:2 * n_cast + 1]
    qb_ref, kb_ref, vt_ref, *s_refs = refs[2 * n_cast + 1:]
    _run_cast_rider(cast_tiles, cast_src, cast_dst)

    h = pl.program_id(1)
    far_bias = rb_ref[REL_BUCKETS - 1, h] * LOG2E
    blk, hd = MOBA_BLOCK, ATTN_HEAD_DIM
    nt = (((1,), (1,)), ((), ()))

    q = q_ref[0].astype(F32)
    k = k_ref[0].astype(F32)
    qn = (q * lax.rsqrt(jnp.mean(q * q, axis=-1, keepdims=True) + EPS)) * gq_ref[...]
    kn = (k * lax.rsqrt(jnp.mean(k * k, axis=-1, keepdims=True) + EPS)) * gk_ref[...]
    qb_ref[...] = (qn * (hd ** -0.5 * LOG2E)).astype(BF16)
    kb_ref[...] = kn.astype(BF16)
    vt_ref[0:hd, :] = v_ref[0].astype(F32).T.astype(BF16)
    vt_ref[hd:, :] = jnp.ones((MOBA_ONES_ROWS, SEQ), BF16)

    k_mean = jnp.concatenate(
        [jnp.sum(kn[n * blk:(n + 1) * blk], axis=0, keepdims=True) for n in range(N_BLOCKS)],
        axis=0) * (1.0 / blk)
    gate = lax.dot_general(k_mean, qn, nt, preferred_element_type=F32,
                           precision=lax.Precision.HIGHEST)
    row = lax.broadcasted_iota(jnp.int32, (N_BLOCKS, blk), 0)

    def scores(qi, s_ref):
        cols = slice(qi * blk, (qi + 1) * blk)
        nk = (qi + 1) * blk

        mask_add = None
        if qi > MOBA_TOPK:
            g = gate[:, cols]
            rank = jnp.zeros((N_BLOCKS, blk), F32)
            for m in range(qi):
                gm = g[m:m + 1, :]
                beats = (gm > g) | ((gm == g) & (row > m))
                rank = rank + jnp.where(beats, 1.0, 0.0)
            mask_add = jnp.where(rank < MOBA_TOPK, 0.0, MASK_VALUE)

        s_all = lax.dot_general(kb_ref[0:nk, :], qb_ref[cols, :], nt,
                                preferred_element_type=F32)
        m8 = None
        for n in range(qi + 1):
            s_blk = s_all[n * blk:(n + 1) * blk]
            if n == qi:
                s_blk = s_blk + bt_ref[0, blk:2 * blk, :]
            else:
                if n == qi - 1:
                    s_blk = s_blk + bt_ref[0, 0:blk, :]
                    if mask_add is not None:
                        s_blk = s_blk + mask_add[n:n + 1, :]
                elif mask_add is not None:
                    s_blk = s_blk + (mask_add[n:n + 1, :] + far_bias)
                else:
                    s_blk = s_blk + far_bias
            s_ref[n * blk:(n + 1) * blk, :] = s_blk
            b8 = jnp.max(s_blk.reshape(blk // 8, 8, blk), axis=0)
            m8 = b8 if m8 is None else jnp.maximum(m8, b8)
        return jnp.max(m8, axis=0, keepdims=True)

    def attend(qi, s_ref, m):
        cols = slice(qi * blk, (qi + 1) * blk)
        nk = (qi + 1) * blk
        p = jnp.exp2(s_ref[0:nk, :] - m).astype(BF16)
        o_aug = jnp.dot(vt_ref[:, 0:nk], p, preferred_element_type=F32)
        o_t = o_aug[0:hd] * (1.0 / o_aug[hd:hd + 1])
        o_ref[0, cols, :] = o_t.T.astype(o_ref.dtype)

    tiles = list(zip(MOBA_TILE_ORDER, s_refs))
    maxes = [scores(qi, s_ref) for qi, s_ref in tiles]
    for (qi, s_ref), m in zip(tiles, maxes):
        attend(qi, s_ref, m)


def _moba(proj3, rel_bias, q_norm_g, k_norm_g, bias_tiles, cast_weights, cast_tiles):
    hd = ATTN_HEAD_DIM
    head = lambda off: (lambda b, h: (b, 0, off // hd + h))
    cast_in, cast_out, cast_shapes = _cast_rider(
        cast_weights, cast_tiles, BATCH * ATTN_HEADS, lambda b, h: b * ATTN_HEADS + h)
    cast_bytes = sum(2 * (4 + 2) * w.size // (BATCH * ATTN_HEADS) for w in cast_weights)
    vmem = (2 * (4 * SEQ * hd * 2 + 2 * MOBA_BLOCK * MOBA_BLOCK * 4)
            + 2 * SEQ * hd * 2 + (hd + MOBA_ONES_ROWS) * SEQ * 2
            + len(MOBA_TILE_ORDER) * SEQ * MOBA_BLOCK * 4 + cast_bytes + 4 * MIB)
    outs = pl.pallas_call(
        functools.partial(_moba_kernel, tuple(cast_tiles)),
        grid=(BATCH, ATTN_HEADS),
        in_specs=[pl.BlockSpec(memory_space=pltpu.SMEM),
                  pl.BlockSpec((1, SEQ, hd), head(OFF_QA)),
                  pl.BlockSpec((1, SEQ, hd), head(OFF_KA)),
                  pl.BlockSpec((1, SEQ, hd), head(OFF_VA)),
                  pl.BlockSpec((1, hd), lambda b, h: (0, 0)),
                  pl.BlockSpec((1, hd), lambda b, h: (0, 0)),
                  pl.BlockSpec((1, 2 * MOBA_BLOCK, MOBA_BLOCK), lambda b, h: (h, 0, 0))]
                 + cast_in,
        out_specs=[pl.BlockSpec((1, SEQ, hd), lambda b, h: (b, 0, h))] + cast_out,
        out_shape=[jax.ShapeDtypeStruct((BATCH, SEQ, ATTN_WIDTH), BF16)] + cast_shapes,
        scratch_shapes=[pltpu.VMEM((SEQ, hd), BF16),
                        pltpu.VMEM((SEQ, hd), BF16),
                        pltpu.VMEM((hd + MOBA_ONES_ROWS, SEQ), BF16)]
                       + [pltpu.VMEM((SEQ, MOBA_BLOCK), F32)] * len(MOBA_TILE_ORDER),
        compiler_params=_params(("arbitrary", "arbitrary"), vmem),
        name="moba",
    )(rel_bias, proj3, proj3, proj3, q_norm_g, k_norm_g, bias_tiles, *cast_weights)
    return outs[0], outs[1:]


def _retention_kernel(cast_tiles, cd_ref, q_ref, k_ref, v_ref, gr_ref, cos_ref, sin_ref,
                      dmask_ref, qdec_ref, kdec_ref, gn_ref, *refs):
    n_cast = len(cast_tiles)
    cast_src, (o_ref, *cast_dst) = refs[:n_cast], refs[n_cast:2 * n_cast + 1]
    qf_ref, kf_ref, y_ref, kv_ref = refs[2 * n_cast + 1:]
    _run_cast_rider(cast_tiles, cast_src, cast_dst)
    h = pl.program_id(1)
    chunk_decay = cd_ref[h]
    half = RET_KEY_DIM // 2
    cos = cos_ref[...]
    sin = sin_ref[...]
    q = q_ref[0].astype(F32)
    k = k_ref[0].astype(F32)
    qf_ref[...] = q * cos + pltpu.roll(q, half, 1) * sin
    kf_ref[...] = (k * cos + pltpu.roll(k, half, 1) * sin) * (RET_KEY_DIM ** -0.5)

    nt = (((1,), (1,)), ((), ()))
    chunks = [slice(c * RET_CHUNK, (c + 1) * RET_CHUNK) for c in range(N_CHUNKS)]

    scores = [(lax.dot_general(qf_ref[rows, :].astype(BF16), kf_ref[rows, :].astype(BF16), nt,
                               preferred_element_type=F32) * dmask_ref[0]).astype(BF16)
              for rows in chunks]
    for c, rows in enumerate(chunks):
        vc = v_ref[0, rows, :]
        y_ref[rows, :] = jnp.dot(scores[c], vc, preferred_element_type=F32)
        kd_t = (kf_ref[rows, :] * kdec_ref[0]).T.astype(BF16)
        kv_ref[c] = jnp.dot(kd_t, vc, preferred_element_type=F32)

    state = jnp.zeros((RET_KEY_DIM, RET_VAL_DIM), F32)
    for c, rows in enumerate(chunks):
        y_ref[rows, :] += jnp.dot((qf_ref[rows, :] * qdec_ref[0]).astype(BF16),
                                  state.astype(BF16), preferred_element_type=F32)
        state = chunk_decay * state + kv_ref[c]

    for rows in chunks:
        y = y_ref[rows, :]
        mu = jnp.mean(y, axis=-1, keepdims=True)
        yc = y - mu
        var = jnp.mean(yc * yc, axis=-1, keepdims=True)
        yn = (yc * lax.rsqrt(var + EPS)) * gn_ref[...]
        o_ref[0, rows, :] = (yn * _silu(gr_ref[0, rows, :]).astype(F32)).astype(o_ref.dtype)


def _retention_tables():
    f32 = np.float32
    half = RET_KEY_DIM // 2
    freqs = np.power(f32(ROPE_BASE), -np.arange(half, dtype=f32) / f32(half))
    ang = np.arange(SEQ, dtype=f32)[:, None] * freqs[None, :]
    cos, sin = np.cos(ang), np.sin(ang)
    cos_full = np.concatenate([cos, cos], axis=-1)
    sin_signed = np.concatenate([-sin, sin], axis=-1)

    log_decay = np.log(f32(1.0) - np.power(f32(2.0), f32(-5.0) - np.arange(RET_HEADS, dtype=f32)))
    i = np.arange(RET_CHUNK, dtype=f32)
    diff = i[:, None] - i[None, :]
    ld = log_decay[:, None, None]
    inner_decay = np.where(diff >= 0, np.exp(ld * np.maximum(diff, f32(0.0))), f32(0.0))
    q_decay = np.exp(log_decay[:, None] * (i + f32(1.0)))
    k_decay = np.exp(log_decay[:, None] * (f32(RET_CHUNK - 1.0) - i))
    chunk_decay = np.exp(log_decay * f32(RET_CHUNK))
    bcast = lambda t: np.ascontiguousarray(
        np.broadcast_to(t[:, :, None], (RET_HEADS, RET_CHUNK, RET_KEY_DIM)))
    tables = (cos_full, sin_signed, inner_decay, bcast(q_decay), bcast(k_decay), chunk_decay)
    assert all(t.dtype == f32 for t in tables)
    return tuple(jnp.asarray(t) for t in tables)


def _retention(proj3, ret_norm_g, cast_weights, cast_tiles):
    dk, dv = RET_KEY_DIM, RET_VAL_DIM
    cos, sin, dmask, qdec, kdec, chunk_decay = _retention_tables()
    head = lambda off, w: (lambda b, h: (b, 0, off // w + h))
    per_head = lambda b, h: (h, 0, 0)
    cast_in, cast_out, cast_shapes = _cast_rider(
        cast_weights, cast_tiles, BATCH * RET_HEADS, lambda b, h: b * RET_HEADS + h)
    cast_bytes = sum(2 * (4 + 2) * w.size // (BATCH * RET_HEADS) for w in cast_weights)
    vmem = (2 * (2 * SEQ * dk * 2 + 3 * SEQ * dv * 2 + 2 * SEQ * dk * 4
                 + RET_CHUNK * (RET_CHUNK + 2 * dk) * 4)
            + 2 * SEQ * dk * 4 + SEQ * dv * 4 + N_CHUNKS * dk * dv * 4 + cast_bytes + 4 * MIB)
    outs = pl.pallas_call(
        functools.partial(_retention_kernel, tuple(cast_tiles)),
        grid=(BATCH, RET_HEADS),
        in_specs=[pl.BlockSpec(memory_space=pltpu.SMEM),
                  pl.BlockSpec((1, SEQ, dk), head(OFF_QR, dk)),
                  pl.BlockSpec((1, SEQ, dk), head(OFF_KR, dk)),
                  pl.BlockSpec((1, SEQ, dv), head(OFF_VR, dv)),
                  pl.BlockSpec((1, SEQ, dv), head(OFF_GR, dv)),
                  pl.BlockSpec((SEQ, dk), lambda b, h: (0, 0)),
                  pl.BlockSpec((SEQ, dk), lambda b, h: (0, 0)),
                  pl.BlockSpec((1, RET_CHUNK, RET_CHUNK), per_head),
                  pl.BlockSpec((1, RET_CHUNK, dk), per_head),
                  pl.BlockSpec((1, RET_CHUNK, dk), per_head),
                  pl.BlockSpec((1, dv), lambda b, h: (0, h))] + cast_in,
        out_specs=[pl.BlockSpec((1, SEQ, dv), lambda b, h: (b, 0, h))] + cast_out,
        out_shape=[jax.ShapeDtypeStruct((BATCH, SEQ, RET_V_WIDTH), BF16)] + cast_shapes,
        scratch_shapes=[pltpu.VMEM((SEQ, dk), F32), pltpu.VMEM((SEQ, dk), F32),
                        pltpu.VMEM((SEQ, dv), F32), pltpu.VMEM((N_CHUNKS, dk, dv), F32)],
        compiler_params=_params(("arbitrary", "arbitrary"), vmem),
        name="retention",
    )(chunk_decay, proj3, proj3, proj3, proj3, cos, sin, dmask, qdec, kdec, ret_norm_g,
      *cast_weights)
    return outs[0], outs[1:]


def _mixer_kernel(ya_ref, yr_ref, wa_ref, wr_ref, wo_ref, x_ref, mod_ref, *refs):
    n_gate = D_MODEL // MIX_GATE_TILE
    ga_refs, gb_refs, o_ref = refs[:n_gate], refs[n_gate:2 * n_gate], refs[2 * n_gate]
    a = jnp.dot(ya_ref[...], wa_ref[...], preferred_element_type=F32)
    r = jnp.dot(yr_ref[...], wr_ref[...], preferred_element_type=F32)
    merged = []
    for c, (ga_ref, gb_ref) in enumerate(zip(ga_refs, gb_refs)):
        cols = slice(c * MIX_GATE_TILE, (c + 1) * MIX_GATE_TILE)
        merged.append((_sigmoid(ga_ref[...].astype(F32)) * a[:, cols]
                       + _sigmoid(gb_ref[...].astype(F32)) * r[:, cols]).astype(BF16))
    out = jnp.dot(jnp.concatenate(merged, axis=-1), wo_ref[...], preferred_element_type=F32)
    o_ref[...] = x_ref[...] + mod_ref[0, GATE1:GATE1 + 1, :] * out


def _mixer(ya2d, yr2d, proj2d, w_attn_br, w_ret_br, w_o, x2d, mod3):
    tm, gt = MIX_TM, MIX_GATE_TILE
    assert OFF_GA % gt == 0 and OFF_GB % gt == 0
    n_gate = D_MODEL // gt
    resident = lambda shape: pl.BlockSpec(shape, lambda i: (0,) * len(shape),
                                          pipeline_mode=pl.Buffered(1))
    gate_tile = lambda off, c: pl.BlockSpec((tm, gt), lambda i: (i, off // gt + c))
    vmem = (2 * tm * (ATTN_WIDTH + RET_V_WIDTH + 2 * D_MODEL) * 2
            + (ATTN_WIDTH + RET_V_WIDTH + D_MODEL) * D_MODEL * 2
            + 4 * tm * D_MODEL * 4 + 2 * tm * D_MODEL * 4 + MIB)
    return pl.pallas_call(
        _mixer_kernel,
        grid=(TOKENS // tm,),
        in_specs=[pl.BlockSpec((tm, ATTN_WIDTH), lambda i: (i, 0)),
                  pl.BlockSpec((tm, RET_V_WIDTH), lambda i: (i, 0)),
                  resident((ATTN_WIDTH, D_MODEL)),
                  resident((RET_V_WIDTH, D_MODEL)),
                  resident((D_MODEL, D_MODEL)),
                  pl.BlockSpec((tm, D_MODEL), lambda i: (i, 0)),
                  pl.BlockSpec((1, N_MOD, D_MODEL), lambda i: (i // (SEQ // tm), 0, 0))]
                 + [gate_tile(OFF_GA, c) for c in range(n_gate)]
                 + [gate_tile(OFF_GB, c) for c in range(n_gate)],
        out_specs=pl.BlockSpec((tm, D_MODEL), lambda i: (i, 0)),
        out_shape=jax.ShapeDtypeStruct((TOKENS, D_MODEL), F32),
        compiler_params=_params(("arbitrary",), vmem),
        name="mixer",
    )(ya2d, yr2d, w_attn_br, w_ret_br, w_o, x2d, mod3, *([proj2d] * (2 * n_gate)))


def _ffn_kernel(x_ref, halo_ref, mod_ref, g_ref, wu_ref, cp_ref, wd_ref,
                o_ref, h_ref, u_ref):
    i = pl.program_id(0)
    j = pl.program_id(1)
    tm, halo = FFN_TM, FFN_HALO

    @pl.when(j == 0)
    def _():
        o_ref[...] = jnp.zeros_like(o_ref)
        shift = mod_ref[0, SHIFT2:SHIFT2 + 1, :]
        gain = g_ref[...] * (1.0 + mod_ref[0, SCALE2:SCALE2 + 1, :])
        seq_start = (i % (SEQ // tm)) == 0
        h_halo = _rms_mod(halo_ref[...], gain, shift)
        h_ref[0:halo, :] = jnp.where(seq_start, 0.0, h_halo).astype(BF16)

        def body(r, carry):
            src = pl.ds(pl.multiple_of(r * NORM_ROWS, NORM_ROWS), NORM_ROWS)
            dst = pl.ds(pl.multiple_of(halo + r * NORM_ROWS, halo), NORM_ROWS)
            h_ref[dst, :] = _rms_mod(x_ref[src, :], gain, shift).astype(BF16)
            return carry

        lax.fori_loop(0, tm // NORM_ROWS, body, 0, unroll=NORM_UNROLL)

    def conv(half):
        w = wu_ref[0, :, half * FFN_TN:(half + 1) * FFN_TN]
        u_ref[half] = jnp.dot(h_ref[...], w, preferred_element_type=F32)
        y = cp_ref[half, CONV_WIDTH:CONV_WIDTH + 1, :]
        for t in range(CONV_WIDTH):
            lag = CONV_WIDTH - 1 - t
            y = y + cp_ref[half, t:t + 1, :] * u_ref[half, halo - lag:halo - lag + tm, :]
        return y

    gt = conv(1)
    val = conv(0)
    kw = FFN_TN // FFN_DOWN_SPLITS
    for kh in range(FFN_DOWN_SPLITS):
        cols = slice(kh * kw, (kh + 1) * kw)
        act = (_silu(gt[:, cols]) * val[:, cols]).astype(BF16)
        o_ref[...] += jnp.dot(act, wd_ref[cols, :], preferred_element_type=F32)

    @pl.when(j == pl.num_programs(1) - 1)
    def _():
        o_ref[...] = x_ref[...] + mod_ref[0, GATE2:GATE2 + 1, :] * o_ref[...]


def _ffn(x1, mod3, norm_g, w_up, conv_params, w_down):
    tm, tn, halo, nj = FFN_TM, FFN_TN, FFN_HALO, FFN_NJ
    assert w_up.shape == (nj, D_MODEL, 2 * tn)
    vmem = (4 * tm * D_MODEL * 4 + 2 * halo * D_MODEL * 4
            + 2 * 3 * D_MODEL * tn * 2
            + (tm + halo) * D_MODEL * 2 + 2 * (tm + halo) * tn * 4
            + tm * tn * 4 + 4 * MIB)
    return pl.pallas_call(
        _ffn_kernel,
        grid=(TOKENS // tm, nj),
        in_specs=[pl.BlockSpec((tm, D_MODEL), lambda i, j: (i, 0)),
                  pl.BlockSpec((halo, D_MODEL),
                               lambda i, j: (jnp.maximum(i * (tm // halo) - 1, 0), 0)),
                  pl.BlockSpec((1, N_MOD, D_MODEL), lambda i, j: (i // (SEQ // tm), 0, 0)),
                  pl.BlockSpec((1, D_MODEL), lambda i, j: (0, 0)),
                  pl.BlockSpec((1, D_MODEL, 2 * tn), lambda i, j: (j, 0, 0)),
                  pl.BlockSpec((2, CONV_WIDTH + 1, tn), lambda i, j: (0, 0, j)),
                  pl.BlockSpec((tn, D_MODEL), lambda i, j: (j, 0))],
        out_specs=pl.BlockSpec((tm, D_MODEL), lambda i, j: (i, 0)),
        out_shape=jax.ShapeDtypeStruct((TOKENS, D_MODEL), F32),
        scratch_shapes=[pltpu.VMEM((tm + halo, D_MODEL), BF16),
                        pltpu.VMEM((2, tm + halo, tn), F32)],
        compiler_params=_params(("arbitrary", "arbitrary"), vmem),
        name="ffn",
    )(x1, x1, mod3, norm_g, w_up, conv_params, w_down)


def kernel(x, c, w_ada, b_ada, norm1_g, w_in, q_norm_g, k_norm_g, rel_bias, ret_norm_g,
           w_attn_br, w_ret_br, w_o, norm2_g, w_up, conv_w, conv_b, w_down):
    assert x.shape == (BATCH, SEQ, D_MODEL) and w_ada.shape[0] == 1
    assert _far_bucket_is_last()
    layer = 0
    x2d = x.reshape(TOKENS, D_MODEL)

    c_pad = jnp.pad(c, ((0, F32_ROWS - BATCH), (0, 0)))
    mod = _modulation(c_pad, w_ada[layer], b_ada)[:BATCH]
    mod3 = mod.reshape(BATCH, N_MOD, D_MODEL)

    proj = _input_projection(_prenorm(x2d, mod3, norm1_g), w_in[layer])
    proj3 = proj.reshape(BATCH, SEQ, IN_WIDTH)

    bias_tiles = _relbias_tiles(rel_bias)
    yr, (w_attn_b, w_ret_b, w_o_b) = _retention(
        proj3, ret_norm_g, (w_attn_br[layer], w_ret_br[layer], w_o[layer]),
        (None, None, None))
    ya, (w_up_b, w_down_b) = _moba(proj3, rel_bias, q_norm_g, k_norm_g, bias_tiles,
                                   (w_up[layer], w_down[layer]), ((FFN_TN, 2), None))

    x1 = _mixer(ya.reshape(TOKENS, ATTN_WIDTH), yr.reshape(TOKENS, RET_V_WIDTH), proj,
                w_attn_b, w_ret_b, w_o_b, x2d, mod3)

    conv_params = jnp.concatenate([conv_w[layer], conv_b], axis=0).reshape(
        CONV_WIDTH + 1, 2, FFN_DIM).transpose(1, 0, 2)
    out = _ffn(x1, mod3, norm2_g, w_up_b, conv_params, w_down_b)
    return out.reshape(BATCH, SEQ, D_MODEL)
```
